```python
import jax
import jax.numpy as jnp
from jax import lax
import numpy as np

D_MODEL = 1024
BATCH = 32
SEQ = 2048
DEPTH = 1
DEC_BATCH = 1
DEC_SEQ = 16384
PAST_LEN = 128

N_META = 16
H_A = 8
Q_LORA = 256
KV_LORA = 128
NOPE = 64
ROPE = 32
QK_DIM = NOPE + ROPE
V_DIM = 64
ROPE_THETA = 10000.0
Q_BLOCK = 128
H_B = 4
DH_B = 128
CHUNK = 64
NEG = -1e30
D_FF = 2816
CONV_W = 3
EPS = 1e-6
IN_SIZES = (Q_LORA, KV_LORA, ROPE, H_B * DH_B, H_B * DH_B, H_B * DH_B, H_B * DH_B, 2 * H_B, 2 * H_B, D_MODEL, D_MODEL)
IN_COLS = sum(IN_SIZES)

kernel_name = "hybrid_mla_mlstm_encoder"


def _rmsnorm(x, g):
    xf = x.astype(jnp.float32)
    y = xf * lax.rsqrt(jnp.mean(xf * xf, axis=-1, keepdims=True) + EPS)
    return (y * g.astype(jnp.float32)).astype(x.dtype)


def _rope_tables(L):
    inv = ROPE_THETA ** (-jnp.arange(0, ROPE, 2, dtype=jnp.float32) / ROPE)
    ang = jnp.arange(L, dtype=jnp.float32)[:, None] * inv[None, :]
    return jnp.cos(ang), jnp.sin(ang)


def _rope(x, cos, sin):
    x1, x2 = jnp.split(x.astype(jnp.float32), 2, axis=-1)
    c = cos[:, None, :]
    s = sin[:, None, :]
    return jnp.concatenate([x1 * c - x2 * s, x1 * s + x2 * c], axis=-1).astype(x.dtype)


def _block_attention(q, k, v):
    B, H, L, dq = q.shape
    nb = -(-L // Q_BLOCK)
    Lq = nb * Q_BLOCK
    qp = jnp.pad(q, ((0, 0), (0, 0), (0, Lq - L), (0, 0)))
    qb = jnp.moveaxis(qp.reshape(B, H, nb, Q_BLOCK, dq), 2, 0)
    scale = QK_DIM ** -0.5

    def block(qi):
        s = jnp.einsum('bhqd,bhkd->bhqk', qi, k).astype(jnp.float32) * scale
        p = jax.nn.softmax(s, axis=-1)
        return jnp.einsum('bhqk,bhkd->bhqd', p.astype(v.dtype), v)

    o = lax.map(block, qb)
    return jnp.moveaxis(o, 0, 2).reshape(B, H, Lq, v.shape[-1])[:, :, :L]


def _mlstm_dir(q, k, v, i_pre, log_f):
    B, H, T, dk = q.shape
    dv = v.shape[-1]
    N = T // CHUNK
    f32 = jnp.float32
    qc = q.reshape(B, H, N, CHUNK, dk).astype(f32)
    kc = k.reshape(B, H, N, CHUNK, dk).astype(f32) * (dk ** -0.5)
    vc = v.reshape(B, H, N, CHUNK, dv).astype(f32)
    ic = i_pre.reshape(B, H, N, CHUNK)
    bc = jnp.cumsum(log_f.reshape(B, H, N, CHUNK), axis=-1)
    gc = bc[..., -1]
    a = gc[..., None] - bc + ic
    m_loc = jnp.max(a, axis=-1)
    w = jnp.exp(a - m_loc[..., None])
    dC = jnp.einsum('bhncd,bhnce->bhnde', kc * w[..., None], vc)
    dn = jnp.einsum('bhncd,bhnc->bhnd', kc, w)

    def step(carry, xs):
        C, n, m = carry
        dC_, dn_, g_, ml_ = xs
        m_new = jnp.maximum(g_ + m, ml_)
        sp = jnp.exp(g_ + m - m_new)
        sl = jnp.exp(ml_ - m_new)
        C_new = sp[..., None, None] * C + sl[..., None, None] * dC_
        n_new = sp[..., None] * n + sl[..., None] * dn_
        return (C_new, n_new, m_new), (C, n, m)

    init = (jnp.zeros((B, H, dk, dv), f32), jnp.zeros((B, H, dk), f32), jnp.full((B, H), NEG, f32))
    xs = (jnp.moveaxis(dC, 2, 0), jnp.moveaxis(dn, 2, 0), jnp.moveaxis(gc, 2, 0), jnp.moveaxis(m_loc, 2, 0))
    _, (Cs, ns, ms) = lax.scan(step, init, xs)
    Cs = jnp.moveaxis(Cs, 0, 2)
    ns = jnp.moveaxis(ns, 0, 2)
    ms = jnp.moveaxis(ms, 0, 2)

    D = bc[..., :, None] - bc[..., None, :] + ic[..., None, :]
    tril = jnp.tril(jnp.ones((CHUNK, CHUNK), dtype=bool))
    D = jnp.where(tril, D, -jnp.inf)
    inter = bc + ms[..., None]
    m_t = jnp.maximum(jnp.max(D, axis=-1), inter)
    P = jnp.exp(D - m_t[..., None])
    s = jnp.einsum('bhntd,bhnsd->bhnts', qc, kc) * P
    e_in = jnp.exp(inter - m_t)
    num = jnp.einsum('bhnts,bhnse->bhnte', s, vc) + e_in[..., None] * jnp.einsum('bhntd,bhnde->bhnte', qc, Cs)
    den = jnp.sum(s, axis=-1) + e_in * jnp.einsum('bhntd,bhnd->bhnt', qc, ns)
    h = num / jnp.maximum(jnp.abs(den), jnp.exp(-m_t))[..., None]
    return h.reshape(B, H, T, dv)


def _mixer(u, cos, sin, valid, w_in, q_norm, w_uq, kv_norm, w_ukv, b_igate, b_fgate, mlstm_norm,
           w_o_attn, w_o_mlstm, w_out):
    B, L, _ = u.shape
    z = u @ w_in
    points = np.cumsum(IN_SIZES)[:-1].tolist()
    z_q, z_kv, z_kr, m_q, m_k, m_v, m_o, z_i, z_f, z_ga, z_gb = jnp.split(z, points, axis=-1)

    q = (_rmsnorm(z_q, q_norm) @ w_uq).reshape(B, L, H_A, QK_DIM)
    q = jnp.concatenate([q[..., :NOPE], _rope(q[..., NOPE:], cos, sin)], axis=-1)
    kv = (_rmsnorm(z_kv, kv_norm) @ w_ukv).reshape(B, L, H_A, NOPE + V_DIM)
    k_r = _rope(z_kr[:, :, None, :], cos, sin)
    k = jnp.concatenate([kv[..., :NOPE], jnp.broadcast_to(k_r, (B, L, H_A, ROPE))], axis=-1)
    v = kv[..., NOPE:]
    attn = _block_attention(q.transpose(0, 2, 1, 3), k.transpose(0, 2, 1, 3), v.transpose(0, 2, 1, 3))
    attn = attn.transpose(0, 2, 1, 3).reshape(B, L, H_A * V_DIM)

    pad = CHUNK - N_META

    def heads(t):
        t = jnp.pad(t.reshape(B, L, H_B, DH_B), ((0, 0), (pad, 0), (0, 0), (0, 0)))
        return t.transpose(0, 2, 1, 3)

    mq, mk, mv = heads(m_q), heads(m_k), heads(m_v)
    gi = z_i.reshape(B, L, 2, H_B).astype(jnp.float32) + b_igate.astype(jnp.float32)
    gf = z_f.reshape(B, L, 2, H_B).astype(jnp.float32) + b_fgate.astype(jnp.float32)
    gi = jnp.pad(gi, ((0, 0), (pad, 0), (0, 0), (0, 0))).transpose(2, 0, 3, 1)
    gf = jnp.pad(gf, ((0, 0), (pad, 0), (0, 0), (0, 0))).transpose(2, 0, 3, 1)
    gi = jnp.where(valid, gi, NEG)
    log_f = jnp.where(valid, jax.nn.log_sigmoid(gf), 0.0)
    h_fwd = _mlstm_dir(mq, mk, mv, gi[0], log_f[0])
    h_bwd = jnp.flip(_mlstm_dir(jnp.flip(mq, 2), jnp.flip(mk, 2), jnp.flip(mv, 2),
                                jnp.flip(gi[1], -1), jnp.flip(log_f[1], -1)), 2)
    hm = (h_fwd + h_bwd)[:, :, pad:].transpose(0, 2, 1, 3).astype(u.dtype)
    hm = _rmsnorm(hm, mlstm_norm.reshape(H_B, DH_B)) * jax.nn.sigmoid(m_o).reshape(B, L, H_B, DH_B)
    hm = hm.reshape(B, L, H_B * DH_B)

    y = jax.nn.sigmoid(z_ga) * (attn @ w_o_attn) + jax.nn.sigmoid(z_gb) * (hm @ w_o_mlstm)
    return y @ w_out


def _conv_ffn(u, w_up, w_gate, conv_w, conv_b, w_down):
    L = u.shape[1]
    a = u @ w_up
    half = CONV_W // 2
    ap = jnp.pad(a, ((0, 0), (half, half), (0, 0)))
    a = sum(ap[:, j:j + L] * conv_w[j] for j in range(CONV_W)) + conv_b
    return (jax.nn.gelu(a, approximate=True) * (u @ w_gate)) @ w_down


def _encode(x, meta_tokens, norm_mix, w_in, q_norm, w_uq, kv_norm, w_ukv, b_igate, b_fgate, mlstm_norm,
            w_o_attn, w_o_mlstm, w_out, norm_ffn, w_up, w_gate, conv_w, conv_b, w_down, norm_final):
    B, S, _ = x.shape
    meta = jnp.broadcast_to(meta_tokens.astype(x.dtype)[None], (B, N_META, D_MODEL))
    h = jnp.concatenate([meta, x], axis=1)
    L = N_META + S
    cos, sin = _rope_tables(L)
    valid = jnp.arange(CHUNK + S) >= (CHUNK - N_META)
    for l in range(DEPTH):
        h = h + _mixer(_rmsnorm(h, norm_mix[l]), cos, sin, valid, w_in[l], q_norm[l], w_uq[l], kv_norm[l],
                       w_ukv[l], b_igate[l], b_fgate[l], mlstm_norm[l], w_o_attn[l], w_o_mlstm[l], w_out[l])
        h = h + _conv_ffn(_rmsnorm(h, norm_ffn[l]), w_up[l], w_gate[l], conv_w[l], conv_b[l], w_down[l])
    return _rmsnorm(h, norm_final)[:, N_META:]


def setup_inputs(seed: int = 0) -> dict:
    key = jax.random.key(seed)
    ks = jax.random.split(key, 24)
    f32 = jnp.float32

    def nrm(k, shape, fan):
        return jax.random.normal(k, shape, f32) * (fan ** -0.5)

    def gain(k, shape):
        return 1.0 + 0.01 * jax.random.normal(k, shape, f32)

    Ld = DEPTH
    return {
        "x_prompt": jax.random.normal(ks[0], (BATCH, SEQ, D_MODEL), f32),
        "x_sample": jax.random.normal(ks[1], (DEC_BATCH, DEC_SEQ, D_MODEL), f32),
        "meta_tokens": jax.random.normal(ks[2], (N_META, D_MODEL), f32),
        "norm_mix": gain(ks[3], (Ld, D_MODEL)),
        "w_in": nrm(ks[4], (Ld, D_MODEL, IN_COLS), D_MODEL),
        "q_norm": gain(ks[5], (Ld, Q_LORA)),
        "w_uq": nrm(ks[6], (Ld, Q_LORA, H_A * QK_DIM), Q_LORA),
        "kv_norm": gain(ks[7], (Ld, KV_LORA)),
        "w_ukv": nrm(ks[8], (Ld, KV_LORA, H_A * (NOPE + V_DIM)), KV_LORA),
        "b_igate": 0.1 * jax.random.normal(ks[9], (Ld, 2, H_B), f32),
        "b_fgate": jnp.linspace(3.0, 6.0, H_B, dtype=f32) + 0.1 * jax.random.normal(ks[10], (Ld, 2, H_B), f32),
        "mlstm_norm": gain(ks[11], (Ld, H_B * DH_B)),
        "w_o_attn": nrm(ks[12], (Ld, H_A * V_DIM, D_MODEL), H_A * V_DIM),
        "w_o_mlstm": nrm(ks[13], (Ld, H_B * DH_B, D_MODEL), H_B * DH_B),
        "w_out": nrm(ks[14], (Ld, D_MODEL, D_MODEL), D_MODEL),
        "norm_ffn": gain(ks[15], (Ld, D_MODEL)),
        "w_up": nrm(ks[16], (Ld, D_MODEL, D_FF), D_MODEL),
        "w_gate": nrm(ks[17], (Ld, D_MODEL, D_FF), D_MODEL),
        "conv_w": nrm(ks[18], (Ld, CONV_W, D_FF), CONV_W),
        "conv_b": 0.01 * jax.random.normal(ks[19], (Ld, D_FF), f32),
        "w_down": nrm(ks[20], (Ld, D_FF, D_MODEL), D_FF),
        "norm_final": gain(ks[21], (D_MODEL,)),
    }


def reference(x_prompt, x_sample, meta_tokens, norm_mix, w_in, q_norm, w_uq, kv_norm, w_ukv, b_igate, b_fgate,
              mlstm_norm, w_o_attn, w_o_mlstm, w_out, norm_ffn, w_up, w_gate, conv_w, conv_b, w_down, norm_final):
    y_prompt = _encode(x_prompt, meta_tokens, norm_mix, w_in, q_norm, w_uq, kv_norm, w_ukv, b_igate, b_fgate,
                       mlstm_norm, w_o_attn, w_o_mlstm, w_out, norm_ffn, w_up, w_gate, conv_w, conv_b, w_down,
                       norm_final)
    y_sample = _encode(x_sample, meta_tokens, norm_mix, w_in, q_norm, w_uq, kv_norm, w_ukv, b_igate, b_fgate,
                       mlstm_norm, w_o_attn, w_o_mlstm, w_out, norm_ffn, w_up, w_gate, conv_w, conv_b, w_down,
                       norm_final)
    return (y_prompt, y_sample)
```

```python
import functools
import math

import jax
import jax.numpy as jnp
from jax import lax
from jax.experimental import pallas as pl
from jax.experimental.pallas import tpu as pltpu

D_MODEL = 1024
N_META = 16
H_A = 8
Q_LORA = 256
KV_LORA = 128
NOPE = 64
ROPE = 32
QK_DIM = NOPE + ROPE
V_DIM = 64
ROPE_THETA = 10000.0
H_B = 4
DH_B = 128
D_FF = 2816
CONV_W = 3
EPS = 1e-6
NEG = -1e30
IN_SIZES = (Q_LORA, KV_LORA, ROPE, H_B * DH_B, H_B * DH_B, H_B * DH_B, H_B * DH_B, 2 * H_B, 2 * H_B, D_MODEL, D_MODEL)

LANES = 128
SUBLANES = 8
TAIL = 128
HEAD_PAD = LANES
CHUNK = 128
TQ = 256
KC = 256
VMEM_LIMIT = 56 * 1024 * 1024
QSCALE = (QK_DIM ** -0.5) * math.log2(math.e)
FF_CHUNKS = ((0, 1024), (1024, 2048), (2048, D_FF))

f32 = jnp.float32
bf16 = jnp.bfloat16


def _rms(x, g):
    return x * lax.rsqrt(jnp.mean(x * x, axis=-1, keepdims=True) + EPS) * g


def _sigmoid(x):
    return 1.0 / (1.0 + jnp.exp(-x))


def _log_sigmoid(x):
    return jnp.minimum(x, 0.0) - jnp.log1p(jnp.exp(-jnp.abs(x)))


def _dot(a, b):
    return jnp.dot(a, b, preferred_element_type=f32)


def _const_spec(shape):
    zeros = (0,) * len(shape)
    return pl.BlockSpec(shape, lambda *_: zeros, pipeline_mode=pl.Buffered(1))


def _params(sem):
    return pltpu.CompilerParams(dimension_semantics=sem, vmem_limit_bytes=VMEM_LIMIT)


def _inproj_kernel(h_ref, cm_ref, sm_ref, nmix_ref, wa_ref, wb_ref, wc_ref, qn_ref, wq_ref, kvn_ref, wkv_ref,
                   gb_ref, qT_ref, k_ref, vT_ref, mq_ref, mkT_ref, mv_ref, so_ref, g_ref, gT_ref, sga_ref, sgb_ref,
                   *, cq, mask_tail):
    tm = h_ref.shape[0]
    u = _rms(h_ref[...], nmix_ref[...]).astype(bf16)
    cm = cm_ref[...]
    sm = sm_ref[...]

    za = _dot(u, wa_ref[...])
    qn = _rms(za[:, :Q_LORA], qn_ref[...]).astype(bf16)
    kvn = _rms(za[:, Q_LORA:Q_LORA + KV_LORA], kvn_ref[...]).astype(bf16)
    kr = za[:, 384:512] * cm + za[:, 512:640] * sm
    cm8 = jnp.concatenate([cm] * H_A, axis=1)
    sm8 = jnp.concatenate([sm] * H_A, axis=1)
    q = (_dot(qn, wq_ref[:, :H_A * HEAD_PAD]) * cm8 + _dot(qn, wq_ref[:, H_A * HEAD_PAD:]) * sm8) * QSCALE
    qT = q.T
    for c in range(tm // cq):
        qT_ref[c] = qT[:, c * cq:(c + 1) * cq].astype(bf16)
    kv = _dot(kvn, wkv_ref[...])
    k_ref[...] = (kv[:, :H_A * HEAD_PAD] + jnp.concatenate([kr] * H_A, axis=1)).astype(bf16)
    vT = kv[:, H_A * HEAD_PAD:].T
    for c in range(tm // cq):
        vT_ref[c] = vT[:, c * cq:(c + 1) * cq].astype(bf16)

    zb = _dot(u, wb_ref[...])
    mq_ref[...] = zb[:, 0:512].astype(bf16)
    mkT = (zb[:, 512:1024] * (DH_B ** -0.5)).T
    for c in range(tm // CHUNK):
        mkT_ref[c] = mkT[:, c * CHUNK:(c + 1) * CHUNK].astype(bf16)
    mv_ref[...] = zb[:, 1024:1536].astype(bf16)
    so_ref[...] = _sigmoid(zb[:, 1536:2048]).astype(bf16)
    gz = zb[:, 2048:2176] + gb_ref[...]
    lane = lax.broadcasted_iota(jnp.int32, gz.shape, 1)
    gates = jnp.where(lane < 2 * H_B, gz, _log_sigmoid(gz))
    if mask_tail:
        row = lax.broadcasted_iota(jnp.int32, gz.shape, 0)
        pad = (row % TAIL) >= N_META
        gates = jnp.where(pad, jnp.where(lane < 2 * H_B, NEG, 0.0), gates)
    g_ref[...] = gates
    gT = gates.T
    for c in range(tm // CHUNK):
        gT_ref[c] = gT[:4 * H_B, c * CHUNK:(c + 1) * CHUNK]

    zc = _dot(u, wc_ref[...])
    sga_ref[...] = _sigmoid(zc[:, :D_MODEL]).astype(bf16)
    sgb_ref[...] = _sigmoid(zc[:, D_MODEL:]).astype(bf16)


def _inproj(h, cm, sm, tab_map, w, *, tm, cq, mask_tail):
    rows = h.shape[0]
    n = rows // tm
    row_spec = lambda width: pl.BlockSpec((tm, width), lambda i: (i, 0))
    chunk_spec = lambda height, cw: pl.BlockSpec((tm // cw, height, cw), lambda i: (i, 0, 0))
    out_shape = (
        jax.ShapeDtypeStruct((rows // cq, H_A * HEAD_PAD, cq), bf16),
        jax.ShapeDtypeStruct((rows, H_A * HEAD_PAD), bf16),
        jax.ShapeDtypeStruct((rows // cq, H_A * V_DIM, cq), bf16),
        jax.ShapeDtypeStruct((rows, H_B * DH_B), bf16),
        jax.ShapeDtypeStruct((rows // CHUNK, H_B * DH_B, CHUNK), bf16),
        jax.ShapeDtypeStruct((rows, H_B * DH_B), bf16),
        jax.ShapeDtypeStruct((rows, H_B * DH_B), bf16),
        jax.ShapeDtypeStruct((rows, LANES), f32),
        jax.ShapeDtypeStruct((rows // CHUNK, 4 * H_B, CHUNK), f32),
        jax.ShapeDtypeStruct((rows, D_MODEL), bf16),
        jax.ShapeDtypeStruct((rows, D_MODEL), bf16),
    )
    out_specs = (
        chunk_spec(H_A * HEAD_PAD, cq), row_spec(H_A * HEAD_PAD), chunk_spec(H_A * V_DIM, cq),
        row_spec(H_B * DH_B), chunk_spec(H_B * DH_B, CHUNK), row_spec(H_B * DH_B), row_spec(H_B * DH_B),
        row_spec(LANES), chunk_spec(4 * H_B, CHUNK), row_spec(D_MODEL), row_spec(D_MODEL),
    )
    in_specs = [
        row_spec(D_MODEL),
        pl.BlockSpec((tm, LANES), tab_map), pl.BlockSpec((tm, LANES), tab_map),
        _const_spec((1, D_MODEL)), _const_spec(w["wa"].shape), _const_spec(w["wb"].shape),
        _const_spec(w["wc"].shape), _const_spec((1, Q_LORA)), _const_spec(w["wq"].shape),
        _const_spec((1, KV_LORA)), _const_spec(w["wkv"].shape), _const_spec((1, LANES)),
    ]
    return pl.pallas_call(
        functools.partial(_inproj_kernel, cq=cq, mask_tail=mask_tail),
        grid=(n,), in_specs=in_specs, out_specs=out_specs, out_shape=out_shape,
        compiler_params=_params(("parallel",)), name="inproj",
    )(h, cm, sm, w["norm_mix"], w["wa"], w["wb"], w["wc"], w["q_norm"], w["wq"], w["kv_norm"], w["wkv"], w["gbias"])


def _attn_kernel(qT_ref, k_ref, vT_ref, kt_ref, vtt_ref, o_ref, m_sc, l_sc, acc_sc):
    ki = pl.program_id(2)
    nqs, _, tq = qT_ref.shape
    nkc, _, kc = vT_ref.shape

    @pl.when(ki == 0)
    def _():
        def qs_body(qs, carry):
            ms, ls = [], []
            for h in range(H_A):
                qT = qT_ref[qs, h * HEAD_PAD:(h + 1) * HEAD_PAD, :]
                s = _dot(kt_ref[0:N_META, h * HEAD_PAD:(h + 1) * HEAD_PAD], qT)
                m = jnp.max(s, axis=0, keepdims=True)
                p = jnp.exp2(s - m)
                ms.append(m)
                ls.append(jnp.sum(p, axis=0, keepdims=True))
                acc_sc[qs, h * V_DIM:(h + 1) * V_DIM, :] = _dot(
                    vtt_ref[0, h * V_DIM:(h + 1) * V_DIM, 0:N_META], p.astype(bf16))
            m_sc[qs] = jnp.concatenate(ms, axis=0)
            l_sc[qs] = jnp.concatenate(ls, axis=0)
            return carry
        lax.fori_loop(0, nqs, qs_body, 0)

    def qs_body(qs, carry):
        def kc_body(c, carry2):
            m_all = m_sc[qs]
            l_all = l_sc[qs]
            off = pl.multiple_of(c * kc, kc)
            ms, ls = [], []
            for h in range(H_A):
                qT = qT_ref[qs, h * HEAD_PAD:(h + 1) * HEAD_PAD, :]
                s = _dot(k_ref[pl.ds(off, kc), h * HEAD_PAD:(h + 1) * HEAD_PAD], qT)
                m_old = m_all[h:h + 1, :]
                m_new = jnp.maximum(m_old, jnp.max(s, axis=0, keepdims=True))
                p = jnp.exp2(s - m_new)
                alpha = jnp.exp2(m_old - m_new)
                ms.append(m_new)
                ls.append(alpha * l_all[h:h + 1, :] + jnp.sum(p, axis=0, keepdims=True))
                pv = _dot(vT_ref[c, h * V_DIM:(h + 1) * V_DIM, :], p.astype(bf16))
                acc_sc[qs, h * V_DIM:(h + 1) * V_DIM, :] = alpha * acc_sc[qs, h * V_DIM:(h + 1) * V_DIM, :] + pv
            m_sc[qs] = jnp.concatenate(ms, axis=0)
            l_sc[qs] = jnp.concatenate(ls, axis=0)
            return carry2
        lax.fori_loop(0, nkc, kc_body, 0)
        return carry
    lax.fori_loop(0, nqs, qs_body, 0)

    @pl.when(ki == pl.num_programs(2) - 1)
    def _():
        def qs_body2(qs, carry):
            inv = 1.0 / l_sc[qs]
            outs = [acc_sc[qs, h * V_DIM:(h + 1) * V_DIM, :] * inv[h:h + 1, :] for h in range(H_A)]
            o = jnp.concatenate(outs, axis=0)
            o_ref[pl.ds(pl.multiple_of(qs * tq, tq), tq), :] = o.T.astype(bf16)
            return carry
        lax.fori_loop(0, nqs, qs_body2, 0)


def _attention(qT, k, vT, k_tail, vT_tail, *, batch, seq, tqb, tkb, q_is_tail):
    tq = qT.shape[2]
    kc = vT.shape[2]
    nqb = (TAIL if q_is_tail else seq) // tqb
    nkb = seq // tkb
    nqs = tqb // tq
    q_rows = qT.shape[0] * tq
    in_specs = [
        pl.BlockSpec((nqs, H_A * HEAD_PAD, tq), lambda b, qi, ki: (b * nqb + qi, 0, 0)),
        pl.BlockSpec((tkb, H_A * HEAD_PAD), lambda b, qi, ki: (b * nkb + ki, 0)),
        pl.BlockSpec((tkb // kc, H_A * V_DIM, kc), lambda b, qi, ki: (b * nkb + ki, 0, 0)),
        pl.BlockSpec((TAIL, H_A * HEAD_PAD), lambda b, qi, ki: (b, 0)),
        pl.BlockSpec((1, H_A * V_DIM, TAIL), lambda b, qi, ki: (b, 0, 0)),
    ]
    return pl.pallas_call(
        _attn_kernel,
        grid=(batch, nqb, nkb), in_specs=in_specs,
        out_specs=pl.BlockSpec((tqb, H_A * V_DIM), lambda b, qi, ki: (b * nqb + qi, 0)),
        out_shape=jax.ShapeDtypeStruct((q_rows, H_A * V_DIM), bf16),
        scratch_shapes=[pltpu.VMEM((nqs, H_A, tq), f32), pltpu.VMEM((nqs, H_A, tq), f32),
                        pltpu.VMEM((nqs, H_A * V_DIM, tq), f32)],
        compiler_params=_params(("parallel", "parallel", "arbitrary")), name="attention",
    )(qT, k, vT, k_tail, vT_tail)


def _split3(x):
    a = x.astype(bf16)
    r = x - a.astype(f32)
    b = r.astype(bf16)
    c = (r - b.astype(f32)).astype(bf16)
    return a, b, c


def _mlstm_chunk(q, kT, v, g, gT, c_sc, m_sc, *, reverse):
    d = 1 if reverse else 0
    r = lax.broadcasted_iota(jnp.int32, (CHUNK, CHUNK), 0)
    c = lax.broadcasted_iota(jnp.int32, (CHUNK, CHUNK), 1)
    mask = (c >= r) if reverse else (c <= r)
    tri = jnp.where(mask, 1.0, 0.0).astype(bf16)
    triT = jnp.where((r >= c) if reverse else (r <= c), 1.0, 0.0).astype(bf16)
    g3 = _split3(g)
    bcol_all = _dot(tri, g3[0]) + _dot(tri, g3[1]) + _dot(tri, g3[2])
    t3 = _split3(gT)
    brow_all = _dot(t3[0], triT) + _dot(t3[1], triT) + _dot(t3[2], triT)
    lane = lax.broadcasted_iota(jnp.int32, (CHUNK, LANES), 1)
    ones_col = jnp.where(lane == 0, 1.0, 0.0).astype(bf16)
    outs = []
    for h in range(H_B):
        ji = d * H_B + h
        jf = 2 * H_B + d * H_B + h
        bc = bcol_all[:, jf:jf + 1]
        br = brow_all[jf:jf + 1, :]
        ic = g[:, ji:ji + 1]
        ir = gT[ji:ji + 1, :]
        gtot = br[:, 0:1] if reverse else br[:, CHUNK - 1:CHUNK]
        m_prev = m_sc[h]
        qh = q[:, h * DH_B:(h + 1) * DH_B]
        kTh = kT[h * DH_B:(h + 1) * DH_B, :]
        v_aug = jnp.concatenate([v[:, h * DH_B:(h + 1) * DH_B], ones_col], axis=1)
        dmat = jnp.where(mask, bc - br + ir, -jnp.inf)
        inter = bc + m_prev
        m_t = jnp.maximum(jnp.max(dmat, axis=1, keepdims=True), inter)
        p = jnp.exp(dmat - m_t)
        s = (_dot(qh, kTh) * p).astype(bf16)
        e_in = jnp.exp(inter - m_t)
        c_prev = c_sc[h]
        numden = _dot(s, v_aug) + e_in * _dot(qh, c_prev.astype(bf16))
        den = numden[:, DH_B:DH_B + 1]
        outs.append(numden[:, :DH_B] / jnp.maximum(jnp.abs(den), jnp.exp(-m_t)))
        a_col = gtot - bc + ic
        m_loc = jnp.max(a_col, axis=0, keepdims=True)
        m_new = jnp.maximum(gtot + m_prev, m_loc)
        sp = jnp.exp(gtot + m_prev - m_new)
        sl = jnp.exp(m_loc - m_new)
        wv = (jnp.exp(a_col - m_loc) * v_aug.astype(f32)).astype(bf16)
        c_sc[h] = sp * c_prev + sl * _dot(kTh, wv)
        m_sc[h] = m_new
    return jnp.concatenate(outs, axis=1)


def _mlstm_kernel(q_ref, kT_ref, v_ref, g_ref, gT_ref, qt_ref, kTt_ref, vt_ref, gt_ref, gTt_ref,
                  o_ref, ot_ref, c_sc, m_sc, *, reverse):
    j = pl.program_id(1)
    nj = pl.num_programs(1)
    nch = kT_ref.shape[0]
    step = functools.partial(_mlstm_chunk, c_sc=c_sc, m_sc=m_sc, reverse=reverse)

    def tail_chunk():
        ot_ref[...] = step(qt_ref[...], kTt_ref[0], vt_ref[...], gt_ref[...], gTt_ref[0])

    @pl.when(j == 0)
    def _():
        c_sc[...] = jnp.zeros(c_sc.shape, f32)
        m_sc[...] = jnp.full(m_sc.shape, NEG, f32)
        if not reverse:
            tail_chunk()

    for c in (range(nch - 1, -1, -1) if reverse else range(nch)):
        rows = slice(c * CHUNK, (c + 1) * CHUNK)
        o_ref[rows, :] = step(q_ref[rows, :], kT_ref[c], v_ref[rows, :], g_ref[rows, :], gT_ref[c])

    if reverse:
        @pl.when(j == nj - 1)
        def _():
            tail_chunk()


def _mlstm(x_arrs, t_arrs, *, batch, seq, rb, reverse):
    mq, mkT, mv, g, gT = x_arrs
    mq_t, mkT_t, mv_t, g_t, gT_t = t_arrs
    nj = seq // rb
    nch = rb // CHUNK
    blk = (lambda b, j: b * nj + (nj - 1 - j)) if reverse else (lambda b, j: b * nj + j)
    in_specs = [
        pl.BlockSpec((rb, H_B * DH_B), lambda b, j: (blk(b, j), 0)),
        pl.BlockSpec((nch, H_B * DH_B, CHUNK), lambda b, j: (blk(b, j), 0, 0)),
        pl.BlockSpec((rb, H_B * DH_B), lambda b, j: (blk(b, j), 0)),
        pl.BlockSpec((rb, LANES), lambda b, j: (blk(b, j), 0)),
        pl.BlockSpec((nch, 4 * H_B, CHUNK), lambda b, j: (blk(b, j), 0, 0)),
        pl.BlockSpec((TAIL, H_B * DH_B), lambda b, j: (b, 0)),
        pl.BlockSpec((1, H_B * DH_B, CHUNK), lambda b, j: (b, 0, 0)),
        pl.BlockSpec((TAIL, H_B * DH_B), lambda b, j: (b, 0)),
        pl.BlockSpec((TAIL, LANES), lambda b, j: (b, 0)),
        pl.BlockSpec((1, 4 * H_B, CHUNK), lambda b, j: (b, 0, 0)),
    ]
    out_specs = (pl.BlockSpec((rb, H_B * DH_B), lambda b, j: (blk(b, j), 0)),
                 pl.BlockSpec((TAIL, H_B * DH_B), lambda b, j: (b, 0)))
    out_shape = (jax.ShapeDtypeStruct((batch * seq, H_B * DH_B), f32),
                 jax.ShapeDtypeStruct((batch * TAIL, H_B * DH_B), f32))
    return pl.pallas_call(
        functools.partial(_mlstm_kernel, reverse=reverse),
        grid=(batch, nj), in_specs=in_specs, out_specs=out_specs, out_shape=out_shape,
        scratch_shapes=[pltpu.VMEM((H_B, DH_B, 2 * DH_B), f32), pltpu.VMEM((H_B, 1, 1), f32)],
        compiler_params=_params(("parallel", "arbitrary")), name="mlstm_bwd" if reverse else "mlstm_fwd",
    )(mq, mkT, mv, g, gT, mq_t, mkT_t, mv_t, g_t, gT_t)


def _merge_kernel(attn_ref, hf_ref, hb_ref, so_ref, sga_ref, sgb_ref, res_ref, mn_ref, woa_ref, wom_ref, wout_ref,
                  o_ref):
    hm = hf_ref[...] + hb_ref[...]
    mn = mn_ref[...]
    parts = [_rms(hm[:, h * DH_B:(h + 1) * DH_B], mn[:, h * DH_B:(h + 1) * DH_B]) for h in range(H_B)]
    hmn = (jnp.concatenate(parts, axis=1) * so_ref[...].astype(f32)).astype(bf16)
    y = (sga_ref[...].astype(f32) * _dot(attn_ref[...], woa_ref[...])
         + sgb_ref[...].astype(f32) * _dot(hmn, wom_ref[...]))
    o_ref[...] = res_ref[...] + _dot(y.astype(bf16), wout_ref[...])


def _merge(attn, hf, hb, so, sga, sgb, res, w, *, tm):
    rows = attn.shape[0]
    row_spec = lambda width: pl.BlockSpec((tm, width), lambda i: (i, 0))
    in_specs = [row_spec(H_A * V_DIM), row_spec(H_B * DH_B), row_spec(H_B * DH_B), row_spec(H_B * DH_B),
                row_spec(D_MODEL), row_spec(D_MODEL), row_spec(D_MODEL),
                _const_spec((1, H_B * DH_B)), _const_spec(w["woa"].shape), _const_spec(w["wom"].shape),
                _const_spec(w["wout"].shape)]
    return pl.pallas_call(
        _merge_kernel, grid=(rows // tm,), in_specs=in_specs, out_specs=row_spec(D_MODEL),
        out_shape=jax.ShapeDtypeStruct((rows, D_MODEL), f32),
        compiler_params=_params(("parallel",)), name="merge",
    )(attn, hf, hb, so, sga, sgb, res, w["mlstm_norm"], w["woa"], w["wom"], w["wout"])


def _gelu_tanh(x):
    return 0.5 * x * (1.0 + jnp.tanh(math.sqrt(2.0 / math.pi) * (x + 0.044715 * (x * x * x))))


def _ffn_kernel(h_ref, prev_ref, prevt_ref, next_ref, nf_ref, wup_ref, wgate_ref, cw_ref, cb_ref, wdown_ref,
                nfin_ref, o_ref, *, tiles_per_batch):
    i = pl.program_id(0)
    tm = h_ref.shape[0]
    first = (i % tiles_per_batch) == 0
    last = (i % tiles_per_batch) == tiles_per_batch - 1
    nf = nf_ref[...]
    h = h_ref[...]
    u = _rms(h, nf).astype(bf16)
    prev = jnp.where(first, prevt_ref[...], prev_ref[...])
    u_prev = _rms(prev, nf).astype(bf16)
    u_next = _rms(next_ref[...], nf).astype(bf16)
    row = lax.broadcasted_iota(jnp.int32, (tm, 1), 0)
    acc = h
    for lo, hi in FF_CHUNKS:
        wup = wup_ref[:, lo:hi]
        a = _dot(u, wup)
        a_prev = _dot(u_prev, wup)[SUBLANES - 1:SUBLANES, :]
        a_next = jnp.where(last, 0.0, _dot(u_next, wup)[0:1, :])
        a_dn = jnp.where(row == 0, a_prev, pltpu.roll(a, 1, axis=0))
        a_up = jnp.where(row == tm - 1, a_next, pltpu.roll(a, tm - 1, axis=0))
        cw = cw_ref[:, lo:hi]
        conv = a_dn * cw[0:1, :] + a * cw[1:2, :] + a_up * cw[2:3, :] + cb_ref[:, lo:hi]
        gated = (_gelu_tanh(conv) * _dot(u, wgate_ref[:, lo:hi])).astype(bf16)
        acc = acc + _dot(gated, wdown_ref[lo:hi, :])
    o_ref[...] = _rms(acc, nfin_ref[...])


def _ffn(h1, h1_tail, w, *, seq, tm):
    rows = h1.shape[0]
    n = rows // tm
    tpb = seq // tm
    hb = tm // SUBLANES
    in_specs = [
        pl.BlockSpec((tm, D_MODEL), lambda i: (i, 0)),
        pl.BlockSpec((SUBLANES, D_MODEL), lambda i: (jnp.maximum(i * hb - 1, 0), 0)),
        pl.BlockSpec((SUBLANES, D_MODEL), lambda i: ((i // tpb) * (TAIL // SUBLANES) + N_META // SUBLANES - 1, 0)),
        pl.BlockSpec((SUBLANES, D_MODEL), lambda i: (jnp.minimum((i + 1) * hb, n * hb - 1), 0)),
        _const_spec((1, D_MODEL)), _const_spec(w["wup"].shape), _const_spec(w["wgate"].shape),
        _const_spec((SUBLANES, D_FF)), _const_spec((1, D_FF)), _const_spec(w["wdown"].shape),
        _const_spec((1, D_MODEL)),
    ]
    return pl.pallas_call(
        functools.partial(_ffn_kernel, tiles_per_batch=tpb),
        grid=(n,), in_specs=in_specs, out_specs=pl.BlockSpec((tm, D_MODEL), lambda i: (i, 0)),
        out_shape=jax.ShapeDtypeStruct((rows, D_MODEL), f32),
        compiler_params=_params(("parallel",)), name="conv_ffn",
    )(h1, h1, h1_tail, h1, w["norm_ffn"], w["wup"], w["wgate"], w["conv_w"], w["conv_b"], w["wdown"],
      w["norm_final"])


def _prep_weights(norm_mix, w_in, q_norm, w_uq, kv_norm, w_ukv, b_igate, b_fgate, mlstm_norm, w_o_attn, w_o_mlstm,
                  w_out, norm_ffn, w_up, w_gate, conv_w, conv_b, w_down, norm_final):
    w = w_in[0]
    offs = [0]
    for s in IN_SIZES:
        offs.append(offs[-1] + s)
    col = lambda j: w[:, offs[j]:offs[j + 1]]
    zeros = lambda r, c: jnp.zeros((r, c), f32)
    w_kr = col(2)
    half = ROPE // 2
    kr_main = jnp.concatenate([zeros(D_MODEL, NOPE), w_kr, zeros(D_MODEL, HEAD_PAD - QK_DIM)], axis=1)
    kr_rot = jnp.concatenate([zeros(D_MODEL, NOPE), -w_kr[:, half:], w_kr[:, :half],
                              zeros(D_MODEL, HEAD_PAD - QK_DIM)], axis=1)
    wa = jnp.concatenate([col(0), col(1), kr_main, kr_rot], axis=1)
    wb = jnp.concatenate([col(3), col(4), col(5), col(6), col(7), col(8), zeros(D_MODEL, LANES - 4 * H_B)], axis=1)
    wc = jnp.concatenate([col(9), col(10)], axis=1)
    wuq = w_uq[0].reshape(Q_LORA, H_A, QK_DIM)
    nope, rope = wuq[:, :, :NOPE], wuq[:, :, NOPE:]
    zq = lambda c: jnp.zeros((Q_LORA, H_A, c), f32)
    q_main = jnp.concatenate([nope, rope, zq(HEAD_PAD - QK_DIM)], axis=-1).reshape(Q_LORA, H_A * HEAD_PAD)
    q_rot = jnp.concatenate([zq(NOPE), -rope[:, :, half:], rope[:, :, :half], zq(HEAD_PAD - QK_DIM)],
                            axis=-1).reshape(Q_LORA, H_A * HEAD_PAD)
    wq = jnp.concatenate([q_main, q_rot], axis=1)
    wukv = w_ukv[0].reshape(KV_LORA, H_A, NOPE + V_DIM)
    wk = jnp.concatenate([wukv[:, :, :NOPE], jnp.zeros((KV_LORA, H_A, HEAD_PAD - NOPE), f32)],
                         axis=-1).reshape(KV_LORA, H_A * HEAD_PAD)
    wv = wukv[:, :, NOPE:].reshape(KV_LORA, H_A * V_DIM)
    wkv = jnp.concatenate([wk, wv], axis=1)
    gbias = jnp.concatenate([b_igate[0].reshape(-1), b_fgate[0].reshape(-1), jnp.zeros((LANES - 4 * H_B,), f32)])
    cw = jnp.concatenate([conv_w[0], jnp.zeros((SUBLANES - CONV_W, D_FF), f32)], axis=0)
    return dict(
        norm_mix=norm_mix[0][None], wa=wa.astype(bf16), wb=wb.astype(bf16), wc=wc.astype(bf16),
        q_norm=q_norm[0][None], wq=wq.astype(bf16), kv_norm=kv_norm[0][None], wkv=wkv.astype(bf16),
        gbias=gbias[None], mlstm_norm=mlstm_norm[0][None], woa=w_o_attn[0].astype(bf16),
        wom=w_o_mlstm[0].astype(bf16), wout=w_out[0].astype(bf16), norm_ffn=norm_ffn[0][None],
        wup=w_up[0].astype(bf16), wgate=w_gate[0].astype(bf16), conv_w=cw, conv_b=conv_b[0][None],
        wdown=w_down[0].astype(bf16), norm_final=norm_final[None])


def _rope_tables(positions):
    inv = ROPE_THETA ** (-jnp.arange(0, ROPE, 2, dtype=f32) / ROPE)
    ang = positions.astype(f32)[:, None] * inv[None, :]
    n = positions.shape[0]
    cos, sin = jnp.cos(ang), jnp.sin(ang)
    cm = jnp.concatenate([jnp.ones((n, NOPE), f32), cos, cos, jnp.zeros((n, HEAD_PAD - QK_DIM), f32)], axis=1)
    sm = jnp.concatenate([jnp.zeros((n, NOPE), f32), sin, sin, jnp.zeros((n, HEAD_PAD - QK_DIM), f32)], axis=1)
    return cm, sm


def _tiles(batch, seq):
    tm = math.gcd(seq, 512)
    tm_tail = math.gcd(batch * TAIL, 512)
    tkb = math.gcd(seq, 2048)
    tqb = math.gcd(seq, 512 if seq // tkb == 1 else 1024)
    return dict(tm=tm, tm_tail=tm_tail, tqb=tqb, tkb=tkb, rb=tm)


def _encode(x, meta_tokens, w):
    batch, seq, _ = x.shape
    t = _tiles(batch, seq)
    xf = x.reshape(batch * seq, D_MODEL)
    tail_one = jnp.concatenate([meta_tokens.astype(f32), jnp.zeros((TAIL - N_META, D_MODEL), f32)], axis=0)
    tailf = jnp.tile(tail_one, (batch, 1))

    cm_x, sm_x = _rope_tables(N_META + jnp.arange(seq))
    pos_t = jnp.tile(jnp.arange(TAIL), t["tm_tail"] // TAIL)
    cm_t, sm_t = _rope_tables(pos_t)
    nper = seq // t["tm"]
    xs = _inproj(xf, cm_x, sm_x, lambda i: (i % nper, 0), w, tm=t["tm"], cq=TQ, mask_tail=False)
    ts = _inproj(tailf, cm_t, sm_t, lambda i: (0, 0), w, tm=t["tm_tail"], cq=TAIL, mask_tail=True)
    qT, k, vT, mq, mkT, mv, so, g, gT, sga, sgb = xs
    qT_t, k_t, vT_t, mq_t, mkT_t, mv_t, so_t, g_t, gT_t, sga_t, sgb_t = ts

    attn = _attention(qT, k, vT, k_t, vT_t, batch=batch, seq=seq, tqb=t["tqb"], tkb=t["tkb"], q_is_tail=False)
    attn_t = _attention(qT_t, k, vT, k_t, vT_t, batch=batch, seq=seq, tqb=TAIL, tkb=t["tkb"], q_is_tail=True)

    x_arrs = (mq, mkT, mv, g, gT)
    t_arrs = (mq_t, mkT_t, mv_t, g_t, gT_t)
    hf, hf_t = _mlstm(x_arrs, t_arrs, batch=batch, seq=seq, rb=t["rb"], reverse=False)
    hb, hb_t = _mlstm(x_arrs, t_arrs, batch=batch, seq=seq, rb=t["rb"], reverse=True)

    h1 = _merge(attn, hf, hb, so, sga, sgb, xf, w, tm=t["tm"])
    h1_t = _merge(attn_t, hf_t, hb_t, so_t, sga_t, sgb_t, tailf, w, tm=t["tm_tail"])
    y = _ffn(h1, h1_t, w, seq=seq, tm=t["tm"])
    return y.reshape(batch, seq, D_MODEL)


def kernel(x_prompt, x_sample, meta_tokens, norm_mix, w_in, q_norm, w_uq, kv_norm, w_ukv, b_igate, b_fgate,
           mlstm_norm, w_o_attn, w_o_mlstm, w_out, norm_ffn, w_up, w_gate, conv_w, conv_b, w_down, norm_final):
    w = _prep_weights(norm_mix, w_in, q_norm, w_uq, kv_norm, w_ukv, b_igate, b_fgate, mlstm_norm, w_o_attn,
                      w_o_mlstm, w_out, norm_ffn, w_up, w_gate, conv_w, conv_b, w_down, norm_final)
    return (_encode(x_prompt, meta_tokens, w), _encode(x_sample, meta_tokens, w))
```

```python
import functools
import math

import jax
import jax.numpy as jnp
from jax import lax
from jax.experimental import pallas as pl
from jax.experimental.pallas import tpu as pltpu

D_MODEL = 1024
N_META = 16
H_A = 8
Q_LORA = 256
KV_LORA = 128
NOPE = 64
ROPE = 32
QK_DIM = NOPE + ROPE
V_DIM = 64
ROPE_THETA = 10000.0
H_B = 4
DH_B = 128
D_FF = 2816
CONV_W = 3
EPS = 1e-6
NEG = -1e30
IN_SIZES = (Q_LORA, KV_LORA, ROPE, H_B * DH_B, H_B * DH_B, H_B * DH_B, H_B * DH_B, 2 * H_B, 2 * H_B, D_MODEL, D_MODEL)

LANES = 128
SUBLANES = 8
TAIL = 128
HEAD_PAD = LANES
CHUNK = 128
TQ = 256
KC = 512
VMEM_LIMIT = 56 * 1024 * 1024
QSCALE = (QK_DIM ** -0.5) * math.log2(math.e)
FF_CHUNKS = ((0, 1024), (1024, 2048), (2048, D_FF))

f32 = jnp.float32
bf16 = jnp.bfloat16


def _rms(x, g):
    return x * lax.rsqrt(jnp.mean(x * x, axis=-1, keepdims=True) + EPS) * g


def _sigmoid(x):
    return 1.0 / (1.0 + jnp.exp(-x))


def _log_sigmoid(x):
    return jnp.minimum(x, 0.0) - jnp.log1p(jnp.exp(-jnp.abs(x)))


def _dot(a, b):
    return jnp.dot(a, b, preferred_element_type=f32)


def _const_spec(shape):
    zeros = (0,) * len(shape)
    return pl.BlockSpec(shape, lambda *_: zeros, pipeline_mode=pl.Buffered(1))


def _params(sem):
    return pltpu.CompilerParams(dimension_semantics=sem, vmem_limit_bytes=VMEM_LIMIT)


def _inproj_kernel(h_ref, cm_ref, sm_ref, nmix_ref, wa_ref, wb_ref, wc_ref, qn_ref, wq_ref, kvn_ref, wkv_ref,
                   gb_ref, qT_ref, k_ref, vT_ref, mq_ref, mkT_ref, mv_ref, so_ref, g_ref, gT_ref, sga_ref, sgb_ref,
                   *, cq, ck, mask_tail):
    tm = h_ref.shape[0]
    u = _rms(h_ref[...], nmix_ref[...]).astype(bf16)
    cm = cm_ref[...]
    sm = sm_ref[...]

    za = _dot(u, wa_ref[...])
    qn = _rms(za[:, :Q_LORA], qn_ref[...]).astype(bf16)
    kvn = _rms(za[:, Q_LORA:Q_LORA + KV_LORA], kvn_ref[...]).astype(bf16)
    kr = za[:, 384:512] * cm + za[:, 512:640] * sm
    cm8 = jnp.concatenate([cm] * H_A, axis=1)
    sm8 = jnp.concatenate([sm] * H_A, axis=1)
    q = (_dot(qn, wq_ref[:, :H_A * HEAD_PAD]) * cm8 + _dot(qn, wq_ref[:, H_A * HEAD_PAD:]) * sm8) * QSCALE
    qT = q.T
    for c in range(tm // cq):
        qT_ref[c] = qT[:, c * cq:(c + 1) * cq].astype(bf16)
    kv = _dot(kvn, wkv_ref[...])
    k_ref[...] = (kv[:, :H_A * HEAD_PAD] + jnp.concatenate([kr] * H_A, axis=1)).astype(bf16)
    vT = kv[:, H_A * HEAD_PAD:].T
    for c in range(tm // ck):
        vT_ref[c] = vT[:, c * ck:(c + 1) * ck].astype(bf16)

    zb = _dot(u, wb_ref[...])
    mq_ref[...] = zb[:, 0:512].astype(bf16)
    mkT = (zb[:, 512:1024] * (DH_B ** -0.5)).T
    for c in range(tm // CHUNK):
        mkT_ref[c] = mkT[:, c * CHUNK:(c + 1) * CHUNK].astype(bf16)
    mv_ref[...] = zb[:, 1024:1536].astype(bf16)
    so_ref[...] = _sigmoid(zb[:, 1536:2048]).astype(bf16)
    gz = zb[:, 2048:2176] + gb_ref[...]
    lane = lax.broadcasted_iota(jnp.int32, gz.shape, 1)
    gates = jnp.where(lane < 2 * H_B, gz, _log_sigmoid(gz))
    if mask_tail:
        row = lax.broadcasted_iota(jnp.int32, gz.shape, 0)
        pad = (row % TAIL) >= N_META
        gates = jnp.where(pad, jnp.where(lane < 2 * H_B, NEG, 0.0), gates)
    g_ref[...] = gates
    gT = gates.T
    for c in range(tm // CHUNK):
        gT_ref[c] = gT[:4 * H_B, c * CHUNK:(c + 1) * CHUNK]

    zc = _dot(u, wc_ref[...])
    sga_ref[...] = _sigmoid(zc[:, :D_MODEL]).astype(bf16)
    sgb_ref[...] = _sigmoid(zc[:, D_MODEL:]).astype(bf16)


def _inproj(h, cm, sm, tab_map, w, *, tm, cq, ck, mask_tail):
    rows = h.shape[0]
    n = rows // tm
    row_spec = lambda width: pl.BlockSpec((tm, width), lambda i: (i, 0))
    chunk_spec = lambda height, cw: pl.BlockSpec((tm // cw, height, cw), lambda i: (i, 0, 0))
    out_shape = (
        jax.ShapeDtypeStruct((rows // cq, H_A * HEAD_PAD, cq), bf16),
        jax.ShapeDtypeStruct((rows, H_A * HEAD_PAD), bf16),
        jax.ShapeDtypeStruct((rows // ck, H_A * V_DIM, ck), bf16),
        jax.ShapeDtypeStruct((rows, H_B * DH_B), bf16),
        jax.ShapeDtypeStruct((rows // CHUNK, H_B * DH_B, CHUNK), bf16),
        jax.ShapeDtypeStruct((rows, H_B * DH_B), bf16),
        jax.ShapeDtypeStruct((rows, H_B * DH_B), bf16),
        jax.ShapeDtypeStruct((rows, LANES), f32),
        jax.ShapeDtypeStruct((rows // CHUNK, 4 * H_B, CHUNK), f32),
        jax.ShapeDtypeStruct((rows, D_MODEL), bf16),
        jax.ShapeDtypeStruct((rows, D_MODEL), bf16),
    )
    out_specs = (
        chunk_spec(H_A * HEAD_PAD, cq), row_spec(H_A * HEAD_PAD), chunk_spec(H_A * V_DIM, ck),
        row_spec(H_B * DH_B), chunk_spec(H_B * DH_B, CHUNK), row_spec(H_B * DH_B), row_spec(H_B * DH_B),
        row_spec(LANES), chunk_spec(4 * H_B, CHUNK), row_spec(D_MODEL), row_spec(D_MODEL),
    )
    in_specs = [
        row_spec(D_MODEL),
        pl.BlockSpec((tm, LANES), tab_map), pl.BlockSpec((tm, LANES), tab_map),
        _const_spec((1, D_MODEL)), _const_spec(w["wa"].shape), _const_spec(w["wb"].shape),
        _const_spec(w["wc"].shape), _const_spec((1, Q_LORA)), _const_spec(w["wq"].shape),
        _const_spec((1, KV_LORA)), _const_spec(w["wkv"].shape), _const_spec((1, LANES)),
    ]
    return pl.pallas_call(
        functools.partial(_inproj_kernel, cq=cq, ck=ck, mask_tail=mask_tail),
        grid=(n,), in_specs=in_specs, out_specs=out_specs, out_shape=out_shape,
        compiler_params=_params(("parallel",)), name="inproj",
    )(h, cm, sm, w["norm_mix"], w["wa"], w["wb"], w["wc"], w["q_norm"], w["wq"], w["kv_norm"], w["wkv"], w["gbias"])


def _attn_scores(qT_ref, k_ref, t, s_ref):
    kc = s_ref.shape[1]
    nkc = k_ref.shape[0] // kc
    if isinstance(t, int):
        qs, off = t // nkc, (t % nkc) * kc
    else:
        qs = lax.div(t, nkc)
        off = pl.multiple_of((t - qs * nkc) * kc, kc)
    def head(h):
        s_ref[h] = _dot(k_ref[pl.ds(off, kc), h * HEAD_PAD:(h + 1) * HEAD_PAD],
                        qT_ref[qs, h * HEAD_PAD:(h + 1) * HEAD_PAD, :])
    return head


def _attn_update(vT_ref, t, s_ref, m_sc, l_sc, acc_sc, scores_head):
    nkc = vT_ref.shape[0]
    qs = lax.div(t, nkc)
    c = t - qs * nkc
    m_all = m_sc[qs]
    l_all = l_sc[qs]
    kc = s_ref.shape[1]
    half = kc // 2
    ones = jnp.ones((2 * SUBLANES, kc), bf16)
    ms, ls = [], []
    for h in range(H_A):
        scores_head(h)
        m_old = m_all[h:h + 1, :]
        m_new = jnp.maximum(m_old, jnp.max(s_ref[h], axis=0, keepdims=True))
        p = jnp.concatenate([jnp.exp2(s_ref[h, 0:half, :] - m_new), jnp.exp2(s_ref[h, half:kc, :] - m_new)],
                            axis=0).astype(bf16)
        alpha = jnp.exp2(m_old - m_new)
        v_aug = jnp.concatenate([vT_ref[c, h * V_DIM:(h + 1) * V_DIM, :], ones], axis=0)
        pv = _dot(v_aug, p)
        ms.append(m_new)
        ls.append(alpha * l_all[h:h + 1, :] + pv[V_DIM:V_DIM + 1, :])
        acc_sc[qs, h * V_DIM:(h + 1) * V_DIM, :] = (alpha * acc_sc[qs, h * V_DIM:(h + 1) * V_DIM, :]
                                                    + pv[0:V_DIM, :])
    m_sc[qs] = jnp.concatenate(ms, axis=0)
    l_sc[qs] = jnp.concatenate(ls, axis=0)


def _attn_kernel(qT_ref, k_ref, vT_ref, kt_ref, vtt_ref, o_ref, m_sc, l_sc, acc_sc, s0_sc, s1_sc):
    ki = pl.program_id(2)
    nqs, _, tq = qT_ref.shape
    nkc = vT_ref.shape[0]
    n = nqs * nkc

    @pl.when(ki == 0)
    def _():
        hk = lax.broadcasted_iota(jnp.int32, (H_A * N_META, H_A * HEAD_PAD), 0) // N_META
        ck = lax.broadcasted_iota(jnp.int32, (H_A * N_META, H_A * HEAD_PAD), 1) // HEAD_PAD
        kt = kt_ref[0:N_META, :]
        k_bd = jnp.where(hk == ck, jnp.concatenate([kt] * H_A, axis=0), jnp.zeros((), bf16))
        rr = lax.broadcasted_iota(jnp.int32, (TAIL, H_A * N_META), 0)
        rc = lax.broadcasted_iota(jnp.int32, (TAIL, H_A * N_META), 1)
        rep = jnp.where((rr < N_META) & (rc % N_META == rr), 1.0, 0.0).astype(bf16)
        hv = lax.broadcasted_iota(jnp.int32, (H_A * V_DIM, H_A * N_META), 0) // V_DIM
        cv = lax.broadcasted_iota(jnp.int32, (H_A * V_DIM, H_A * N_META), 1) // N_META
        v_bd = jnp.where(hv == cv, _dot(vtt_ref[0], rep), 0.0).astype(bf16)
        for qs in range(nqs):
            s = _dot(k_bd, qT_ref[qs])
            ms, ls, ps = [], [], []
            for h in range(H_A):
                sh = s[h * N_META:(h + 1) * N_META, :]
                m = jnp.max(sh, axis=0, keepdims=True)
                p = jnp.exp2(sh - m)
                ms.append(m)
                ls.append(jnp.sum(p, axis=0, keepdims=True))
                ps.append(p)
            m_sc[qs] = jnp.concatenate(ms, axis=0)
            l_sc[qs] = jnp.concatenate(ls, axis=0)
            acc_sc[qs] = _dot(v_bd, jnp.concatenate(ps, axis=0).astype(bf16))

    first = _attn_scores(qT_ref, k_ref, 0, s0_sc)
    for h in range(H_A):
        first(h)

    def pair(i, carry):
        t = 2 * i
        _attn_update(vT_ref, t, s0_sc, m_sc, l_sc, acc_sc, _attn_scores(qT_ref, k_ref, t + 1, s1_sc))
        _attn_update(vT_ref, t + 1, s1_sc, m_sc, l_sc, acc_sc,
                     _attn_scores(qT_ref, k_ref, jnp.minimum(t + 2, n - 1), s0_sc))
        return carry
    lax.fori_loop(0, n // 2, pair, 0)

    @pl.when(ki == pl.num_programs(2) - 1)
    def _():
        for qs in range(nqs):
            inv = 1.0 / l_sc[qs]
            outs = [acc_sc[qs, h * V_DIM:(h + 1) * V_DIM, :] * inv[h:h + 1, :] for h in range(H_A)]
            o = jnp.concatenate(outs, axis=0)
            o_ref[qs * tq:(qs + 1) * tq, :] = o.T.astype(bf16)


def _attention(qT, k, vT, k_tail, vT_tail, *, batch, seq, tqb, tkb, q_is_tail):
    tq = qT.shape[2]
    kc = vT.shape[2]
    nqb = (TAIL if q_is_tail else seq) // tqb
    nkb = seq // tkb
    nqs = tqb // tq
    assert (nqs * (tkb // kc)) % 2 == 0, "the kernel consumes (query sub-block, key chunk) steps in pairs"
    q_rows = qT.shape[0] * tq
    in_specs = [
        pl.BlockSpec((nqs, H_A * HEAD_PAD, tq), lambda b, qi, ki: (b * nqb + qi, 0, 0)),
        pl.BlockSpec((tkb, H_A * HEAD_PAD), lambda b, qi, ki: (b * nkb + ki, 0)),
        pl.BlockSpec((tkb // kc, H_A * V_DIM, kc), lambda b, qi, ki: (b * nkb + ki, 0, 0)),
        pl.BlockSpec((TAIL, H_A * HEAD_PAD), lambda b, qi, ki: (b, 0)),
        pl.BlockSpec((1, H_A * V_DIM, TAIL), lambda b, qi, ki: (b, 0, 0)),
    ]
    return pl.pallas_call(
        _attn_kernel,
        grid=(batch, nqb, nkb), in_specs=in_specs,
        out_specs=pl.BlockSpec((tqb, H_A * V_DIM), lambda b, qi, ki: (b * nqb + qi, 0)),
        out_shape=jax.ShapeDtypeStruct((q_rows, H_A * V_DIM), bf16),
        scratch_shapes=[pltpu.VMEM((nqs, H_A, tq), f32), pltpu.VMEM((nqs, H_A, tq), f32),
                        pltpu.VMEM((nqs, H_A * V_DIM, tq), f32),
                        pltpu.VMEM((H_A, kc, tq), f32), pltpu.VMEM((H_A, kc, tq), f32)],
        compiler_params=_params(("parallel", "parallel", "arbitrary")), name="attention",
    )(qT, k, vT, k_tail, vT_tail)


def _split3(x):
    a = x.astype(bf16)
    r = x - a.astype(f32)
    b = r.astype(bf16)
    c = (r - b.astype(f32)).astype(bf16)
    return a, b, c


def _mlstm_chunk(q, kT, v, g, gT, c_sc, m_sc, *, reverse):
    d = 1 if reverse else 0
    r = lax.broadcasted_iota(jnp.int32, (CHUNK, CHUNK), 0)
    c = lax.broadcasted_iota(jnp.int32, (CHUNK, CHUNK), 1)
    mask = (c >= r) if reverse else (c <= r)
    tri = jnp.where(mask, 1.0, 0.0).astype(bf16)
    triT = jnp.where((r >= c) if reverse else (r <= c), 1.0, 0.0).astype(bf16)
    g3 = _split3(g)
    bcol_all = _dot(tri, g3[0]) + _dot(tri, g3[1]) + _dot(tri, g3[2])
    t3 = _split3(gT)
    brow_all = _dot(t3[0], triT) + _dot(t3[1], triT) + _dot(t3[2], triT)
    lane = lax.broadcasted_iota(jnp.int32, (CHUNK, LANES), 1)
    ones_col = jnp.where(lane == 0, 1.0, 0.0).astype(bf16)
    outs = []
    for h in range(H_B):
        ji = d * H_B + h
        jf = 2 * H_B + d * H_B + h
        bc = bcol_all[:, jf:jf + 1]
        br = brow_all[jf:jf + 1, :]
        ic = g[:, ji:ji + 1]
        ir = gT[ji:ji + 1, :]
        gtot = br[:, 0:1] if reverse else br[:, CHUNK - 1:CHUNK]
        m_prev = m_sc[h]
        qh = q[:, h * DH_B:(h + 1) * DH_B]
        kTh = kT[h * DH_B:(h + 1) * DH_B, :]
        v_aug = jnp.concatenate([v[:, h * DH_B:(h + 1) * DH_B], ones_col], axis=1)
        dmat = jnp.where(mask, bc - br + ir, -jnp.inf)
        inter = bc + m_prev
        m_t = jnp.maximum(jnp.max(dmat, axis=1, keepdims=True), inter)
        p = jnp.exp(dmat - m_t)
        s = (_dot(qh, kTh) * p).astype(bf16)
        e_in = jnp.exp(inter - m_t)
        c_prev = c_sc[h]
        numden = _dot(s, v_aug) + e_in * _dot(qh, c_prev.astype(bf16))
        den = numden[:, DH_B:DH_B + 1]
        outs.append(numden[:, :DH_B] / jnp.maximum(jnp.abs(den), jnp.exp(-m_t)))
        a_col = gtot - bc + ic
        m_loc = jnp.max(a_col, axis=0, keepdims=True)
        m_new = jnp.maximum(gtot + m_prev, m_loc)
        sp = jnp.exp(gtot + m_prev - m_new)
        sl = jnp.exp(m_loc - m_new)
        wv = (jnp.exp(a_col - m_loc) * v_aug.astype(f32)).astype(bf16)
        c_sc[h] = sp * c_prev + sl * _dot(kTh, wv)
        m_sc[h] = m_new
    return jnp.concatenate(outs, axis=1)


def _mlstm_kernel(q_ref, kT_ref, v_ref, g_ref, gT_ref, qt_ref, kTt_ref, vt_ref, gt_ref, gTt_ref,
                  o_ref, ot_ref, c_sc, m_sc, *, reverse):
    j = pl.program_id(1)
    nj = pl.num_programs(1)
    nch = kT_ref.shape[0]
    step = functools.partial(_mlstm_chunk, c_sc=c_sc, m_sc=m_sc, reverse=reverse)

    def tail_chunk():
        ot_ref[...] = step(qt_ref[...], kTt_ref[0], vt_ref[...], gt_ref[...], gTt_ref[0])

    @pl.when(j == 0)
    def _():
        c_sc[...] = jnp.zeros(c_sc.shape, f32)
        m_sc[...] = jnp.full(m_sc.shape, NEG, f32)
        if not reverse:
            tail_chunk()

    for c in (range(nch - 1, -1, -1) if reverse else range(nch)):
        rows = slice(c * CHUNK, (c + 1) * CHUNK)
        o_ref[rows, :] = step(q_ref[rows, :], kT_ref[c], v_ref[rows, :], g_ref[rows, :], gT_ref[c])

    if reverse:
        @pl.when(j == nj - 1)
        def _():
            tail_chunk()


def _mlstm(x_arrs, t_arrs, *, batch, seq, rb, reverse):
    mq, mkT, mv, g, gT = x_arrs
    mq_t, mkT_t, mv_t, g_t, gT_t = t_arrs
    nj = seq // rb
    nch = rb // CHUNK
    blk = (lambda b, j: b * nj + (nj - 1 - j)) if reverse else (lambda b, j: b * nj + j)
    in_specs = [
        pl.BlockSpec((rb, H_B * DH_B), lambda b, j: (blk(b, j), 0)),
        pl.BlockSpec((nch, H_B * DH_B, CHUNK), lambda b, j: (blk(b, j), 0, 0)),
        pl.BlockSpec((rb, H_B * DH_B), lambda b, j: (blk(b, j), 0)),
        pl.BlockSpec((rb, LANES), lambda b, j: (blk(b, j), 0)),
        pl.BlockSpec((nch, 4 * H_B, CHUNK), lambda b, j: (blk(b, j), 0, 0)),
        pl.BlockSpec((TAIL, H_B * DH_B), lambda b, j: (b, 0)),
        pl.BlockSpec((1, H_B * DH_B, CHUNK), lambda b, j: (b, 0, 0)),
        pl.BlockSpec((TAIL, H_B * DH_B), lambda b, j: (b, 0)),
        pl.BlockSpec((TAIL, LANES), lambda b, j: (b, 0)),
        pl.BlockSpec((1, 4 * H_B, CHUNK), lambda b, j: (b, 0, 0)),
    ]
    out_specs = (pl.BlockSpec((rb, H_B * DH_B), lambda b, j: (blk(b, j), 0)),
                 pl.BlockSpec((TAIL, H_B * DH_B), lambda b, j: (b, 0)))
    out_shape = (jax.ShapeDtypeStruct((batch * seq, H_B * DH_B), f32),
                 jax.ShapeDtypeStruct((batch * TAIL, H_B * DH_B), f32))
    return pl.pallas_call(
        functools.partial(_mlstm_kernel, reverse=reverse),
        grid=(batch, nj), in_specs=in_specs, out_specs=out_specs, out_shape=out_shape,
        scratch_shapes=[pltpu.VMEM((H_B, DH_B, 2 * DH_B), f32), pltpu.VMEM((H_B, 1, 1), f32)],
        compiler_params=_params(("parallel", "arbitrary")), name="mlstm_bwd" if reverse else "mlstm_fwd",
    )(mq, mkT, mv, g, gT, mq_t, mkT_t, mv_t, g_t, gT_t)


def _merge_kernel(attn_ref, hf_ref, hb_ref, so_ref, sga_ref, sgb_ref, res_ref, mn_ref, woa_ref, wom_ref, wout_ref,
                  o_ref):
    hm = hf_ref[...] + hb_ref[...]
    mn = mn_ref[...]
    parts = [_rms(hm[:, h * DH_B:(h + 1) * DH_B], mn[:, h * DH_B:(h + 1) * DH_B]) for h in range(H_B)]
    hmn = (jnp.concatenate(parts, axis=1) * so_ref[...].astype(f32)).astype(bf16)
    y = (sga_ref[...].astype(f32) * _dot(attn_ref[...], woa_ref[...])
         + sgb_ref[...].astype(f32) * _dot(hmn, wom_ref[...]))
    o_ref[...] = res_ref[...] + _dot(y.astype(bf16), wout_ref[...])


def _merge(attn, hf, hb, so, sga, sgb, res, w, *, tm):
    rows = attn.shape[0]
    row_spec = lambda width: pl.BlockSpec((tm, width), lambda i: (i, 0))
    in_specs = [row_spec(H_A * V_DIM), row_spec(H_B * DH_B), row_spec(H_B * DH_B), row_spec(H_B * DH_B),
                row_spec(D_MODEL), row_spec(D_MODEL), row_spec(D_MODEL),
                _const_spec((1, H_B * DH_B)), _const_spec(w["woa"].shape), _const_spec(w["wom"].shape),
                _const_spec(w["wout"].shape)]
    return pl.pallas_call(
        _merge_kernel, grid=(rows // tm,), in_specs=in_specs, out_specs=row_spec(D_MODEL),
        out_shape=jax.ShapeDtypeStruct((rows, D_MODEL), f32),
        compiler_params=_params(("parallel",)), name="merge",
    )(attn, hf, hb, so, sga, sgb, res, w["mlstm_norm"], w["woa"], w["wom"], w["wout"])


def _gelu_tanh(x):
    return 0.5 * x * (1.0 + jnp.tanh(math.sqrt(2.0 / math.pi) * (x + 0.044715 * (x * x * x))))


def _ffn_kernel(h_ref, prev_ref, prevt_ref, next_ref, nf_ref, wup_ref, wgate_ref, cw_ref, cb_ref, wdown_ref,
                nfin_ref, o_ref, *, tiles_per_batch):
    i = pl.program_id(0)
    tm = h_ref.shape[0]
    first = (i % tiles_per_batch) == 0
    last = (i % tiles_per_batch) == tiles_per_batch - 1
    nf = nf_ref[...]
    h = h_ref[...]
    u = _rms(h, nf).astype(bf16)
    prev = jnp.where(first, prevt_ref[...], prev_ref[...])
    u_prev = _rms(prev, nf).astype(bf16)
    u_next = _rms(next_ref[...], nf).astype(bf16)
    row = lax.broadcasted_iota(jnp.int32, (tm, 1), 0)
    acc = h
    for lo, hi in FF_CHUNKS:
        wup = wup_ref[:, lo:hi]
        a = _dot(u, wup)
        a_prev = _dot(u_prev, wup)[SUBLANES - 1:SUBLANES, :]
        a_next = jnp.where(last, 0.0, _dot(u_next, wup)[0:1, :])
        a_dn = jnp.where(row == 0, a_prev, pltpu.roll(a, 1, axis=0))
        a_up = jnp.where(row == tm - 1, a_next, pltpu.roll(a, tm - 1, axis=0))
        cw = cw_ref[:, lo:hi]
        conv = a_dn * cw[0:1, :] + a * cw[1:2, :] + a_up * cw[2:3, :] + cb_ref[:, lo:hi]
        gated = (_gelu_tanh(conv) * _dot(u, wgate_ref[:, lo:hi])).astype(bf16)
        acc = acc + _dot(gated, wdown_ref[lo:hi, :])
    o_ref[...] = _rms(acc, nfin_ref[...])


def _ffn(h1, h1_tail, w, *, seq, tm):
    rows = h1.shape[0]
    n = rows // tm
    tpb = seq // tm
    hb = tm // SUBLANES
    in_specs = [
        pl.BlockSpec((tm, D_MODEL), lambda i: (i, 0)),
        pl.BlockSpec((SUBLANES, D_MODEL), lambda i: (jnp.maximum(i * hb - 1, 0), 0)),
        pl.BlockSpec((SUBLANES, D_MODEL), lambda i: ((i // tpb) * (TAIL // SUBLANES) + N_META // SUBLANES - 1, 0)),
        pl.BlockSpec((SUBLANES, D_MODEL), lambda i: (jnp.minimum((i + 1) * hb, n * hb - 1), 0)),
        _const_spec((1, D_MODEL)), _const_spec(w["wup"].shape), _const_spec(w["wgate"].shape),
        _const_spec((SUBLANES, D_FF)), _const_spec((1, D_FF)), _const_spec(w["wdown"].shape),
        _const_spec((1, D_MODEL)),
    ]
    return pl.pallas_call(
        functools.partial(_ffn_kernel, tiles_per_batch=tpb),
        grid=(n,), in_specs=in_specs, out_specs=pl.BlockSpec((tm, D_MODEL), lambda i: (i, 0)),
        out_shape=jax.ShapeDtypeStruct((rows, D_MODEL), f32),
        compiler_params=_params(("parallel",)), name="conv_ffn",
    )(h1, h1, h1_tail, h1, w["norm_ffn"], w["wup"], w["wgate"], w["conv_w"], w["conv_b"], w["wdown"],
      w["norm_final"])


def _prep_weights(norm_mix, w_in, q_norm, w_uq, kv_norm, w_ukv, b_igate, b_fgate, mlstm_norm, w_o_attn, w_o_mlstm,
                  w_out, norm_ffn, w_up, w_gate, conv_w, conv_b, w_down, norm_final):
    w = w_in[0]
    offs = [0]
    for s in IN_SIZES:
        offs.append(offs[-1] + s)
    col = lambda j: w[:, offs[j]:offs[j + 1]]
    zeros = lambda r, c: jnp.zeros((r, c), f32)
    w_kr = col(2)
    half = ROPE // 2
    kr_main = jnp.concatenate([zeros(D_MODEL, NOPE), w_kr, zeros(D_MODEL, HEAD_PAD - QK_DIM)], axis=1)
    kr_rot = jnp.concatenate([zeros(D_MODEL, NOPE), -w_kr[:, half:], w_kr[:, :half],
                              zeros(D_MODEL, HEAD_PAD - QK_DIM)], axis=1)
    wa = jnp.concatenate([col(0), col(1), kr_main, kr_rot], axis=1)
    wb = jnp.concatenate([col(3), col(4), col(5), col(6), col(7), col(8), zeros(D_MODEL, LANES - 4 * H_B)], axis=1)
    wc = jnp.concatenate([col(9), col(10)], axis=1)
    wuq = w_uq[0].reshape(Q_LORA, H_A, QK_DIM)
    nope, rope = wuq[:, :, :NOPE], wuq[:, :, NOPE:]
    zq = lambda c: jnp.zeros((Q_LORA, H_A, c), f32)
    q_main = jnp.concatenate([nope, rope, zq(HEAD_PAD - QK_DIM)], axis=-1).reshape(Q_LORA, H_A * HEAD_PAD)
    q_rot = jnp.concatenate([zq(NOPE), -rope[:, :, half:], rope[:, :, :half], zq(HEAD_PAD - QK_DIM)],
                            axis=-1).reshape(Q_LORA, H_A * HEAD_PAD)
    wq = jnp.concatenate([q_main, q_rot], axis=1)
    wukv = w_ukv[0].reshape(KV_LORA, H_A, NOPE + V_DIM)
    wk = jnp.concatenate([wukv[:, :, :NOPE], jnp.zeros((KV_LORA, H_A, HEAD_PAD - NOPE), f32)],
                         axis=-1).reshape(KV_LORA, H_A * HEAD_PAD)
    wv = wukv[:, :, NOPE:].reshape(KV_LORA, H_A * V_DIM)
    wkv = jnp.concatenate([wk, wv], axis=1)
    gbias = jnp.concatenate([b_igate[0].reshape(-1), b_fgate[0].reshape(-1), jnp.zeros((LANES - 4 * H_B,), f32)])
    cw = jnp.concatenate([conv_w[0], jnp.zeros((SUBLANES - CONV_W, D_FF), f32)], axis=0)
    return dict(
        norm_mix=norm_mix[0][None], wa=wa.astype(bf16), wb=wb.astype(bf16), wc=wc.astype(bf16),
        q_norm=q_norm[0][None], wq=wq.astype(bf16), kv_norm=kv_norm[0][None], wkv=wkv.astype(bf16),
        gbias=gbias[None], mlstm_norm=mlstm_norm[0][None], woa=w_o_attn[0].astype(bf16),
        wom=w_o_mlstm[0].astype(bf16), wout=w_out[0].astype(bf16), norm_ffn=norm_ffn[0][None],
        wup=w_up[0].astype(bf16), wgate=w_gate[0].astype(bf16), conv_w=cw, conv_b=conv_b[0][None],
        wdown=w_down[0].astype(bf16), norm_final=norm_final[None])


def _rope_tables(positions):
    inv = ROPE_THETA ** (-jnp.arange(0, ROPE, 2, dtype=f32) / ROPE)
    ang = positions.astype(f32)[:, None] * inv[None, :]
    n = positions.shape[0]
    cos, sin = jnp.cos(ang), jnp.sin(ang)
    cm = jnp.concatenate([jnp.ones((n, NOPE), f32), cos, cos, jnp.zeros((n, HEAD_PAD - QK_DIM), f32)], axis=1)
    sm = jnp.concatenate([jnp.zeros((n, NOPE), f32), sin, sin, jnp.zeros((n, HEAD_PAD - QK_DIM), f32)], axis=1)
    return cm, sm


def _tiles(batch, seq):
    tm = math.gcd(seq, 512)
    tm_tail = math.gcd(batch * TAIL, 512)
    tkb = math.gcd(seq, 2048)
    tqb = math.gcd(seq, 512 if seq // tkb == 1 else 1024)
    return dict(tm=tm, tm_tail=tm_tail, tqb=tqb, tkb=tkb, rb=tm)


def _encode(x, meta_tokens, w):
    batch, seq, _ = x.shape
    t = _tiles(batch, seq)
    xf = x.reshape(batch * seq, D_MODEL)
    tail_one = jnp.concatenate([meta_tokens.astype(f32), jnp.zeros((TAIL - N_META, D_MODEL), f32)], axis=0)
    tailf = jnp.tile(tail_one, (batch, 1))

    cm_x, sm_x = _rope_tables(N_META + jnp.arange(seq))
    pos_t = jnp.tile(jnp.arange(TAIL), t["tm_tail"] // TAIL)
    cm_t, sm_t = _rope_tables(pos_t)
    nper = seq // t["tm"]
    xs = _inproj(xf, cm_x, sm_x, lambda i: (i % nper, 0), w, tm=t["tm"], cq=TQ, ck=KC, mask_tail=False)
    ts = _inproj(tailf, cm_t, sm_t, lambda i: (0, 0), w, tm=t["tm_tail"], cq=TAIL, ck=TAIL, mask_tail=True)
    qT, k, vT, mq, mkT, mv, so, g, gT, sga, sgb = xs
    qT_t, k_t, vT_t, mq_t, mkT_t, mv_t, so_t, g_t, gT_t, sga_t, sgb_t = ts

    attn = _attention(qT, k, vT, k_t, vT_t, batch=batch, seq=seq, tqb=t["tqb"], tkb=t["tkb"], q_is_tail=False)
    attn_t = _attention(qT_t, k, vT, k_t, vT_t, batch=batch, seq=seq, tqb=TAIL, tkb=t["tkb"], q_is_tail=True)

    x_arrs = (mq, mkT, mv, g, gT)
    t_arrs = (mq_t, mkT_t, mv_t, g_t, gT_t)
    hf, hf_t = _mlstm(x_arrs, t_arrs, batch=batch, seq=seq, rb=t["rb"], reverse=False)
    hb, hb_t = _mlstm(x_arrs, t_arrs, batch=batch, seq=seq, rb=t["rb"], reverse=True)

    h1 = _merge(attn, hf, hb, so, sga, sgb, xf, w, tm=t["tm"])
    h1_t = _merge(attn_t, hf_t, hb_t, so_t, sga_t, sgb_t, tailf, w, tm=t["tm_tail"])
    y = _ffn(h1, h1_t, w, seq=seq, tm=t["tm"])
    return y.reshape(batch, seq, D_MODEL)


def kernel(x_prompt, x_sample, meta_tokens, norm_mix, w_in, q_norm, w_uq, kv_norm, w_ukv, b_igate, b_fgate,
           mlstm_norm, w_o_attn, w_o_mlstm, w_out, norm_ffn, w_up, w_gate, conv_w, conv_b, w_down, norm_final):
    w = _prep_weights(norm_mix, w_in, q_norm, w_uq, kv_norm, w_ukv, b_igate, b_fgate, mlstm_norm, w_o_attn,
                      w_o_mlstm, w_out, norm_ffn, w_up, w_gate, conv_w, conv_b, w_down, norm_final)
    return (_encode(x_prompt, meta_tokens, w), _encode(x_sample, meta_tokens, w))
```

```python
import functools
import math

import jax
import jax.numpy as jnp
from jax import lax
from jax.experimental import pallas as pl
from jax.experimental.pallas import tpu as pltpu

D_MODEL = 1024
N_META = 16
H_A = 8
Q_LORA = 256
KV_LORA = 128
NOPE = 64
ROPE = 32
QK_DIM = NOPE + ROPE
V_DIM = 64
ROPE_THETA = 10000.0
H_B = 4
DH_B = 128
D_FF = 2816
CONV_W = 3
EPS = 1e-6
NEG = -1e30
IN_SIZES = (Q_LORA, KV_LORA, ROPE, H_B * DH_B, H_B * DH_B, H_B * DH_B, H_B * DH_B, 2 * H_B, 2 * H_B, D_MODEL, D_MODEL)

LANES = 128
SUBLANES = 8
TAIL = 128
HEAD_PAD = LANES
CHUNK = 128
TQ = 256
KC = 512
VMEM_LIMIT = 56 * 1024 * 1024
QSCALE = (QK_DIM ** -0.5) * math.log2(math.e)
FF_CHUNKS = ((0, 1024), (1024, 2048), (2048, D_FF))

f32 = jnp.float32
bf16 = jnp.bfloat16


def _rms(x, g):
    return x * lax.rsqrt(jnp.mean(x * x, axis=-1, keepdims=True) + EPS) * g


def _sigmoid(x):
    return 1.0 / (1.0 + jnp.exp(-x))


def _log_sigmoid(x):
    return jnp.minimum(x, 0.0) - jnp.log1p(jnp.exp(-jnp.abs(x)))


def _dot(a, b):
    return jnp.dot(a, b, preferred_element_type=f32)


def _const_spec(shape):
    zeros = (0,) * len(shape)
    return pl.BlockSpec(shape, lambda *_: zeros, pipeline_mode=pl.Buffered(1))


def _params(sem):
    return pltpu.CompilerParams(dimension_semantics=sem, vmem_limit_bytes=VMEM_LIMIT)


def _inproj_kernel(h_ref, cm_ref, sm_ref, nmix_ref, wa_ref, wb_ref, wc_ref, qn_ref, wq_ref, kvn_ref, wkv_ref,
                   gb_ref, qT_ref, k_ref, vT_ref, mq_ref, mkT_ref, mv_ref, so_ref, g_ref, gT_ref, sga_ref, sgb_ref,
                   *, cq, ck, mask_tail):
    tm = h_ref.shape[0]
    u = _rms(h_ref[...], nmix_ref[...]).astype(bf16)
    cm = cm_ref[...]
    sm = sm_ref[...]

    za = _dot(u, wa_ref[...])
    qn = _rms(za[:, :Q_LORA], qn_ref[...]).astype(bf16)
    kvn = _rms(za[:, Q_LORA:Q_LORA + KV_LORA], kvn_ref[...]).astype(bf16)
    kr = za[:, 384:512] * cm + za[:, 512:640] * sm
    cm8 = jnp.concatenate([cm] * H_A, axis=1)
    sm8 = jnp.concatenate([sm] * H_A, axis=1)
    q = (_dot(qn, wq_ref[:, :H_A * HEAD_PAD]) * cm8 + _dot(qn, wq_ref[:, H_A * HEAD_PAD:]) * sm8) * QSCALE
    qT = q.T
    for c in range(tm // cq):
        qT_ref[c] = qT[:, c * cq:(c + 1) * cq].astype(bf16)
    kv = _dot(kvn, wkv_ref[...])
    k_ref[...] = (kv[:, :H_A * HEAD_PAD] + jnp.concatenate([kr] * H_A, axis=1)).astype(bf16)
    vT = kv[:, H_A * HEAD_PAD:].T
    for c in range(tm // ck):
        vT_ref[c] = vT[:, c * ck:(c + 1) * ck].astype(bf16)

    zb = _dot(u, wb_ref[...])
    mq_ref[...] = zb[:, 0:512].astype(bf16)
    mkT = (zb[:, 512:1024] * (DH_B ** -0.5)).T
    for c in range(tm // CHUNK):
        mkT_ref[c] = mkT[:, c * CHUNK:(c + 1) * CHUNK].astype(bf16)
    mv_ref[...] = zb[:, 1024:1536].astype(bf16)
    so_ref[...] = _sigmoid(zb[:, 1536:2048]).astype(bf16)
    gz = zb[:, 2048:2176] + gb_ref[...]
    lane = lax.broadcasted_iota(jnp.int32, gz.shape, 1)
    gates = jnp.where(lane < 2 * H_B, gz, _log_sigmoid(gz))
    if mask_tail:
        row = lax.broadcasted_iota(jnp.int32, gz.shape, 0)
        pad = (row % TAIL) >= N_META
        gates = jnp.where(pad, jnp.where(lane < 2 * H_B, NEG, 0.0), gates)
    g_ref[...] = gates
    gT = gates.T
    for c in range(tm // CHUNK):
        gT_ref[c] = gT[:4 * H_B, c * CHUNK:(c + 1) * CHUNK]

    zc = _dot(u, wc_ref[...])
    sga_ref[...] = _sigmoid(zc[:, :D_MODEL]).astype(bf16)
    sgb_ref[...] = _sigmoid(zc[:, D_MODEL:]).astype(bf16)


def _inproj(h, cm, sm, tab_map, w, *, tm, cq, ck, mask_tail):
    rows = h.shape[0]
    n = rows // tm
    row_spec = lambda width: pl.BlockSpec((tm, width), lambda i: (i, 0))
    chunk_spec = lambda height, cw: pl.BlockSpec((tm // cw, height, cw), lambda i: (i, 0, 0))
    out_shape = (
        jax.ShapeDtypeStruct((rows // cq, H_A * HEAD_PAD, cq), bf16),
        jax.ShapeDtypeStruct((rows, H_A * HEAD_PAD), bf16),
        jax.ShapeDtypeStruct((rows // ck, H_A * V_DIM, ck), bf16),
        jax.ShapeDtypeStruct((rows, H_B * DH_B), bf16),
        jax.ShapeDtypeStruct((rows // CHUNK, H_B * DH_B, CHUNK), bf16),
        jax.ShapeDtypeStruct((rows, H_B * DH_B), bf16),
        jax.ShapeDtypeStruct((rows, H_B * DH_B), bf16),
        jax.ShapeDtypeStruct((rows, LANES), f32),
        jax.ShapeDtypeStruct((rows // CHUNK, 4 * H_B, CHUNK), f32),
        jax.ShapeDtypeStruct((rows, D_MODEL), bf16),
        jax.ShapeDtypeStruct((rows, D_MODEL), bf16),
    )
    out_specs = (
        chunk_spec(H_A * HEAD_PAD, cq), row_spec(H_A * HEAD_PAD), chunk_spec(H_A * V_DIM, ck),
        row_spec(H_B * DH_B), chunk_spec(H_B * DH_B, CHUNK), row_spec(H_B * DH_B), row_spec(H_B * DH_B),
        row_spec(LANES), chunk_spec(4 * H_B, CHUNK), row_spec(D_MODEL), row_spec(D_MODEL),
    )
    in_specs = [
        row_spec(D_MODEL),
        pl.BlockSpec((tm, LANES), tab_map), pl.BlockSpec((tm, LANES), tab_map),
        _const_spec((1, D_MODEL)), _const_spec(w["wa"].shape), _const_spec(w["wb"].shape),
        _const_spec(w["wc"].shape), _const_spec((1, Q_LORA)), _const_spec(w["wq"].shape),
        _const_spec((1, KV_LORA)), _const_spec(w["wkv"].shape), _const_spec((1, LANES)),
    ]
    return pl.pallas_call(
        functools.partial(_inproj_kernel, cq=cq, ck=ck, mask_tail=mask_tail),
        grid=(n,), in_specs=in_specs, out_specs=out_specs, out_shape=out_shape,
        compiler_params=_params(("parallel",)), name="inproj",
    )(h, cm, sm, w["norm_mix"], w["wa"], w["wb"], w["wc"], w["q_norm"], w["wq"], w["kv_norm"], w["wkv"], w["gbias"])


def _attn_scores(qT_ref, k_ref, t, s_ref):
    kc = s_ref.shape[1]
    nkc = k_ref.shape[0] // kc
    if isinstance(t, int):
        qs, off = t // nkc, (t % nkc) * kc
    else:
        qs = lax.div(t, nkc)
        off = pl.multiple_of((t - qs * nkc) * kc, kc)
    def head(h):
        s_ref[h] = _dot(k_ref[pl.ds(off, kc), h * HEAD_PAD:(h + 1) * HEAD_PAD],
                        qT_ref[qs, h * HEAD_PAD:(h + 1) * HEAD_PAD, :])
    return head


def _attn_update(vT_ref, t, s_ref, m_sc, l_sc, acc_sc, scores_head):
    nkc = vT_ref.shape[0]
    if isinstance(t, int):
        qs, c = t // nkc, t % nkc
    else:
        qs = lax.div(t, nkc)
        c = t - qs * nkc
    m_all = m_sc[qs]
    l_all = l_sc[qs]
    kc = s_ref.shape[1]
    half = kc // 2
    ones = jnp.ones((2 * SUBLANES, kc), bf16)
    ms, ls = [], []
    for h in range(H_A):
        if scores_head is not None:
            scores_head(h)
        m_old = m_all[h:h + 1, :]
        m_new = jnp.maximum(m_old, jnp.max(s_ref[h], axis=0, keepdims=True))
        p = jnp.concatenate([jnp.exp2((s_ref[h, 0:half, :] - m_new).astype(bf16)),
                             jnp.exp2((s_ref[h, half:kc, :] - m_new).astype(bf16))], axis=0)
        alpha = jnp.exp2(m_old - m_new)
        v_aug = jnp.concatenate([vT_ref[c, h * V_DIM:(h + 1) * V_DIM, :], ones], axis=0)
        pv = _dot(v_aug, p)
        ms.append(m_new)
        ls.append(alpha * l_all[h:h + 1, :] + pv[V_DIM:V_DIM + 1, :])
        acc_sc[qs, h * V_DIM:(h + 1) * V_DIM, :] = (alpha * acc_sc[qs, h * V_DIM:(h + 1) * V_DIM, :]
                                                    + pv[0:V_DIM, :])
    m_sc[qs] = jnp.concatenate(ms, axis=0)
    l_sc[qs] = jnp.concatenate(ls, axis=0)


def _attn_kernel(qT_ref, k_ref, vT_ref, kt_ref, vtt_ref, o_ref, m_sc, l_sc, acc_sc, s0_sc, s1_sc, *, single_kv_block):
    ki = pl.program_id(2)
    nqs, _, tq = qT_ref.shape
    nkc = vT_ref.shape[0]
    n = nqs * nkc

    def init():
        hk = lax.broadcasted_iota(jnp.int32, (H_A * N_META, H_A * HEAD_PAD), 0) // N_META
        ck = lax.broadcasted_iota(jnp.int32, (H_A * N_META, H_A * HEAD_PAD), 1) // HEAD_PAD
        kt = kt_ref[0:N_META, :]
        k_bd = jnp.where(hk == ck, jnp.concatenate([kt] * H_A, axis=0), jnp.zeros((), bf16))
        rr = lax.broadcasted_iota(jnp.int32, (TAIL, H_A * N_META), 0)
        rc = lax.broadcasted_iota(jnp.int32, (TAIL, H_A * N_META), 1)
        rep = jnp.where((rr < N_META) & (rc % N_META == rr), 1.0, 0.0).astype(bf16)
        hv = lax.broadcasted_iota(jnp.int32, (H_A * V_DIM, H_A * N_META), 0) // V_DIM
        cv = lax.broadcasted_iota(jnp.int32, (H_A * V_DIM, H_A * N_META), 1) // N_META
        v_bd = jnp.where(hv == cv, _dot(vtt_ref[0], rep), 0.0).astype(bf16)
        for qs in range(nqs):
            s = _dot(k_bd, qT_ref[qs])
            ms, ls, ps = [], [], []
            for h in range(H_A):
                sh = s[h * N_META:(h + 1) * N_META, :]
                m = jnp.max(sh, axis=0, keepdims=True)
                p = jnp.exp2(sh - m)
                ms.append(m)
                ls.append(jnp.sum(p, axis=0, keepdims=True))
                ps.append(p)
            m_sc[qs] = jnp.concatenate(ms, axis=0)
            l_sc[qs] = jnp.concatenate(ls, axis=0)
            acc_sc[qs] = _dot(v_bd, jnp.concatenate(ps, axis=0).astype(bf16))

    def finalize():
        for qs in range(nqs):
            inv = 1.0 / l_sc[qs]
            outs = [acc_sc[qs, h * V_DIM:(h + 1) * V_DIM, :] * inv[h:h + 1, :] for h in range(H_A)]
            o = jnp.concatenate(outs, axis=0)
            o_ref[qs * tq:(qs + 1) * tq, :] = o.T.astype(bf16)

    if single_kv_block:
        init()
    else:
        pl.when(ki == 0)(init)

    first = _attn_scores(qT_ref, k_ref, 0, s0_sc)
    for h in range(H_A):
        first(h)

    def pair(i, carry):
        t = 2 * i
        _attn_update(vT_ref, t, s0_sc, m_sc, l_sc, acc_sc, _attn_scores(qT_ref, k_ref, t + 1, s1_sc))
        _attn_update(vT_ref, t + 1, s1_sc, m_sc, l_sc, acc_sc, _attn_scores(qT_ref, k_ref, t + 2, s0_sc))
        return carry
    lax.fori_loop(0, n // 2 - 1, pair, 0)
    _attn_update(vT_ref, n - 2, s0_sc, m_sc, l_sc, acc_sc, _attn_scores(qT_ref, k_ref, n - 1, s1_sc))
    _attn_update(vT_ref, n - 1, s1_sc, m_sc, l_sc, acc_sc, None)

    if single_kv_block:
        finalize()
    else:
        pl.when(ki == pl.num_programs(2) - 1)(finalize)


def _attention(qT, k, vT, k_tail, vT_tail, *, batch, seq, tqb, tkb, q_is_tail):
    tq = qT.shape[2]
    kc = vT.shape[2]
    nqb = (TAIL if q_is_tail else seq) // tqb
    nkb = seq // tkb
    nqs = tqb // tq
    assert (nqs * (tkb // kc)) % 2 == 0, "the kernel consumes (query sub-block, key chunk) steps in pairs"
    q_rows = qT.shape[0] * tq
    in_specs = [
        pl.BlockSpec((nqs, H_A * HEAD_PAD, tq), lambda b, qi, ki: (b * nqb + qi, 0, 0)),
        pl.BlockSpec((tkb, H_A * HEAD_PAD), lambda b, qi, ki: (b * nkb + ki, 0)),
        pl.BlockSpec((tkb // kc, H_A * V_DIM, kc), lambda b, qi, ki: (b * nkb + ki, 0, 0)),
        pl.BlockSpec((TAIL, H_A * HEAD_PAD), lambda b, qi, ki: (b, 0)),
        pl.BlockSpec((1, H_A * V_DIM, TAIL), lambda b, qi, ki: (b, 0, 0)),
    ]
    return pl.pallas_call(
        functools.partial(_attn_kernel, single_kv_block=(nkb == 1)),
        grid=(batch, nqb, nkb), in_specs=in_specs,
        out_specs=pl.BlockSpec((tqb, H_A * V_DIM), lambda b, qi, ki: (b * nqb + qi, 0)),
        out_shape=jax.ShapeDtypeStruct((q_rows, H_A * V_DIM), bf16),
        scratch_shapes=[pltpu.VMEM((nqs, H_A, tq), f32), pltpu.VMEM((nqs, H_A, tq), f32),
                        pltpu.VMEM((nqs, H_A * V_DIM, tq), f32),
                        pltpu.VMEM((H_A, kc, tq), f32), pltpu.VMEM((H_A, kc, tq), f32)],
        compiler_params=_params(("parallel", "parallel", "arbitrary")), name="attention",
    )(qT, k, vT, k_tail, vT_tail)


def _split3(x):
    a = x.astype(bf16)
    r = x - a.astype(f32)
    b = r.astype(bf16)
    c = (r - b.astype(f32)).astype(bf16)
    return a, b, c


def _mlstm_gates(g, gT, *, reverse):
    r = lax.broadcasted_iota(jnp.int32, (CHUNK, CHUNK), 0)
    c = lax.broadcasted_iota(jnp.int32, (CHUNK, CHUNK), 1)
    tri = jnp.where((c >= r) if reverse else (c <= r), 1.0, 0.0).astype(bf16)
    triT = jnp.where((r >= c) if reverse else (r <= c), 1.0, 0.0).astype(bf16)
    g3 = _split3(g)
    bcol_all = _dot(tri, g3[0]) + _dot(tri, g3[1]) + _dot(tri, g3[2])
    t3 = _split3(gT)
    brow_all = _dot(t3[0], triT) + _dot(t3[1], triT) + _dot(t3[2], triT)
    return bcol_all, brow_all


def _mlstm_head(h, qk, cums, q, kT, v, gT, c_sc, m_sc, *, reverse):
    d = 1 if reverse else 0
    bcol_all, brow_all = cums
    r = lax.broadcasted_iota(jnp.int32, (CHUNK, CHUNK), 0)
    c = lax.broadcasted_iota(jnp.int32, (CHUNK, CHUNK), 1)
    mask = (c >= r) if reverse else (c <= r)
    ones_blk = jnp.ones((CHUNK, LANES), bf16)
    tile_shape = (SUBLANES, LANES)
    ji = d * H_B + h
    jf = 2 * H_B + d * H_B + h
    bcb = jnp.broadcast_to(bcol_all[:, jf:jf + 1], (CHUNK, LANES))
    br = brow_all[jf:jf + 1, :]
    a_row = gT[ji:ji + 1, :] - br
    gtot = jnp.broadcast_to(br[:, 0:1] if reverse else br[:, CHUNK - 1:CHUNK], tile_shape)
    m_prev = m_sc[h]
    mpb = jnp.concatenate([m_prev] * (CHUNK // SUBLANES), axis=0)
    qh = q[:, h * DH_B:(h + 1) * DH_B]
    kTh = kT[h * DH_B:(h + 1) * DH_B, :]
    v_aug = jnp.concatenate([v[:, h * DH_B:(h + 1) * DH_B], ones_blk], axis=1)
    amask = jnp.where(mask, a_row, -jnp.inf)
    mt = jnp.maximum(jnp.broadcast_to(jnp.max(amask, axis=1, keepdims=True), (CHUNK, LANES)), mpb)
    s = (qk * jnp.exp(amask - mt)).astype(bf16)
    e_in = jnp.exp(mpb - mt)
    c_prev = c_sc[h]
    numden = _dot(s, v_aug) + jnp.concatenate([e_in, e_in], axis=1) * _dot(qh, c_prev.astype(bf16))
    den = numden[:, DH_B:]
    out = numden[:, :DH_B] * (1.0 / jnp.maximum(jnp.abs(den), jnp.exp(-(bcb + mt))))
    a_end = gtot[0:1, :] + a_row
    m_loc = jnp.broadcast_to(jnp.max(a_end, axis=1, keepdims=True), tile_shape)
    m_new = jnp.maximum(gtot + m_prev, m_loc)
    sp = jnp.exp(gtot + m_prev - m_new)
    sl = jnp.exp(m_loc - m_new)
    kw = (kTh.astype(f32) * jnp.exp(a_end - m_loc[0:1, :])).astype(bf16)
    rep = lambda x: jnp.concatenate([jnp.concatenate([x] * (DH_B // SUBLANES), axis=0)] * 2, axis=1)
    c_sc[h] = rep(sp) * c_prev + rep(sl) * _dot(kw, v_aug)
    m_sc[h] = m_new
    return out


def _mlstm_chunks(loaders, stores, c_sc, m_sc, *, reverse):
    qk_head = lambda q, kT, h: _dot(q[:, h * DH_B:(h + 1) * DH_B], kT[h * DH_B:(h + 1) * DH_B, :])
    cur = loaders[0]()
    cums = _mlstm_gates(cur[3], cur[4], reverse=reverse)
    qks = [qk_head(cur[0], cur[1], h) for h in range(H_B)]
    for i, store in enumerate(stores):
        q, kT, v, _, gT = cur
        nxt = loaders[i + 1]() if i + 1 < len(loaders) else None
        cums_n = _mlstm_gates(nxt[3], nxt[4], reverse=reverse) if nxt is not None else None
        qks_n, outs = [], []
        for h in range(H_B):
            if nxt is not None:
                qks_n.append(qk_head(nxt[0], nxt[1], h))
            outs.append(_mlstm_head(h, qks[h], cums, q, kT, v, gT, c_sc, m_sc, reverse=reverse))
        store(jnp.concatenate(outs, axis=1))
        cur, cums, qks = nxt, cums_n, qks_n


def _mlstm_kernel(q_ref, kT_ref, v_ref, g_ref, gT_ref, qt_ref, kTt_ref, vt_ref, gt_ref, gTt_ref,
                  o_ref, ot_ref, c_sc, m_sc, *, reverse):
    j = pl.program_id(1)
    nj = pl.num_programs(1)
    nch = kT_ref.shape[0]
    run = functools.partial(_mlstm_chunks, c_sc=c_sc, m_sc=m_sc, reverse=reverse)

    def tail_chunk():
        def store(o):
            ot_ref[...] = o
        run([lambda: (qt_ref[...], kTt_ref[0], vt_ref[...], gt_ref[...], gTt_ref[0])], [store])

    @pl.when(j == 0)
    def _():
        c_sc[...] = jnp.zeros(c_sc.shape, f32)
        m_sc[...] = jnp.full(m_sc.shape, NEG, f32)
        if not reverse:
            tail_chunk()

    def loader(c):
        rows = slice(c * CHUNK, (c + 1) * CHUNK)
        return lambda: (q_ref[rows, :], kT_ref[c], v_ref[rows, :], g_ref[rows, :], gT_ref[c])

    def storer(c):
        def store(o):
            o_ref[c * CHUNK:(c + 1) * CHUNK, :] = o
        return store

    order = list(range(nch - 1, -1, -1) if reverse else range(nch))
    run([loader(c) for c in order], [storer(c) for c in order])

    if reverse:
        @pl.when(j == nj - 1)
        def _():
            tail_chunk()


def _mlstm(x_arrs, t_arrs, *, batch, seq, rb, reverse):
    mq, mkT, mv, g, gT = x_arrs
    mq_t, mkT_t, mv_t, g_t, gT_t = t_arrs
    nj = seq // rb
    nch = rb // CHUNK
    blk = (lambda b, j: b * nj + (nj - 1 - j)) if reverse else (lambda b, j: b * nj + j)
    in_specs = [
        pl.BlockSpec((rb, H_B * DH_B), lambda b, j: (blk(b, j), 0)),
        pl.BlockSpec((nch, H_B * DH_B, CHUNK), lambda b, j: (blk(b, j), 0, 0)),
        pl.BlockSpec((rb, H_B * DH_B), lambda b, j: (blk(b, j), 0)),
        pl.BlockSpec((rb, LANES), lambda b, j: (blk(b, j), 0)),
        pl.BlockSpec((nch, 4 * H_B, CHUNK), lambda b, j: (blk(b, j), 0, 0)),
        pl.BlockSpec((TAIL, H_B * DH_B), lambda b, j: (b, 0)),
        pl.BlockSpec((1, H_B * DH_B, CHUNK), lambda b, j: (b, 0, 0)),
        pl.BlockSpec((TAIL, H_B * DH_B), lambda b, j: (b, 0)),
        pl.BlockSpec((TAIL, LANES), lambda b, j: (b, 0)),
        pl.BlockSpec((1, 4 * H_B, CHUNK), lambda b, j: (b, 0, 0)),
    ]
    out_specs = (pl.BlockSpec((rb, H_B * DH_B), lambda b, j: (blk(b, j), 0)),
                 pl.BlockSpec((TAIL, H_B * DH_B), lambda b, j: (b, 0)))
    out_shape = (jax.ShapeDtypeStruct((batch * seq, H_B * DH_B), f32),
                 jax.ShapeDtypeStruct((batch * TAIL, H_B * DH_B), f32))
    return pl.pallas_call(
        functools.partial(_mlstm_kernel, reverse=reverse),
        grid=(batch, nj), in_specs=in_specs, out_specs=out_specs, out_shape=out_shape,
        scratch_shapes=[pltpu.VMEM((H_B, DH_B, 2 * DH_B), f32), pltpu.VMEM((H_B, SUBLANES, LANES), f32)],
        compiler_params=_params(("parallel", "arbitrary")), name="mlstm_bwd" if reverse else "mlstm_fwd",
    )(mq, mkT, mv, g, gT, mq_t, mkT_t, mv_t, g_t, gT_t)


def _merge_kernel(attn_ref, hf_ref, hb_ref, so_ref, sga_ref, sgb_ref, res_ref, mn_ref, woa_ref, wom_ref, wout_ref,
                  o_ref):
    hm = hf_ref[...] + hb_ref[...]
    mn = mn_ref[...]
    parts = [_rms(hm[:, h * DH_B:(h + 1) * DH_B], mn[:, h * DH_B:(h + 1) * DH_B]) for h in range(H_B)]
    hmn = (jnp.concatenate(parts, axis=1) * so_ref[...].astype(f32)).astype(bf16)
    y = (sga_ref[...].astype(f32) * _dot(attn_ref[...], woa_ref[...])
         + sgb_ref[...].astype(f32) * _dot(hmn, wom_ref[...]))
    o_ref[...] = res_ref[...] + _dot(y.astype(bf16), wout_ref[...])


def _merge(attn, hf, hb, so, sga, sgb, res, w, *, tm):
    rows = attn.shape[0]
    row_spec = lambda width: pl.BlockSpec((tm, width), lambda i: (i, 0))
    in_specs = [row_spec(H_A * V_DIM), row_spec(H_B * DH_B), row_spec(H_B * DH_B), row_spec(H_B * DH_B),
                row_spec(D_MODEL), row_spec(D_MODEL), row_spec(D_MODEL),
                _const_spec((1, H_B * DH_B)), _const_spec(w["woa"].shape), _const_spec(w["wom"].shape),
                _const_spec(w["wout"].shape)]
    return pl.pallas_call(
        _merge_kernel, grid=(rows // tm,), in_specs=in_specs, out_specs=row_spec(D_MODEL),
        out_shape=jax.ShapeDtypeStruct((rows, D_MODEL), f32),
        compiler_params=_params(("parallel",)), name="merge",
    )(attn, hf, hb, so, sga, sgb, res, w["mlstm_norm"], w["woa"], w["wom"], w["wout"])


def _gelu_tanh(x):
    return 0.5 * x * (1.0 + jnp.tanh(math.sqrt(2.0 / math.pi) * (x + 0.044715 * (x * x * x))))


def _ffn_kernel(h_ref, prev_ref, prevt_ref, next_ref, nf_ref, wup_ref, wgate_ref, cw_ref, cb_ref, wdown_ref,
                nfin_ref, o_ref, *, tiles_per_batch):
    i = pl.program_id(0)
    tm = h_ref.shape[0]
    first = (i % tiles_per_batch) == 0
    last = (i % tiles_per_batch) == tiles_per_batch - 1
    nf = nf_ref[...]
    h = h_ref[...]
    u = _rms(h, nf).astype(bf16)
    prev = jnp.where(first, prevt_ref[...], prev_ref[...])
    u_prev = _rms(prev, nf).astype(bf16)
    u_next = _rms(next_ref[...], nf).astype(bf16)
    row = lax.broadcasted_iota(jnp.int32, (tm, 1), 0)
    acc = h
    for lo, hi in FF_CHUNKS:
        wup = wup_ref[:, lo:hi]
        a = _dot(u, wup)
        a_prev = _dot(u_prev, wup)[SUBLANES - 1:SUBLANES, :]
        a_next = jnp.where(last, 0.0, _dot(u_next, wup)[0:1, :])
        a_dn = jnp.where(row == 0, a_prev, pltpu.roll(a, 1, axis=0))
        a_up = jnp.where(row == tm - 1, a_next, pltpu.roll(a, tm - 1, axis=0))
        cw = cw_ref[:, lo:hi]
        conv = a_dn * cw[0:1, :] + a * cw[1:2, :] + a_up * cw[2:3, :] + cb_ref[:, lo:hi]
        gated = (_gelu_tanh(conv) * _dot(u, wgate_ref[:, lo:hi])).astype(bf16)
        acc = acc + _dot(gated, wdown_ref[lo:hi, :])
    o_ref[...] = _rms(acc, nfin_ref[...])


def _ffn(h1, h1_tail, w, *, seq, tm):
    rows = h1.shape[0]
    n = rows // tm
    tpb = seq // tm
    hb = tm // SUBLANES
    in_specs = [
        pl.BlockSpec((tm, D_MODEL), lambda i: (i, 0)),
        pl.BlockSpec((SUBLANES, D_MODEL), lambda i: (jnp.maximum(i * hb - 1, 0), 0)),
        pl.BlockSpec((SUBLANES, D_MODEL), lambda i: ((i // tpb) * (TAIL // SUBLANES) + N_META // SUBLANES - 1, 0)),
        pl.BlockSpec((SUBLANES, D_MODEL), lambda i: (jnp.minimum((i + 1) * hb, n * hb - 1), 0)),
        _const_spec((1, D_MODEL)), _const_spec(w["wup"].shape), _const_spec(w["wgate"].shape),
        _const_spec((SUBLANES, D_FF)), _const_spec((1, D_FF)), _const_spec(w["wdown"].shape),
        _const_spec((1, D_MODEL)),
    ]
    return pl.pallas_call(
        functools.partial(_ffn_kernel, tiles_per_batch=tpb),
        grid=(n,), in_specs=in_specs, out_specs=pl.BlockSpec((tm, D_MODEL), lambda i: (i, 0)),
        out_shape=jax.ShapeDtypeStruct((rows, D_MODEL), f32),
        compiler_params=_params(("parallel",)), name="conv_ffn",
    )(h1, h1, h1_tail, h1, w["norm_ffn"], w["wup"], w["wgate"], w["conv_w"], w["conv_b"], w["wdown"],
      w["norm_final"])


def _prep_weights(norm_mix, w_in, q_norm, w_uq, kv_norm, w_ukv, b_igate, b_fgate, mlstm_norm, w_o_attn, w_o_mlstm,
                  w_out, norm_ffn, w_up, w_gate, conv_w, conv_b, w_down, norm_final):
    w = w_in[0]
    offs = [0]
    for s in IN_SIZES:
        offs.append(offs[-1] + s)
    col = lambda j: w[:, offs[j]:offs[j + 1]]
    zeros = lambda r, c: jnp.zeros((r, c), f32)
    w_kr = col(2)
    half = ROPE // 2
    kr_main = jnp.concatenate([zeros(D_MODEL, NOPE), w_kr, zeros(D_MODEL, HEAD_PAD - QK_DIM)], axis=1)
    kr_rot = jnp.concatenate([zeros(D_MODEL, NOPE), -w_kr[:, half:], w_kr[:, :half],
                              zeros(D_MODEL, HEAD_PAD - QK_DIM)], axis=1)
    wa = jnp.concatenate([col(0), col(1), kr_main, kr_rot], axis=1)
    wb = jnp.concatenate([col(3), col(4), col(5), col(6), col(7), col(8), zeros(D_MODEL, LANES - 4 * H_B)], axis=1)
    wc = jnp.concatenate([col(9), col(10)], axis=1)
    wuq = w_uq[0].reshape(Q_LORA, H_A, QK_DIM)
    nope, rope = wuq[:, :, :NOPE], wuq[:, :, NOPE:]
    zq = lambda c: jnp.zeros((Q_LORA, H_A, c), f32)
    q_main = jnp.concatenate([nope, rope, zq(HEAD_PAD - QK_DIM)], axis=-1).reshape(Q_LORA, H_A * HEAD_PAD)
    q_rot = jnp.concatenate([zq(NOPE), -rope[:, :, half:], rope[:, :, :half], zq(HEAD_PAD - QK_DIM)],
                            axis=-1).reshape(Q_LORA, H_A * HEAD_PAD)
    wq = jnp.concatenate([q_main, q_rot], axis=1)
    wukv = w_ukv[0].reshape(KV_LORA, H_A, NOPE + V_DIM)
    wk = jnp.concatenate([wukv[:, :, :NOPE], jnp.zeros((KV_LORA, H_A, HEAD_PAD - NOPE), f32)],
                         axis=-1).reshape(KV_LORA, H_A * HEAD_PAD)
    wv = wukv[:, :, NOPE:].reshape(KV_LORA, H_A * V_DIM)
    wkv = jnp.concatenate([wk, wv], axis=1)
    gbias = jnp.concatenate([b_igate[0].reshape(-1), b_fgate[0].reshape(-1), jnp.zeros((LANES - 4 * H_B,), f32)])
    cw = jnp.concatenate([conv_w[0], jnp.zeros((SUBLANES - CONV_W, D_FF), f32)], axis=0)
    return dict(
        norm_mix=norm_mix[0][None], wa=wa.astype(bf16), wb=wb.astype(bf16), wc=wc.astype(bf16),
        q_norm=q_norm[0][None], wq=wq.astype(bf16), kv_norm=kv_norm[0][None], wkv=wkv.astype(bf16),
        gbias=gbias[None], mlstm_norm=mlstm_norm[0][None], woa=w_o_attn[0].astype(bf16),
        wom=w_o_mlstm[0].astype(bf16), wout=w_out[0].astype(bf16), norm_ffn=norm_ffn[0][None],
        wup=w_up[0].astype(bf16), wgate=w_gate[0].astype(bf16), conv_w=cw, conv_b=conv_b[0][None],
        wdown=w_down[0].astype(bf16), norm_final=norm_final[None])


def _rope_tables(positions):
    inv = ROPE_THETA ** (-jnp.arange(0, ROPE, 2, dtype=f32) / ROPE)
    ang = positions.astype(f32)[:, None] * inv[None, :]
    n = positions.shape[0]
    cos, sin = jnp.cos(ang), jnp.sin(ang)
    cm = jnp.concatenate([jnp.ones((n, NOPE), f32), cos, cos, jnp.zeros((n, HEAD_PAD - QK_DIM), f32)], axis=1)
    sm = jnp.concatenate([jnp.zeros((n, NOPE), f32), sin, sin, jnp.zeros((n, HEAD_PAD - QK_DIM), f32)], axis=1)
    return cm, sm


def _tiles(batch, seq):
    tm = math.gcd(seq, 512)
    tm_tail = math.gcd(batch * TAIL, 512)
    tkb = math.gcd(seq, 2048)
    tqb = math.gcd(seq, 1024)
    return dict(tm=tm, tm_tail=tm_tail, tqb=tqb, tkb=tkb, rb=tm)


def _encode(x, meta_tokens, w):
    batch, seq, _ = x.shape
    t = _tiles(batch, seq)
    xf = x.reshape(batch * seq, D_MODEL)
    tail_one = jnp.concatenate([meta_tokens.astype(f32), jnp.zeros((TAIL - N_META, D_MODEL), f32)], axis=0)
    tailf = jnp.tile(tail_one, (batch, 1))

    cm_x, sm_x = _rope_tables(N_META + jnp.arange(seq))
    pos_t = jnp.tile(jnp.arange(TAIL), t["tm_tail"] // TAIL)
    cm_t, sm_t = _rope_tables(pos_t)
    nper = seq // t["tm"]
    xs = _inproj(xf, cm_x, sm_x, lambda i: (i % nper, 0), w, tm=t["tm"], cq=TQ, ck=KC, mask_tail=False)
    ts = _inproj(tailf, cm_t, sm_t, lambda i: (0, 0), w, tm=t["tm_tail"], cq=TAIL, ck=TAIL, mask_tail=True)
    qT, k, vT, mq, mkT, mv, so, g, gT, sga, sgb = xs
    qT_t, k_t, vT_t, mq_t, mkT_t, mv_t, so_t, g_t, gT_t, sga_t, sgb_t = ts

    attn = _attention(qT, k, vT, k_t, vT_t, batch=batch, seq=seq, tqb=t["tqb"], tkb=t["tkb"], q_is_tail=False)
    attn_t = _attention(qT_t, k, vT, k_t, vT_t, batch=batch, seq=seq, tqb=TAIL, tkb=t["tkb"], q_is_tail=True)

    x_arrs = (mq, mkT, mv, g, gT)
    t_arrs = (mq_t, mkT_t, mv_t, g_t, gT_t)
    hf, hf_t = _mlstm(x_arrs, t_arrs, batch=batch, seq=seq, rb=t["rb"], reverse=False)
    hb, hb_t = _mlstm(x_arrs, t_arrs, batch=batch, seq=seq, rb=t["rb"], reverse=True)

    h1 = _merge(attn, hf, hb, so, sga, sgb, xf, w, tm=t["tm"])
    h1_t = _merge(attn_t, hf_t, hb_t, so_t, sga_t, sgb_t, tailf, w, tm=t["tm_tail"])
    y = _ffn(h1, h1_t, w, seq=seq, tm=t["tm"])
    return y.reshape(batch, seq, D_MODEL)


def kernel(x_prompt, x_sample, meta_tokens, norm_mix, w_in, q_norm, w_uq, kv_norm, w_ukv, b_igate, b_fgate,
           mlstm_norm, w_o_attn, w_o_mlstm, w_out, norm_ffn, w_up, w_gate, conv_w, conv_b, w_down, norm_final):
    w = _prep_weights(norm_mix, w_in, q_norm, w_uq, kv_norm, w_ukv, b_igate, b_fgate, mlstm_norm, w_o_attn,
                      w_o_mlstm, w_out, norm_ffn, w_up, w_gate, conv_w, conv_b, w_down, norm_final)
    return (_encode(x_prompt, meta_tokens, w), _encode(x_sample, meta_tokens, w))
```

```python
import functools
import math

import jax
import jax.numpy as jnp
from jax import lax
from jax.experimental import pallas as pl
from jax.experimental.pallas import tpu as pltpu

D_MODEL = 1024
N_META = 16
H_A = 8
Q_LORA = 256
KV_LORA = 128
NOPE = 64
ROPE = 32
QK_DIM = NOPE + ROPE
V_DIM = 64
ROPE_THETA = 10000.0
H_B = 4
DH_B = 128
D_FF = 2816
CONV_W = 3
EPS = 1e-6
NEG = -1e30
IN_SIZES = (Q_LORA, KV_LORA, ROPE, H_B * DH_B, H_B * DH_B, H_B * DH_B, H_B * DH_B, 2 * H_B, 2 * H_B, D_MODEL, D_MODEL)

LANES = 128
SUBLANES = 8
TAIL = 128
HEAD_PAD = LANES
CHUNK = 128
TQ = 256
KC = 512
ATTN_UNROLL = 4
VMEM_LIMIT = 56 * 1024 * 1024
QSCALE = (QK_DIM ** -0.5) * math.log2(math.e)
FF_CHUNKS = tuple((lo, min(lo + 256, D_FF)) for lo in range(0, D_FF, 256))
FF_LOOKAHEAD = 2

f32 = jnp.float32
bf16 = jnp.bfloat16


def _rms(x, g):
    return x * lax.rsqrt(jnp.mean(x * x, axis=-1, keepdims=True) + EPS) * g


def _sigmoid(x):
    return 1.0 / (1.0 + jnp.exp(-x))


def _log_sigmoid(x):
    return jnp.minimum(x, 0.0) - jnp.log1p(jnp.exp(-jnp.abs(x)))


def _dot(a, b):
    return jnp.dot(a, b, preferred_element_type=f32)


def _const_spec(shape):
    zeros = (0,) * len(shape)
    return pl.BlockSpec(shape, lambda *_: zeros, pipeline_mode=pl.Buffered(1))


def _params(sem):
    return pltpu.CompilerParams(dimension_semantics=sem, vmem_limit_bytes=VMEM_LIMIT)


def _inproj_kernel(h_ref, cm_ref, sm_ref, nmix_ref, wa_ref, wb_ref, wc_ref, qn_ref, wq_ref, kvn_ref, wkv_ref,
                   gb_ref, qT_ref, k_ref, vT_ref, mq_ref, mkT_ref, mv_ref, so_ref, g_ref, gT_ref, sga_ref, sgb_ref,
                   *, cq, ck, mask_tail):
    tm = h_ref.shape[0]
    u = _rms(h_ref[...], nmix_ref[...]).astype(bf16)
    cm = cm_ref[...]
    sm = sm_ref[...]

    za = _dot(u, wa_ref[...])
    qn = _rms(za[:, :Q_LORA], qn_ref[...]).astype(bf16)
    kvn = _rms(za[:, Q_LORA:Q_LORA + KV_LORA], kvn_ref[...]).astype(bf16)
    kr = za[:, 384:512] * cm + za[:, 512:640] * sm
    cm8 = jnp.concatenate([cm] * H_A, axis=1)
    sm8 = jnp.concatenate([sm] * H_A, axis=1)
    q = (_dot(qn, wq_ref[:, :H_A * HEAD_PAD]) * cm8 + _dot(qn, wq_ref[:, H_A * HEAD_PAD:]) * sm8) * QSCALE
    qT = q.T
    for c in range(tm // cq):
        qT_ref[c] = qT[:, c * cq:(c + 1) * cq].astype(bf16)
    kv = _dot(kvn, wkv_ref[...])
    for h in range(H_A):
        kh = (kv[:, h * HEAD_PAD:(h + 1) * HEAD_PAD] + kr).astype(bf16)
        for c in range(tm // ck):
            k_ref[c, h] = kh[c * ck:(c + 1) * ck, :]
    vT = kv[:, H_A * HEAD_PAD:].T
    for c in range(tm // ck):
        vT_ref[c] = vT[:, c * ck:(c + 1) * ck].astype(bf16)

    zb = _dot(u, wb_ref[...])
    mq_ref[...] = zb[:, 0:512].astype(bf16)
    mkT = (zb[:, 512:1024] * (DH_B ** -0.5)).T
    for c in range(tm // CHUNK):
        mkT_ref[c] = mkT[:, c * CHUNK:(c + 1) * CHUNK].astype(bf16)
    mv_ref[...] = zb[:, 1024:1536].astype(bf16)
    so_ref[...] = _sigmoid(zb[:, 1536:2048]).astype(bf16)
    gz = zb[:, 2048:2176] + gb_ref[...]
    lane = lax.broadcasted_iota(jnp.int32, gz.shape, 1)
    gates = jnp.where(lane < 2 * H_B, gz, _log_sigmoid(gz))
    if mask_tail:
        row = lax.broadcasted_iota(jnp.int32, gz.shape, 0)
        pad = (row % TAIL) >= N_META
        gates = jnp.where(pad, jnp.where(lane < 2 * H_B, NEG, 0.0), gates)
    g_ref[...] = gates
    gT = gates.T
    for c in range(tm // CHUNK):
        gT_ref[c] = gT[:4 * H_B, c * CHUNK:(c + 1) * CHUNK]

    zc = _dot(u, wc_ref[...])
    sga_ref[...] = _sigmoid(zc[:, :D_MODEL]).astype(bf16)
    sgb_ref[...] = _sigmoid(zc[:, D_MODEL:]).astype(bf16)


def _inproj(h, cm, sm, tab_map, w, *, tm, cq, ck, mask_tail):
    rows = h.shape[0]
    n = rows // tm
    row_spec = lambda width: pl.BlockSpec((tm, width), lambda i: (i, 0))
    chunk_spec = lambda height, cw: pl.BlockSpec((tm // cw, height, cw), lambda i: (i, 0, 0))
    out_shape = (
        jax.ShapeDtypeStruct((rows // cq, H_A * HEAD_PAD, cq), bf16),
        jax.ShapeDtypeStruct((rows // ck, H_A, ck, HEAD_PAD), bf16),
        jax.ShapeDtypeStruct((rows // ck, H_A * V_DIM, ck), bf16),
        jax.ShapeDtypeStruct((rows, H_B * DH_B), bf16),
        jax.ShapeDtypeStruct((rows // CHUNK, H_B * DH_B, CHUNK), bf16),
        jax.ShapeDtypeStruct((rows, H_B * DH_B), bf16),
        jax.ShapeDtypeStruct((rows, H_B * DH_B), bf16),
        jax.ShapeDtypeStruct((rows, LANES), f32),
        jax.ShapeDtypeStruct((rows // CHUNK, 4 * H_B, CHUNK), f32),
        jax.ShapeDtypeStruct((rows, D_MODEL), bf16),
        jax.ShapeDtypeStruct((rows, D_MODEL), bf16),
    )
    out_specs = (
        chunk_spec(H_A * HEAD_PAD, cq),
        pl.BlockSpec((tm // ck, H_A, ck, HEAD_PAD), lambda i: (i, 0, 0, 0)), chunk_spec(H_A * V_DIM, ck),
        row_spec(H_B * DH_B), chunk_spec(H_B * DH_B, CHUNK), row_spec(H_B * DH_B), row_spec(H_B * DH_B),
        row_spec(LANES), chunk_spec(4 * H_B, CHUNK), row_spec(D_MODEL), row_spec(D_MODEL),
    )
    in_specs = [
        row_spec(D_MODEL),
        pl.BlockSpec((tm, LANES), tab_map), pl.BlockSpec((tm, LANES), tab_map),
        _const_spec((1, D_MODEL)), _const_spec(w["wa"].shape), _const_spec(w["wb"].shape),
        _const_spec(w["wc"].shape), _const_spec((1, Q_LORA)), _const_spec(w["wq"].shape),
        _const_spec((1, KV_LORA)), _const_spec(w["wkv"].shape), _const_spec((1, LANES)),
    ]
    return pl.pallas_call(
        functools.partial(_inproj_kernel, cq=cq, ck=ck, mask_tail=mask_tail),
        grid=(n,), in_specs=in_specs, out_specs=out_specs, out_shape=out_shape,
        compiler_params=_params(("parallel",)), name="inproj",
    )(h, cm, sm, w["norm_mix"], w["wa"], w["wb"], w["wc"], w["q_norm"], w["wq"], w["kv_norm"], w["wkv"], w["gbias"])


def _attn_scores(qT_ref, k_ref, t, s_ref):
    nkc = k_ref.shape[0]
    if isinstance(t, int):
        qs, c = t // nkc, t % nkc
    else:
        qs = lax.div(t, nkc)
        c = t - qs * nkc

    def head(h):
        s_ref[h] = _dot(k_ref[c, h], qT_ref[qs, h * HEAD_PAD:(h + 1) * HEAD_PAD, :])
    return head


def _attn_update(vT_ref, t, s_ref, m_sc, l_sc, acc_sc, scores_head):
    nkc = vT_ref.shape[0]
    if isinstance(t, int):
        qs, c = t // nkc, t % nkc
    else:
        qs = lax.div(t, nkc)
        c = t - qs * nkc
    m_all = m_sc[qs]
    l_all = l_sc[qs]
    kc = s_ref.shape[1]
    half = kc // 2
    ones = jnp.ones((2 * SUBLANES, kc), bf16)
    ms, ls = [], []

    def value_product(h, p, alpha):
        v_aug = jnp.concatenate([vT_ref[c, h * V_DIM:(h + 1) * V_DIM, :], ones], axis=0)
        pv = _dot(v_aug, p)
        ls.append(alpha * l_all[h:h + 1, :] + pv[V_DIM:V_DIM + 1, :])
        acc_sc[qs, h * V_DIM:(h + 1) * V_DIM, :] = (alpha * acc_sc[qs, h * V_DIM:(h + 1) * V_DIM, :]
                                                    + pv[0:V_DIM, :])

    pending = None
    for h in range(H_A):
        if scores_head is not None:
            scores_head(h)
        if pending is not None:
            value_product(*pending)
        m_old = m_all[h:h + 1, :]
        m_new = jnp.maximum(m_old, jnp.max(s_ref[h], axis=0, keepdims=True))
        p = jnp.concatenate([jnp.exp2((s_ref[h, 0:half, :] - m_new).astype(bf16)),
                             jnp.exp2((s_ref[h, half:kc, :] - m_new).astype(bf16))], axis=0)
        ms.append(m_new)
        pending = (h, p, jnp.exp2(m_old - m_new))
    value_product(*pending)
    m_sc[qs] = jnp.concatenate(ms, axis=0)
    l_sc[qs] = jnp.concatenate(ls, axis=0)


def _attn_kernel(qT_ref, k_ref, vT_ref, kt_ref, vtt_ref, o_ref, m_sc, l_sc, acc_sc, s0_sc, s1_sc, *, single_kv_block):
    ki = pl.program_id(2)
    nqs, _, tq = qT_ref.shape
    nkc = vT_ref.shape[0]
    n = nqs * nkc

    def init():
        zk = jnp.zeros((N_META, HEAD_PAD), bf16)
        k_bd = jnp.concatenate(
            [jnp.concatenate([kt_ref[0, h, 0:N_META, :] if j == h else zk for j in range(H_A)], axis=1)
             for h in range(H_A)], axis=0)
        rr = lax.broadcasted_iota(jnp.int32, (TAIL, H_A * N_META), 0)
        rc = lax.broadcasted_iota(jnp.int32, (TAIL, H_A * N_META), 1)
        rep = jnp.where((rr < N_META) & (rc % N_META == rr), 1.0, 0.0).astype(bf16)
        hv = lax.broadcasted_iota(jnp.int32, (H_A * V_DIM, H_A * N_META), 0) // V_DIM
        cv = lax.broadcasted_iota(jnp.int32, (H_A * V_DIM, H_A * N_META), 1) // N_META
        v_bd = jnp.where(hv == cv, _dot(vtt_ref[0], rep), 0.0).astype(bf16)
        for qs in range(nqs):
            s = _dot(k_bd, qT_ref[qs])
            ms, ls, ps = [], [], []
            for h in range(H_A):
                sh = s[h * N_META:(h + 1) * N_META, :]
                m = jnp.max(sh, axis=0, keepdims=True)
                p = jnp.exp2(sh - m)
                ms.append(m)
                ls.append(jnp.sum(p, axis=0, keepdims=True))
                ps.append(p)
            m_sc[qs] = jnp.concatenate(ms, axis=0)
            l_sc[qs] = jnp.concatenate(ls, axis=0)
            acc_sc[qs] = _dot(v_bd, jnp.concatenate(ps, axis=0).astype(bf16))

    def finalize():
        for qs in range(nqs):
            inv = 1.0 / l_sc[qs]
            outs = [acc_sc[qs, h * V_DIM:(h + 1) * V_DIM, :] * inv[h:h + 1, :] for h in range(H_A)]
            o = jnp.concatenate(outs, axis=0)
            o_ref[qs * tq:(qs + 1) * tq, :] = o.T.astype(bf16)

    if single_kv_block:
        init()
    else:
        pl.when(ki == 0)(init)

    first = _attn_scores(qT_ref, k_ref, 0, s0_sc)
    for h in range(H_A):
        first(h)

    bufs = (s0_sc, s1_sc)

    def group(t, last):
        for u in range(ATTN_UNROLL):
            nxt = None if (last and u == ATTN_UNROLL - 1) else _attn_scores(qT_ref, k_ref, t + u + 1, bufs[(u + 1) % 2])
            _attn_update(vT_ref, t + u, bufs[u % 2], m_sc, l_sc, acc_sc, nxt)

    def body(i, carry):
        group(ATTN_UNROLL * i, False)
        return carry
    lax.fori_loop(0, n // ATTN_UNROLL - 1, body, 0)
    group(n - ATTN_UNROLL, True)

    if single_kv_block:
        finalize()
    else:
        pl.when(ki == pl.num_programs(2) - 1)(finalize)


def _attention(qT, k, vT, k_tail, vT_tail, *, batch, seq, tqb, tkb, q_is_tail):
    tq = qT.shape[2]
    kc = vT.shape[2]
    nqb = (TAIL if q_is_tail else seq) // tqb
    nkb = seq // tkb
    nqs = tqb // tq
    assert (nqs * (tkb // kc)) % ATTN_UNROLL == 0, "the kernel consumes steps in groups of ATTN_UNROLL"
    q_rows = qT.shape[0] * tq
    in_specs = [
        pl.BlockSpec((nqs, H_A * HEAD_PAD, tq), lambda b, qi, ki: (b * nqb + qi, 0, 0)),
        pl.BlockSpec((tkb // kc, H_A, kc, HEAD_PAD), lambda b, qi, ki: (b * nkb + ki, 0, 0, 0)),
        pl.BlockSpec((tkb // kc, H_A * V_DIM, kc), lambda b, qi, ki: (b * nkb + ki, 0, 0)),
        pl.BlockSpec((1, H_A, TAIL, HEAD_PAD), lambda b, qi, ki: (b, 0, 0, 0)),
        pl.BlockSpec((1, H_A * V_DIM, TAIL), lambda b, qi, ki: (b, 0, 0)),
    ]
    return pl.pallas_call(
        functools.partial(_attn_kernel, single_kv_block=(nkb == 1)),
        grid=(batch, nqb, nkb), in_specs=in_specs,
        out_specs=pl.BlockSpec((tqb, H_A * V_DIM), lambda b, qi, ki: (b * nqb + qi, 0)),
        out_shape=jax.ShapeDtypeStruct((q_rows, H_A * V_DIM), bf16),
        scratch_shapes=[pltpu.VMEM((nqs, H_A, tq), f32), pltpu.VMEM((nqs, H_A, tq), f32),
                        pltpu.VMEM((nqs, H_A * V_DIM, tq), f32),
                        pltpu.VMEM((H_A, kc, tq), f32), pltpu.VMEM((H_A, kc, tq), f32)],
        compiler_params=_params(("parallel", "parallel", "arbitrary")), name="attention",
    )(qT, k, vT, k_tail, vT_tail)


def _split3(x):
    a = x.astype(bf16)
    r = x - a.astype(f32)
    b = r.astype(bf16)
    c = (r - b.astype(f32)).astype(bf16)
    return a, b, c


def _mlstm_gates(g, gT, *, reverse):
    r = lax.broadcasted_iota(jnp.int32, (CHUNK, CHUNK), 0)
    c = lax.broadcasted_iota(jnp.int32, (CHUNK, CHUNK), 1)
    tri = jnp.where((c >= r) if reverse else (c <= r), 1.0, 0.0).astype(bf16)
    triT = jnp.where((r >= c) if reverse else (r <= c), 1.0, 0.0).astype(bf16)
    g3 = _split3(g)
    bcol_all = _dot(tri, g3[0]) + _dot(tri, g3[1]) + _dot(tri, g3[2])
    t3 = _split3(gT)
    brow_all = _dot(t3[0], triT) + _dot(t3[1], triT) + _dot(t3[2], triT)
    return bcol_all, brow_all


def _mlstm_head(h, qk, cums, q, kT, v, gT, c_sc, m_sc, *, reverse):
    d = 1 if reverse else 0
    bcol_all, brow_all = cums
    r = lax.broadcasted_iota(jnp.int32, (CHUNK, CHUNK), 0)
    c = lax.broadcasted_iota(jnp.int32, (CHUNK, CHUNK), 1)
    mask = (c >= r) if reverse else (c <= r)
    ones_blk = jnp.ones((CHUNK, LANES), bf16)
    tile_shape = (SUBLANES, LANES)
    ji = d * H_B + h
    jf = 2 * H_B + d * H_B + h
    bcb = jnp.broadcast_to(bcol_all[:, jf:jf + 1], (CHUNK, LANES))
    br = brow_all[jf:jf + 1, :]
    a_row = gT[ji:ji + 1, :] - br
    gtot = jnp.broadcast_to(br[:, 0:1] if reverse else br[:, CHUNK - 1:CHUNK], tile_shape)
    m_prev = m_sc[h]
    mpb = jnp.concatenate([m_prev] * (CHUNK // SUBLANES), axis=0)
    qh = q[:, h * DH_B:(h + 1) * DH_B]
    kTh = kT[h * DH_B:(h + 1) * DH_B, :]
    v_aug = jnp.concatenate([v[:, h * DH_B:(h + 1) * DH_B], ones_blk], axis=1)
    amask = jnp.where(mask, a_row, -jnp.inf)
    mt = jnp.maximum(jnp.broadcast_to(jnp.max(amask, axis=1, keepdims=True), (CHUNK, LANES)), mpb)
    s = (qk * jnp.exp(amask - mt)).astype(bf16)
    e_in = jnp.exp(mpb - mt)
    c_prev = c_sc[h]
    numden = _dot(s, v_aug) + jnp.concatenate([e_in, e_in], axis=1) * _dot(qh, c_prev.astype(bf16))
    den = numden[:, DH_B:]
    out = numden[:, :DH_B] * (1.0 / jnp.maximum(jnp.abs(den), jnp.exp(-(bcb + mt))))
    a_end = gtot[0:1, :] + a_row
    m_loc = jnp.broadcast_to(jnp.max(a_end, axis=1, keepdims=True), tile_shape)
    m_new = jnp.maximum(gtot + m_prev, m_loc)
    sp = jnp.exp(gtot + m_prev - m_new)
    sl = jnp.exp(m_loc - m_new)
    kw = (kTh.astype(f32) * jnp.exp(a_end - m_loc[0:1, :])).astype(bf16)
    rep = lambda x: jnp.concatenate([jnp.concatenate([x] * (DH_B // SUBLANES), axis=0)] * 2, axis=1)
    c_sc[h] = rep(sp) * c_prev + rep(sl) * _dot(kw, v_aug)
    m_sc[h] = m_new
    return out


def _mlstm_chunks(loaders, stores, c_sc, m_sc, *, reverse):
    qk_head = lambda q, kT, h: _dot(q[:, h * DH_B:(h + 1) * DH_B], kT[h * DH_B:(h + 1) * DH_B, :])
    cur = loaders[0]()
    cums = _mlstm_gates(cur[3], cur[4], reverse=reverse)
    qks = [qk_head(cur[0], cur[1], h) for h in range(H_B)]
    for i, store in enumerate(stores):
        q, kT, v, _, gT = cur
        nxt = loaders[i + 1]() if i + 1 < len(loaders) else None
        cums_n = _mlstm_gates(nxt[3], nxt[4], reverse=reverse) if nxt is not None else None
        qks_n, outs = [], []
        for h in range(H_B):
            if nxt is not None:
                qks_n.append(qk_head(nxt[0], nxt[1], h))
            outs.append(_mlstm_head(h, qks[h], cums, q, kT, v, gT, c_sc, m_sc, reverse=reverse))
        store(jnp.concatenate(outs, axis=1))
        cur, cums, qks = nxt, cums_n, qks_n


def _mlstm_kernel(q_ref, kT_ref, v_ref, g_ref, gT_ref, qt_ref, kTt_ref, vt_ref, gt_ref, gTt_ref,
                  o_ref, ot_ref, c_sc, m_sc, *, reverse):
    j = pl.program_id(1)
    nj = pl.num_programs(1)
    nch = kT_ref.shape[0]
    run = functools.partial(_mlstm_chunks, c_sc=c_sc, m_sc=m_sc, reverse=reverse)

    def tail_chunk():
        def store(o):
            ot_ref[...] = o
        run([lambda: (qt_ref[...], kTt_ref[0], vt_ref[...], gt_ref[...], gTt_ref[0])], [store])

    @pl.when(j == 0)
    def _():
        c_sc[...] = jnp.zeros(c_sc.shape, f32)
        m_sc[...] = jnp.full(m_sc.shape, NEG, f32)
        if not reverse:
            tail_chunk()

    def loader(c):
        rows = slice(c * CHUNK, (c + 1) * CHUNK)
        return lambda: (q_ref[rows, :], kT_ref[c], v_ref[rows, :], g_ref[rows, :], gT_ref[c])

    def storer(c):
        def store(o):
            o_ref[c * CHUNK:(c + 1) * CHUNK, :] = o
        return store

    order = list(range(nch - 1, -1, -1) if reverse else range(nch))
    run([loader(c) for c in order], [storer(c) for c in order])

    if reverse:
        @pl.when(j == nj - 1)
        def _():
            tail_chunk()


def _mlstm(x_arrs, t_arrs, *, batch, seq, rb, reverse):
    mq, mkT, mv, g, gT = x_arrs
    mq_t, mkT_t, mv_t, g_t, gT_t = t_arrs
    nj = seq // rb
    nch = rb // CHUNK
    blk = (lambda b, j: b * nj + (nj - 1 - j)) if reverse else (lambda b, j: b * nj + j)
    in_specs = [
        pl.BlockSpec((rb, H_B * DH_B), lambda b, j: (blk(b, j), 0)),
        pl.BlockSpec((nch, H_B * DH_B, CHUNK), lambda b, j: (blk(b, j), 0, 0)),
        pl.BlockSpec((rb, H_B * DH_B), lambda b, j: (blk(b, j), 0)),
        pl.BlockSpec((rb, LANES), lambda b, j: (blk(b, j), 0)),
        pl.BlockSpec((nch, 4 * H_B, CHUNK), lambda b, j: (blk(b, j), 0, 0)),
        pl.BlockSpec((TAIL, H_B * DH_B), lambda b, j: (b, 0)),
        pl.BlockSpec((1, H_B * DH_B, CHUNK), lambda b, j: (b, 0, 0)),
        pl.BlockSpec((TAIL, H_B * DH_B), lambda b, j: (b, 0)),
        pl.BlockSpec((TAIL, LANES), lambda b, j: (b, 0)),
        pl.BlockSpec((1, 4 * H_B, CHUNK), lambda b, j: (b, 0, 0)),
    ]
    out_specs = (pl.BlockSpec((rb, H_B * DH_B), lambda b, j: (blk(b, j), 0)),
                 pl.BlockSpec((TAIL, H_B * DH_B), lambda b, j: (b, 0)))
    out_shape = (jax.ShapeDtypeStruct((batch * seq, H_B * DH_B), f32),
                 jax.ShapeDtypeStruct((batch * TAIL, H_B * DH_B), f32))
    return pl.pallas_call(
        functools.partial(_mlstm_kernel, reverse=reverse),
        grid=(batch, nj), in_specs=in_specs, out_specs=out_specs, out_shape=out_shape,
        scratch_shapes=[pltpu.VMEM((H_B, DH_B, 2 * DH_B), f32), pltpu.VMEM((H_B, SUBLANES, LANES), f32)],
        compiler_params=_params(("parallel", "arbitrary")), name="mlstm_bwd" if reverse else "mlstm_fwd",
    )(mq, mkT, mv, g, gT, mq_t, mkT_t, mv_t, g_t, gT_t)


def _merge_kernel(attn_ref, hf_ref, hb_ref, so_ref, sga_ref, sgb_ref, res_ref, mn_ref, woa_ref, wom_ref, wout_ref,
                  o_ref):
    hm = hf_ref[...] + hb_ref[...]
    mn = mn_ref[...]
    parts = [_rms(hm[:, h * DH_B:(h + 1) * DH_B], mn[:, h * DH_B:(h + 1) * DH_B]) for h in range(H_B)]
    hmn = (jnp.concatenate(parts, axis=1) * so_ref[...].astype(f32)).astype(bf16)
    y = (sga_ref[...].astype(f32) * _dot(attn_ref[...], woa_ref[...])
         + sgb_ref[...].astype(f32) * _dot(hmn, wom_ref[...]))
    o_ref[...] = res_ref[...] + _dot(y.astype(bf16), wout_ref[...])


def _merge(attn, hf, hb, so, sga, sgb, res, w, *, tm):
    rows = attn.shape[0]
    row_spec = lambda width: pl.BlockSpec((tm, width), lambda i: (i, 0))
    in_specs = [row_spec(H_A * V_DIM), row_spec(H_B * DH_B), row_spec(H_B * DH_B), row_spec(H_B * DH_B),
                row_spec(D_MODEL), row_spec(D_MODEL), row_spec(D_MODEL),
                _const_spec((1, H_B * DH_B)), _const_spec(w["woa"].shape), _const_spec(w["wom"].shape),
                _const_spec(w["wout"].shape)]
    return pl.pallas_call(
        _merge_kernel, grid=(rows // tm,), in_specs=in_specs, out_specs=row_spec(D_MODEL),
        out_shape=jax.ShapeDtypeStruct((rows, D_MODEL), f32),
        compiler_params=_params(("parallel",)), name="merge",
    )(attn, hf, hb, so, sga, sgb, res, w["mlstm_norm"], w["woa"], w["wom"], w["wout"])


def _gelu_tanh(x):
    return 0.5 * x * (1.0 + jnp.tanh(math.sqrt(2.0 / math.pi) * (x + 0.044715 * (x * x * x))))


def _ffn_kernel(h_ref, prev_ref, prevt_ref, next_ref, nf_ref, wup_ref, wgate_ref, cw_ref, cb_ref, wdown_ref,
                nfin_ref, o_ref, *, tiles_per_batch):
    i = pl.program_id(0)
    tm = h_ref.shape[0]
    first = (i % tiles_per_batch) == 0
    last = (i % tiles_per_batch) == tiles_per_batch - 1
    nf = nf_ref[...]
    h = h_ref[...]
    prev = jnp.where(first, prevt_ref[...], prev_ref[...])
    u_ext = _rms(jnp.concatenate([prev, h, next_ref[...]], axis=0), nf)
    u = u_ext[SUBLANES:tm + SUBLANES, :].astype(bf16)
    u_ext = u_ext.astype(bf16)
    row = lax.broadcasted_iota(jnp.int32, (tm, 1), 0)

    def up_gate(lo, hi):
        return _dot(u_ext, wup_ref[:, lo:hi]), _dot(u, wgate_ref[:, lo:hi])

    acc = h
    ahead = [up_gate(*FF_CHUNKS[j]) for j in range(FF_LOOKAHEAD)]
    for ci, (lo, hi) in enumerate(FF_CHUNKS):
        if ci + FF_LOOKAHEAD < len(FF_CHUNKS):
            ahead.append(up_gate(*FF_CHUNKS[ci + FF_LOOKAHEAD]))
        a_ext, gate = ahead.pop(0)
        a = a_ext[SUBLANES:tm + SUBLANES, :]
        a_prev = a_ext[SUBLANES - 1:SUBLANES, :]
        a_next = jnp.where(last, 0.0, a_ext[tm + SUBLANES:tm + SUBLANES + 1, :])
        a_dn = jnp.where(row == 0, a_prev, pltpu.roll(a, 1, axis=0))
        a_up = jnp.where(row == tm - 1, a_next, pltpu.roll(a, tm - 1, axis=0))
        cw = cw_ref[:, lo:hi]
        conv = a_dn * cw[0:1, :] + a * cw[1:2, :] + a_up * cw[2:3, :] + cb_ref[:, lo:hi]
        gated = (_gelu_tanh(conv) * gate).astype(bf16)
        acc = acc + _dot(gated, wdown_ref[lo:hi, :])
    o_ref[...] = _rms(acc, nfin_ref[...])


def _ffn(h1, h1_tail, w, *, seq, tm):
    rows = h1.shape[0]
    n = rows // tm
    tpb = seq // tm
    hb = tm // SUBLANES
    in_specs = [
        pl.BlockSpec((tm, D_MODEL), lambda i: (i, 0)),
        pl.BlockSpec((SUBLANES, D_MODEL), lambda i: (jnp.maximum(i * hb - 1, 0), 0)),
        pl.BlockSpec((SUBLANES, D_MODEL), lambda i: ((i // tpb) * (TAIL // SUBLANES) + N_META // SUBLANES - 1, 0)),
        pl.BlockSpec((SUBLANES, D_MODEL), lambda i: (jnp.minimum((i + 1) * hb, n * hb - 1), 0)),
        _const_spec((1, D_MODEL)), _const_spec(w["wup"].shape), _const_spec(w["wgate"].shape),
        _const_spec((SUBLANES, D_FF)), _const_spec((1, D_FF)), _const_spec(w["wdown"].shape),
        _const_spec((1, D_MODEL)),
    ]
    return pl.pallas_call(
        functools.partial(_ffn_kernel, tiles_per_batch=tpb),
        grid=(n,), in_specs=in_specs, out_specs=pl.BlockSpec((tm, D_MODEL), lambda i: (i, 0)),
        out_shape=jax.ShapeDtypeStruct((rows, D_MODEL), f32),
        compiler_params=_params(("parallel",)), name="conv_ffn",
    )(h1, h1, h1_tail, h1, w["norm_ffn"], w["wup"], w["wgate"], w["conv_w"], w["conv_b"], w["wdown"],
      w["norm_final"])


def _prep_weights(norm_mix, w_in, q_norm, w_uq, kv_norm, w_ukv, b_igate, b_fgate, mlstm_norm, w_o_attn, w_o_mlstm,
                  w_out, norm_ffn, w_up, w_gate, conv_w, conv_b, w_down, norm_final):
    w = w_in[0]
    offs = [0]
    for s in IN_SIZES:
        offs.append(offs[-1] + s)
    col = lambda j: w[:, offs[j]:offs[j + 1]]
    zeros = lambda r, c: jnp.zeros((r, c), f32)
    w_kr = col(2)
    half = ROPE // 2
    kr_main = jnp.concatenate([zeros(D_MODEL, NOPE), w_kr, zeros(D_MODEL, HEAD_PAD - QK_DIM)], axis=1)
    kr_rot = jnp.concatenate([zeros(D_MODEL, NOPE), -w_kr[:, half:], w_kr[:, :half],
                              zeros(D_MODEL, HEAD_PAD - QK_DIM)], axis=1)
    wa = jnp.concatenate([col(0), col(1), kr_main, kr_rot], axis=1)
    wb = jnp.concatenate([col(3), col(4), col(5), col(6), col(7), col(8), zeros(D_MODEL, LANES - 4 * H_B)], axis=1)
    wc = jnp.concatenate([col(9), col(10)], axis=1)
    wuq = w_uq[0].reshape(Q_LORA, H_A, QK_DIM)
    nope, rope = wuq[:, :, :NOPE], wuq[:, :, NOPE:]
    zq = lambda c: jnp.zeros((Q_LORA, H_A, c), f32)
    q_main = jnp.concatenate([nope, rope, zq(HEAD_PAD - QK_DIM)], axis=-1).reshape(Q_LORA, H_A * HEAD_PAD)
    q_rot = jnp.concatenate([zq(NOPE), -rope[:, :, half:], rope[:, :, :half], zq(HEAD_PAD - QK_DIM)],
                            axis=-1).reshape(Q_LORA, H_A * HEAD_PAD)
    wq = jnp.concatenate([q_main, q_rot], axis=1)
    wukv = w_ukv[0].reshape(KV_LORA, H_A, NOPE + V_DIM)
    wk = jnp.concatenate([wukv[:, :, :NOPE], jnp.zeros((KV_LORA, H_A, HEAD_PAD - NOPE), f32)],
                         axis=-1).reshape(KV_LORA, H_A * HEAD_PAD)
    wv = wukv[:, :, NOPE:].reshape(KV_LORA, H_A * V_DIM)
    wkv = jnp.concatenate([wk, wv], axis=1)
    gbias = jnp.concatenate([b_igate[0].reshape(-1), b_fgate[0].reshape(-1), jnp.zeros((LANES - 4 * H_B,), f32)])
    cw = jnp.concatenate([conv_w[0], jnp.zeros((SUBLANES - CONV_W, D_FF), f32)], axis=0)
    return dict(
        norm_mix=norm_mix[0][None], wa=wa.astype(bf16), wb=wb.astype(bf16), wc=wc.astype(bf16),
        q_norm=q_norm[0][None], wq=wq.astype(bf16), kv_norm=kv_norm[0][None], wkv=wkv.astype(bf16),
        gbias=gbias[None], mlstm_norm=mlstm_norm[0][None], woa=w_o_attn[0].astype(bf16),
        wom=w_o_mlstm[0].astype(bf16), wout=w_out[0].astype(bf16), norm_ffn=norm_ffn[0][None],
        wup=w_up[0].astype(bf16), wgate=w_gate[0].astype(bf16), conv_w=cw, conv_b=conv_b[0][None],
        wdown=w_down[0].astype(bf16), norm_final=norm_final[None])


def _rope_tables(positions):
    inv = ROPE_THETA ** (-jnp.arange(0, ROPE, 2, dtype=f32) / ROPE)
    ang = positions.astype(f32)[:, None] * inv[None, :]
    n = positions.shape[0]
    cos, sin = jnp.cos(ang), jnp.sin(ang)
    cm = jnp.concatenate([jnp.ones((n, NOPE), f32), cos, cos, jnp.zeros((n, HEAD_PAD - QK_DIM), f32)], axis=1)
    sm = jnp.concatenate([jnp.zeros((n, NOPE), f32), sin, sin, jnp.zeros((n, HEAD_PAD - QK_DIM), f32)], axis=1)
    return cm, sm


def _tiles(batch, seq):
    tm = math.gcd(seq, 512)
    tm_tail = math.gcd(batch * TAIL, 512)
    tkb = math.gcd(seq, 2048)
    tqb = math.gcd(seq, 2048)
    return dict(tm=tm, tm_tail=tm_tail, tqb=tqb, tkb=tkb, rb=math.gcd(seq, 1024))


def _encode(x, meta_tokens, w):
    batch, seq, _ = x.shape
    t = _tiles(batch, seq)
    xf = x.reshape(batch * seq, D_MODEL)
    tail_one = jnp.concatenate([meta_tokens.astype(f32), jnp.zeros((TAIL - N_META, D_MODEL), f32)], axis=0)
    tailf = jnp.tile(tail_one, (batch, 1))

    cm_x, sm_x = _rope_tables(N_META + jnp.arange(seq))
    pos_t = jnp.tile(jnp.arange(TAIL), t["tm_tail"] // TAIL)
    cm_t, sm_t = _rope_tables(pos_t)
    nper = seq // t["tm"]
    xs = _inproj(xf, cm_x, sm_x, lambda i: (i % nper, 0), w, tm=t["tm"], cq=TQ, ck=KC, mask_tail=False)
    ts = _inproj(tailf, cm_t, sm_t, lambda i: (0, 0), w, tm=t["tm_tail"], cq=TAIL, ck=TAIL, mask_tail=True)
    qT, k, vT, mq, mkT, mv, so, g, gT, sga, sgb = xs
    qT_t, k_t, vT_t, mq_t, mkT_t, mv_t, so_t, g_t, gT_t, sga_t, sgb_t = ts

    attn = _attention(qT, k, vT, k_t, vT_t, batch=batch, seq=seq, tqb=t["tqb"], tkb=t["tkb"], q_is_tail=False)
    attn_t = _attention(qT_t, k, vT, k_t, vT_t, batch=batch, seq=seq, tqb=TAIL, tkb=t["tkb"], q_is_tail=True)

    x_arrs = (mq, mkT, mv, g, gT)
    t_arrs = (mq_t, mkT_t, mv_t, g_t, gT_t)
    hf, hf_t = _mlstm(x_arrs, t_arrs, batch=batch, seq=seq, rb=t["rb"], reverse=False)
    hb, hb_t = _mlstm(x_arrs, t_arrs, batch=batch, seq=seq, rb=t["rb"], reverse=True)

    h1 = _merge(attn, hf, hb, so, sga, sgb, xf, w, tm=t["tm"])
    h1_t = _merge(attn_t, hf_t, hb_t, so_t, sga_t, sgb_t, tailf, w, tm=t["tm_tail"])
    y = _ffn(h1, h1_t, w, seq=seq, tm=t["tm"])
    return y.reshape(batch, seq, D_MODEL)


def kernel(x_prompt, x_sample, meta_tokens, norm_mix, w_in, q_norm, w_uq, kv_norm, w_ukv, b_igate, b_fgate,
           mlstm_norm, w_o_attn, w_o_mlstm, w_out, norm_ffn, w_up, w_gate, conv_w, conv_b, w_down, norm_final):
    w = _prep_weights(norm_mix, w_in, q_norm, w_uq, kv_norm, w_ukv, b_igate, b_fgate, mlstm_norm, w_o_attn,
                      w_o_mlstm, w_out, norm_ffn, w_up, w_gate, conv_w, conv_b, w_down, norm_final)
    return (_encode(x_prompt, meta_tokens, w), _encode(x_sample, meta_tokens, w))
```

```python
import functools
import math

import jax
import jax.numpy as jnp
from jax import lax
from jax.experimental import pallas as pl
from jax.experimental.pallas import tpu as pltpu

D_MODEL = 1024
N_META = 16
H_A = 8
Q_LORA = 256
KV_LORA = 128
NOPE = 64
ROPE = 32
QK_DIM = NOPE + ROPE
V_DIM = 64
ROPE_THETA = 10000.0
H_B = 4
DH_B = 128
D_FF = 2816
CONV_W = 3
EPS = 1e-6
NEG = -1e30
IN_SIZES = (Q_LORA, KV_LORA, ROPE, H_B * DH_B, H_B * DH_B, H_B * DH_B, H_B * DH_B, 2 * H_B, 2 * H_B, D_MODEL, D_MODEL)

LANES = 128
SUBLANES = 8
TAIL = 128
HEAD_PAD = LANES
CHUNK = 128
TQ = 256
KC = 512
ATTN_UNROLL = 4
VMEM_LIMIT = 56 * 1024 * 1024
QSCALE = (QK_DIM ** -0.5) * math.log2(math.e)
FF_CHUNKS = tuple((lo, min(lo + 512, D_FF)) for lo in range(0, D_FF, 512))
FF_LOOKAHEAD = 2

f32 = jnp.float32
bf16 = jnp.bfloat16


def _rms(x, g):
    return x * lax.rsqrt(jnp.mean(x * x, axis=-1, keepdims=True) + EPS) * g


def _sigmoid(x):
    return 1.0 / (1.0 + jnp.exp(-x))


def _log_sigmoid(x):
    return jnp.minimum(x, 0.0) - jnp.log1p(jnp.exp(-jnp.abs(x)))


def _dot(a, b):
    return jnp.dot(a, b, preferred_element_type=f32)


def _const_spec(shape):
    zeros = (0,) * len(shape)
    return pl.BlockSpec(shape, lambda *_: zeros, pipeline_mode=pl.Buffered(1))


def _params(sem):
    return pltpu.CompilerParams(dimension_semantics=sem, vmem_limit_bytes=VMEM_LIMIT)


def _inproj_kernel(h_ref, cm_ref, sm_ref, nmix_ref, wa_ref, wb_ref, wc_ref, qn_ref, wq_ref, kvn_ref, wkv_ref,
                   gb_ref, qT_ref, k_ref, vT_ref, mq_ref, mkT_ref, mv_ref, so_ref, g_ref, gT_ref, sga_ref, sgb_ref,
                   *, cq, ck, mask_tail):
    tm = h_ref.shape[0]
    u = _rms(h_ref[...], nmix_ref[...]).astype(bf16)
    cm = cm_ref[...]
    sm = sm_ref[...]

    za = _dot(u, wa_ref[...])
    qn = _rms(za[:, :Q_LORA], qn_ref[...]).astype(bf16)
    kvn = _rms(za[:, Q_LORA:Q_LORA + KV_LORA], kvn_ref[...]).astype(bf16)
    kr = za[:, 384:512] * cm + za[:, 512:640] * sm
    cm8 = jnp.concatenate([cm] * H_A, axis=1)
    sm8 = jnp.concatenate([sm] * H_A, axis=1)
    q = (_dot(qn, wq_ref[:, :H_A * HEAD_PAD]) * cm8 + _dot(qn, wq_ref[:, H_A * HEAD_PAD:]) * sm8) * QSCALE
    qT = q.T
    for c in range(tm // cq):
        qT_ref[c] = qT[:, c * cq:(c + 1) * cq].astype(bf16)
    kv = _dot(kvn, wkv_ref[...])
    for h in range(H_A):
        kh = (kv[:, h * HEAD_PAD:(h + 1) * HEAD_PAD] + kr).astype(bf16)
        for c in range(tm // ck):
            k_ref[c, h] = kh[c * ck:(c + 1) * ck, :]
    vT = kv[:, H_A * HEAD_PAD:].T
    for c in range(tm // ck):
        vT_ref[c] = vT[:, c * ck:(c + 1) * ck].astype(bf16)

    zb = _dot(u, wb_ref[...])
    mq_ref[...] = zb[:, 0:512].astype(bf16)
    mkT = (zb[:, 512:1024] * (DH_B ** -0.5)).T
    for c in range(tm // CHUNK):
        mkT_ref[c] = mkT[:, c * CHUNK:(c + 1) * CHUNK].astype(bf16)
    mv_ref[...] = zb[:, 1024:1536].astype(bf16)
    so_ref[...] = _sigmoid(zb[:, 1536:2048]).astype(bf16)
    gz = zb[:, 2048:2176] + gb_ref[...]
    lane = lax.broadcasted_iota(jnp.int32, gz.shape, 1)
    gates = jnp.where(lane < 2 * H_B, gz, _log_sigmoid(gz))
    if mask_tail:
        row = lax.broadcasted_iota(jnp.int32, gz.shape, 0)
        pad = (row % TAIL) >= N_META
        gates = jnp.where(pad, jnp.where(lane < 2 * H_B, NEG, 0.0), gates)
    g_ref[...] = gates
    gT = gates.T
    for c in range(tm // CHUNK):
        gT_ref[c] = gT[:4 * H_B, c * CHUNK:(c + 1) * CHUNK]

    zc = _dot(u, wc_ref[...])
    sga_ref[...] = _sigmoid(zc[:, :D_MODEL]).astype(bf16)
    sgb_ref[...] = _sigmoid(zc[:, D_MODEL:]).astype(bf16)


def _inproj(h, cm, sm, tab_map, w, *, tm, cq, ck, mask_tail):
    rows = h.shape[0]
    n = rows // tm
    row_spec = lambda width: pl.BlockSpec((tm, width), lambda i: (i, 0))
    chunk_spec = lambda height, cw: pl.BlockSpec((tm // cw, height, cw), lambda i: (i, 0, 0))
    out_shape = (
        jax.ShapeDtypeStruct((rows // cq, H_A * HEAD_PAD, cq), bf16),
        jax.ShapeDtypeStruct((rows // ck, H_A, ck, HEAD_PAD), bf16),
        jax.ShapeDtypeStruct((rows // ck, H_A * V_DIM, ck), bf16),
        jax.ShapeDtypeStruct((rows, H_B * DH_B), bf16),
        jax.ShapeDtypeStruct((rows // CHUNK, H_B * DH_B, CHUNK), bf16),
        jax.ShapeDtypeStruct((rows, H_B * DH_B), bf16),
        jax.ShapeDtypeStruct((rows, H_B * DH_B), bf16),
        jax.ShapeDtypeStruct((rows, LANES), f32),
        jax.ShapeDtypeStruct((rows // CHUNK, 4 * H_B, CHUNK), f32),
        jax.ShapeDtypeStruct((rows, D_MODEL), bf16),
        jax.ShapeDtypeStruct((rows, D_MODEL), bf16),
    )
    out_specs = (
        chunk_spec(H_A * HEAD_PAD, cq),
        pl.BlockSpec((tm // ck, H_A, ck, HEAD_PAD), lambda i: (i, 0, 0, 0)), chunk_spec(H_A * V_DIM, ck),
        row_spec(H_B * DH_B), chunk_spec(H_B * DH_B, CHUNK), row_spec(H_B * DH_B), row_spec(H_B * DH_B),
        row_spec(LANES), chunk_spec(4 * H_B, CHUNK), row_spec(D_MODEL), row_spec(D_MODEL),
    )
    in_specs = [
        row_spec(D_MODEL),
        pl.BlockSpec((tm, LANES), tab_map), pl.BlockSpec((tm, LANES), tab_map),
        _const_spec((1, D_MODEL)), _const_spec(w["wa"].shape), _const_spec(w["wb"].shape),
        _const_spec(w["wc"].shape), _const_spec((1, Q_LORA)), _const_spec(w["wq"].shape),
        _const_spec((1, KV_LORA)), _const_spec(w["wkv"].shape), _const_spec((1, LANES)),
    ]
    return pl.pallas_call(
        functools.partial(_inproj_kernel, cq=cq, ck=ck, mask_tail=mask_tail),
        grid=(n,), in_specs=in_specs, out_specs=out_specs, out_shape=out_shape,
        compiler_params=_params(("parallel",)), name="inproj",
    )(h, cm, sm, w["norm_mix"], w["wa"], w["wb"], w["wc"], w["q_norm"], w["wq"], w["kv_norm"], w["wkv"], w["gbias"])


def _attn_scores(qT_ref, k_ref, t, s_ref):
    nkc = k_ref.shape[0]
    if isinstance(t, int):
        qs, c = t // nkc, t % nkc
    else:
        qs = lax.div(t, nkc)
        c = t - qs * nkc

    def head(h):
        s_ref[h] = _dot(k_ref[c, h], qT_ref[qs, h * HEAD_PAD:(h + 1) * HEAD_PAD, :])
    return head


def _attn_update(vT_ref, t, s_ref, m_sc, l_sc, acc_sc, scores_head):
    nkc = vT_ref.shape[0]
    if isinstance(t, int):
        qs, c = t // nkc, t % nkc
    else:
        qs = lax.div(t, nkc)
        c = t - qs * nkc
    m_all = m_sc[qs]
    l_all = l_sc[qs]
    kc = s_ref.shape[1]
    half = kc // 2
    ones = jnp.ones((2 * SUBLANES, kc), bf16)
    ms, ls = [], []

    def value_product(h, p, alpha):
        v_aug = jnp.concatenate([vT_ref[c, h * V_DIM:(h + 1) * V_DIM, :], ones], axis=0)
        pv = _dot(v_aug, p)
        ls.append(alpha * l_all[h:h + 1, :] + pv[V_DIM:V_DIM + 1, :])
        acc_sc[qs, h * V_DIM:(h + 1) * V_DIM, :] = (alpha * acc_sc[qs, h * V_DIM:(h + 1) * V_DIM, :]
                                                    + pv[0:V_DIM, :])

    pending = None
    for h in range(H_A):
        if scores_head is not None:
            scores_head(h)
        if pending is not None:
            value_product(*pending)
        m_old = m_all[h:h + 1, :]
        m_new = jnp.maximum(m_old, jnp.max(s_ref[h], axis=0, keepdims=True))
        p = jnp.concatenate([jnp.exp2((s_ref[h, 0:half, :] - m_new).astype(bf16)),
                             jnp.exp2((s_ref[h, half:kc, :] - m_new).astype(bf16))], axis=0)
        ms.append(m_new)
        pending = (h, p, jnp.exp2(m_old - m_new))
    value_product(*pending)
    m_sc[qs] = jnp.concatenate(ms, axis=0)
    l_sc[qs] = jnp.concatenate(ls, axis=0)


def _attn_kernel(qT_ref, k_ref, vT_ref, kt_ref, vtt_ref, o_ref, m_sc, l_sc, acc_sc, s0_sc, s1_sc, *, single_kv_block):
    ki = pl.program_id(2)
    nqs, _, tq = qT_ref.shape
    nkc = vT_ref.shape[0]
    n = nqs * nkc

    def init():
        zk = jnp.zeros((N_META, HEAD_PAD), bf16)
        k_bd = jnp.concatenate(
            [jnp.concatenate([kt_ref[0, h, 0:N_META, :] if j == h else zk for j in range(H_A)], axis=1)
             for h in range(H_A)], axis=0)
        rr = lax.broadcasted_iota(jnp.int32, (TAIL, H_A * N_META), 0)
        rc = lax.broadcasted_iota(jnp.int32, (TAIL, H_A * N_META), 1)
        rep = jnp.where((rr < N_META) & (rc % N_META == rr), 1.0, 0.0).astype(bf16)
        hv = lax.broadcasted_iota(jnp.int32, (H_A * V_DIM, H_A * N_META), 0) // V_DIM
        cv = lax.broadcasted_iota(jnp.int32, (H_A * V_DIM, H_A * N_META), 1) // N_META
        v_bd = jnp.where(hv == cv, _dot(vtt_ref[0], rep), 0.0).astype(bf16)
        for qs in range(nqs):
            s = _dot(k_bd, qT_ref[qs])
            ms, ls, ps = [], [], []
            for h in range(H_A):
                sh = s[h * N_META:(h + 1) * N_META, :]
                m = jnp.max(sh, axis=0, keepdims=True)
                p = jnp.exp2(sh - m)
                ms.append(m)
                ls.append(jnp.sum(p, axis=0, keepdims=True))
                ps.append(p)
            m_sc[qs] = jnp.concatenate(ms, axis=0)
            l_sc[qs] = jnp.concatenate(ls, axis=0)
            acc_sc[qs] = _dot(v_bd, jnp.concatenate(ps, axis=0).astype(bf16))

    def finalize():
        for qs in range(nqs):
            inv = 1.0 / l_sc[qs]
            outs = [acc_sc[qs, h * V_DIM:(h + 1) * V_DIM, :] * inv[h:h + 1, :] for h in range(H_A)]
            o = jnp.concatenate(outs, axis=0)
            o_ref[qs * tq:(qs + 1) * tq, :] = o.T.astype(bf16)

    if single_kv_block:
        init()
    else:
        pl.when(ki == 0)(init)

    first = _attn_scores(qT_ref, k_ref, 0, s0_sc)
    for h in range(H_A):
        first(h)

    bufs = (s0_sc, s1_sc)

    def group(t, last):
        for u in range(ATTN_UNROLL):
            nxt = None if (last and u == ATTN_UNROLL - 1) else _attn_scores(qT_ref, k_ref, t + u + 1, bufs[(u + 1) % 2])
            _attn_update(vT_ref, t + u, bufs[u % 2], m_sc, l_sc, acc_sc, nxt)

    def body(i, carry):
        group(ATTN_UNROLL * i, False)
        return carry
    lax.fori_loop(0, n // ATTN_UNROLL - 1, body, 0)
    group(n - ATTN_UNROLL, True)

    if single_kv_block:
        finalize()
    else:
        pl.when(ki == pl.num_programs(2) - 1)(finalize)


def _attention(qT, k, vT, k_tail, vT_tail, *, batch, seq, tqb, tkb, q_is_tail):
    tq = qT.shape[2]
    kc = vT.shape[2]
    nqb = (TAIL if q_is_tail else seq) // tqb
    nkb = seq // tkb
    nqs = tqb // tq
    assert (nqs * (tkb // kc)) % ATTN_UNROLL == 0, "the kernel consumes steps in groups of ATTN_UNROLL"
    q_rows = qT.shape[0] * tq
    in_specs = [
        pl.BlockSpec((nqs, H_A * HEAD_PAD, tq), lambda b, qi, ki: (b * nqb + qi, 0, 0)),
        pl.BlockSpec((tkb // kc, H_A, kc, HEAD_PAD), lambda b, qi, ki: (b * nkb + ki, 0, 0, 0)),
        pl.BlockSpec((tkb // kc, H_A * V_DIM, kc), lambda b, qi, ki: (b * nkb + ki, 0, 0)),
        pl.BlockSpec((1, H_A, TAIL, HEAD_PAD), lambda b, qi, ki: (b, 0, 0, 0)),
        pl.BlockSpec((1, H_A * V_DIM, TAIL), lambda b, qi, ki: (b, 0, 0)),
    ]
    return pl.pallas_call(
        functools.partial(_attn_kernel, single_kv_block=(nkb == 1)),
        grid=(batch, nqb, nkb), in_specs=in_specs,
        out_specs=pl.BlockSpec((tqb, H_A * V_DIM), lambda b, qi, ki: (b * nqb + qi, 0)),
        out_shape=jax.ShapeDtypeStruct((q_rows, H_A * V_DIM), bf16),
        scratch_shapes=[pltpu.VMEM((nqs, H_A, tq), f32), pltpu.VMEM((nqs, H_A, tq), f32),
                        pltpu.VMEM((nqs, H_A * V_DIM, tq), f32),
                        pltpu.VMEM((H_A, kc, tq), f32), pltpu.VMEM((H_A, kc, tq), f32)],
        compiler_params=_params(("parallel", "parallel", "arbitrary")), name="attention",
    )(qT, k, vT, k_tail, vT_tail)


def _split3(x):
    a = x.astype(bf16)
    r = x - a.astype(f32)
    b = r.astype(bf16)
    c = (r - b.astype(f32)).astype(bf16)
    return a, b, c


def _mlstm_gates(g, gT, *, reverse):
    r = lax.broadcasted_iota(jnp.int32, (CHUNK, CHUNK), 0)
    c = lax.broadcasted_iota(jnp.int32, (CHUNK, CHUNK), 1)
    tri = jnp.where((c >= r) if reverse else (c <= r), 1.0, 0.0).astype(bf16)
    triT = jnp.where((r >= c) if reverse else (r <= c), 1.0, 0.0).astype(bf16)
    g3 = _split3(g)
    bcol_all = _dot(tri, g3[0]) + _dot(tri, g3[1]) + _dot(tri, g3[2])
    t3 = _split3(gT)
    brow_all = _dot(t3[0], triT) + _dot(t3[1], triT) + _dot(t3[2], triT)
    return bcol_all, brow_all


def _mlstm_head(h, qk, cums, q, kT, v, gT, c_sc, m_sc, *, reverse):
    d = 1 if reverse else 0
    bcol_all, brow_all = cums
    r = lax.broadcasted_iota(jnp.int32, (CHUNK, CHUNK), 0)
    c = lax.broadcasted_iota(jnp.int32, (CHUNK, CHUNK), 1)
    mask = (c >= r) if reverse else (c <= r)
    ones_blk = jnp.ones((CHUNK, LANES), bf16)
    tile_shape = (SUBLANES, LANES)
    ji = d * H_B + h
    jf = 2 * H_B + d * H_B + h
    bcb = jnp.broadcast_to(bcol_all[:, jf:jf + 1], (CHUNK, LANES))
    br = brow_all[jf:jf + 1, :]
    a_row = gT[ji:ji + 1, :] - br
    gtot = jnp.broadcast_to(br[:, 0:1] if reverse else br[:, CHUNK - 1:CHUNK], tile_shape)
    m_prev = m_sc[h]
    mpb = jnp.concatenate([m_prev] * (CHUNK // SUBLANES), axis=0)
    qh = q[:, h * DH_B:(h + 1) * DH_B]
    kTh = kT[h * DH_B:(h + 1) * DH_B, :]
    v_aug = jnp.concatenate([v[:, h * DH_B:(h + 1) * DH_B], ones_blk], axis=1)
    amask = jnp.where(mask, a_row, -jnp.inf)
    mt = jnp.maximum(jnp.broadcast_to(jnp.max(amask, axis=1, keepdims=True), (CHUNK, LANES)), mpb)
    s = (qk * jnp.exp(amask - mt)).astype(bf16)
    e_in = jnp.exp(mpb - mt)
    c_prev = c_sc[h]
    numden = _dot(s, v_aug) + jnp.concatenate([e_in, e_in], axis=1) * _dot(qh, c_prev.astype(bf16))
    den = numden[:, DH_B:]
    out = numden[:, :DH_B] * (1.0 / jnp.maximum(jnp.abs(den), jnp.exp(-(bcb + mt))))
    a_end = gtot[0:1, :] + a_row
    m_loc = jnp.broadcast_to(jnp.max(a_end, axis=1, keepdims=True), tile_shape)
    m_new = jnp.maximum(gtot + m_prev, m_loc)
    sp = jnp.exp(gtot + m_prev - m_new)
    sl = jnp.exp(m_loc - m_new)
    kw = (kTh.astype(f32) * jnp.exp(a_end - m_loc[0:1, :])).astype(bf16)
    rep = lambda x: jnp.concatenate([jnp.concatenate([x] * (DH_B // SUBLANES), axis=0)] * 2, axis=1)
    c_sc[h] = rep(sp) * c_prev + rep(sl) * _dot(kw, v_aug)
    m_sc[h] = m_new
    return out


def _mlstm_chunks(loaders, stores, c_sc, m_sc, *, reverse):
    qk_head = lambda q, kT, h: _dot(q[:, h * DH_B:(h + 1) * DH_B], kT[h * DH_B:(h + 1) * DH_B, :])
    cur = loaders[0]()
    cums = _mlstm_gates(cur[3], cur[4], reverse=reverse)
    qks = [qk_head(cur[0], cur[1], h) for h in range(H_B)]
    for i, store in enumerate(stores):
        q, kT, v, _, gT = cur
        nxt = loaders[i + 1]() if i + 1 < len(loaders) else None
        cums_n = _mlstm_gates(nxt[3], nxt[4], reverse=reverse) if nxt is not None else None
        qks_n, outs = [], []
        for h in range(H_B):
            if nxt is not None:
                qks_n.append(qk_head(nxt[0], nxt[1], h))
            outs.append(_mlstm_head(h, qks[h], cums, q, kT, v, gT, c_sc, m_sc, reverse=reverse))
        store(jnp.concatenate(outs, axis=1))
        cur, cums, qks = nxt, cums_n, qks_n


def _mlstm_kernel(q_ref, kT_ref, v_ref, g_ref, gT_ref, qt_ref, kTt_ref, vt_ref, gt_ref, gTt_ref, *rest, reverse):
    if reverse:
        hf_ref, hft_ref, o_ref, ot_ref, c_sc, m_sc = rest
    else:
        o_ref, ot_ref, c_sc, m_sc = rest
    j = pl.program_id(1)
    nj = pl.num_programs(1)
    nch = kT_ref.shape[0]
    run = functools.partial(_mlstm_chunks, c_sc=c_sc, m_sc=m_sc, reverse=reverse)

    def tail_chunk():
        def store(o):
            ot_ref[...] = ((o + hft_ref[...].astype(f32)) if reverse else o).astype(bf16)
        run([lambda: (qt_ref[...], kTt_ref[0], vt_ref[...], gt_ref[...], gTt_ref[0])], [store])

    @pl.when(j == 0)
    def _():
        c_sc[...] = jnp.zeros(c_sc.shape, f32)
        m_sc[...] = jnp.full(m_sc.shape, NEG, f32)
        if not reverse:
            tail_chunk()

    def loader(c):
        rows = slice(c * CHUNK, (c + 1) * CHUNK)
        return lambda: (q_ref[rows, :], kT_ref[c], v_ref[rows, :], g_ref[rows, :], gT_ref[c])

    def storer(c):
        rows = slice(c * CHUNK, (c + 1) * CHUNK)

        def store(o):
            o_ref[rows, :] = ((o + hf_ref[rows, :].astype(f32)) if reverse else o).astype(bf16)
        return store

    order = list(range(nch - 1, -1, -1) if reverse else range(nch))
    run([loader(c) for c in order], [storer(c) for c in order])

    if reverse:
        @pl.when(j == nj - 1)
        def _():
            tail_chunk()


def _mlstm(x_arrs, t_arrs, fwd_out=None, *, batch, seq, rb):
    reverse = fwd_out is not None
    mq, mkT, mv, g, gT = x_arrs
    mq_t, mkT_t, mv_t, g_t, gT_t = t_arrs
    nj = seq // rb
    nch = rb // CHUNK
    blk = (lambda b, j: b * nj + (nj - 1 - j)) if reverse else (lambda b, j: b * nj + j)
    in_specs = [
        pl.BlockSpec((rb, H_B * DH_B), lambda b, j: (blk(b, j), 0)),
        pl.BlockSpec((nch, H_B * DH_B, CHUNK), lambda b, j: (blk(b, j), 0, 0)),
        pl.BlockSpec((rb, H_B * DH_B), lambda b, j: (blk(b, j), 0)),
        pl.BlockSpec((rb, LANES), lambda b, j: (blk(b, j), 0)),
        pl.BlockSpec((nch, 4 * H_B, CHUNK), lambda b, j: (blk(b, j), 0, 0)),
        pl.BlockSpec((TAIL, H_B * DH_B), lambda b, j: (b, 0)),
        pl.BlockSpec((1, H_B * DH_B, CHUNK), lambda b, j: (b, 0, 0)),
        pl.BlockSpec((TAIL, H_B * DH_B), lambda b, j: (b, 0)),
        pl.BlockSpec((TAIL, LANES), lambda b, j: (b, 0)),
        pl.BlockSpec((1, 4 * H_B, CHUNK), lambda b, j: (b, 0, 0)),
    ]
    out_specs = (pl.BlockSpec((rb, H_B * DH_B), lambda b, j: (blk(b, j), 0)),
                 pl.BlockSpec((TAIL, H_B * DH_B), lambda b, j: (b, 0)))
    out_shape = (jax.ShapeDtypeStruct((batch * seq, H_B * DH_B), bf16),
                 jax.ShapeDtypeStruct((batch * TAIL, H_B * DH_B), bf16))
    operands = (mq, mkT, mv, g, gT, mq_t, mkT_t, mv_t, g_t, gT_t)
    if reverse:
        in_specs = in_specs + list(out_specs)
        operands = operands + tuple(fwd_out)
    return pl.pallas_call(
        functools.partial(_mlstm_kernel, reverse=reverse),
        grid=(batch, nj), in_specs=in_specs, out_specs=out_specs, out_shape=out_shape,
        scratch_shapes=[pltpu.VMEM((H_B, DH_B, 2 * DH_B), f32), pltpu.VMEM((H_B, SUBLANES, LANES), f32)],
        compiler_params=_params(("parallel", "arbitrary")), name="mlstm_bwd" if reverse else "mlstm_fwd",
    )(*operands)


def _merge_kernel(attn_ref, hm_ref, so_ref, sga_ref, sgb_ref, res_ref, mn_ref, woa_ref, wom_ref, wout_ref, o_ref):
    hm = hm_ref[...].astype(f32)
    mn = mn_ref[...]
    parts = [_rms(hm[:, h * DH_B:(h + 1) * DH_B], mn[:, h * DH_B:(h + 1) * DH_B]) for h in range(H_B)]
    hmn = (jnp.concatenate(parts, axis=1) * so_ref[...].astype(f32)).astype(bf16)
    y = (sga_ref[...].astype(f32) * _dot(attn_ref[...], woa_ref[...])
         + sgb_ref[...].astype(f32) * _dot(hmn, wom_ref[...]))
    o_ref[...] = res_ref[...] + _dot(y.astype(bf16), wout_ref[...])


def _merge(attn, hm, so, sga, sgb, res, w, *, tm):
    rows = attn.shape[0]
    row_spec = lambda width: pl.BlockSpec((tm, width), lambda i: (i, 0))
    in_specs = [row_spec(H_A * V_DIM), row_spec(H_B * DH_B), row_spec(H_B * DH_B),
                row_spec(D_MODEL), row_spec(D_MODEL), row_spec(D_MODEL),
                _const_spec((1, H_B * DH_B)), _const_spec(w["woa"].shape), _const_spec(w["wom"].shape),
                _const_spec(w["wout"].shape)]
    return pl.pallas_call(
        _merge_kernel, grid=(rows // tm,), in_specs=in_specs, out_specs=row_spec(D_MODEL),
        out_shape=jax.ShapeDtypeStruct((rows, D_MODEL), f32),
        compiler_params=_params(("parallel",)), name="merge",
    )(attn, hm, so, sga, sgb, res, w["mlstm_norm"], w["woa"], w["wom"], w["wout"])


def _gelu_tanh(x):
    return 0.5 * x * (1.0 + jnp.tanh(math.sqrt(2.0 / math.pi) * (x + 0.044715 * (x * x * x))))


def _ffn_kernel(h_ref, prev_ref, prevt_ref, next_ref, nf_ref, wup_ref, wgate_ref, cw_ref, cb_ref, wdown_ref,
                nfin_ref, o_ref, *, tiles_per_batch):
    i = pl.program_id(0)
    tm = h_ref.shape[0]
    first = (i % tiles_per_batch) == 0
    last = (i % tiles_per_batch) == tiles_per_batch - 1
    nf = nf_ref[...]
    h = h_ref[...]
    prev = jnp.where(first, prevt_ref[...], prev_ref[...])
    u_ext = _rms(jnp.concatenate([prev, h, next_ref[...]], axis=0), nf)
    u = u_ext[SUBLANES:tm + SUBLANES, :].astype(bf16)
    u_ext = u_ext.astype(bf16)
    row = lax.broadcasted_iota(jnp.int32, (tm, 1), 0)

    def up_gate(lo, hi):
        return _dot(u_ext, wup_ref[:, lo:hi]), _dot(u, wgate_ref[:, lo:hi])

    acc = h
    ahead = [up_gate(*FF_CHUNKS[j]) for j in range(FF_LOOKAHEAD)]
    for ci, (lo, hi) in enumerate(FF_CHUNKS):
        if ci + FF_LOOKAHEAD < len(FF_CHUNKS):
            ahead.append(up_gate(*FF_CHUNKS[ci + FF_LOOKAHEAD]))
        a_ext, gate = ahead.pop(0)
        a = a_ext[SUBLANES:tm + SUBLANES, :]
        a_prev = a_ext[SUBLANES - 1:SUBLANES, :]
        a_next = jnp.where(last, 0.0, a_ext[tm + SUBLANES:tm + SUBLANES + 1, :])
        a_dn = jnp.where(row == 0, a_prev, pltpu.roll(a, 1, axis=0))
        a_up = jnp.where(row == tm - 1, a_next, pltpu.roll(a, tm - 1, axis=0))
        cw = cw_ref[:, lo:hi]
        conv = a_dn * cw[0:1, :] + a * cw[1:2, :] + a_up * cw[2:3, :] + cb_ref[:, lo:hi]
        gated = (_gelu_tanh(conv) * gate).astype(bf16)
        acc = acc + _dot(gated, wdown_ref[lo:hi, :])
    o_ref[...] = _rms(acc, nfin_ref[...])


def _ffn(h1, h1_tail, w, *, seq, tm):
    rows = h1.shape[0]
    n = rows // tm
    tpb = seq // tm
    hb = tm // SUBLANES
    in_specs = [
        pl.BlockSpec((tm, D_MODEL), lambda i: (i, 0)),
        pl.BlockSpec((SUBLANES, D_MODEL), lambda i: (jnp.maximum(i * hb - 1, 0), 0)),
        pl.BlockSpec((SUBLANES, D_MODEL), lambda i: ((i // tpb) * (TAIL // SUBLANES) + N_META // SUBLANES - 1, 0)),
        pl.BlockSpec((SUBLANES, D_MODEL), lambda i: (jnp.minimum((i + 1) * hb, n * hb - 1), 0)),
        _const_spec((1, D_MODEL)), _const_spec(w["wup"].shape), _const_spec(w["wgate"].shape),
        _const_spec((SUBLANES, D_FF)), _const_spec((1, D_FF)), _const_spec(w["wdown"].shape),
        _const_spec((1, D_MODEL)),
    ]
    return pl.pallas_call(
        functools.partial(_ffn_kernel, tiles_per_batch=tpb),
        grid=(n,), in_specs=in_specs, out_specs=pl.BlockSpec((tm, D_MODEL), lambda i: (i, 0)),
        out_shape=jax.ShapeDtypeStruct((rows, D_MODEL), f32),
        compiler_params=_params(("parallel",)), name="conv_ffn",
    )(h1, h1, h1_tail, h1, w["norm_ffn"], w["wup"], w["wgate"], w["conv_w"], w["conv_b"], w["wdown"],
      w["norm_final"])


def _prep_weights(norm_mix, w_in, q_norm, w_uq, kv_norm, w_ukv, b_igate, b_fgate, mlstm_norm, w_o_attn, w_o_mlstm,
                  w_out, norm_ffn, w_up, w_gate, conv_w, conv_b, w_down, norm_final):
    w = w_in[0]
    offs = [0]
    for s in IN_SIZES:
        offs.append(offs[-1] + s)
    col = lambda j: w[:, offs[j]:offs[j + 1]]
    zeros = lambda r, c: jnp.zeros((r, c), f32)
    w_kr = col(2)
    half = ROPE // 2
    kr_main = jnp.concatenate([zeros(D_MODEL, NOPE), w_kr, zeros(D_MODEL, HEAD_PAD - QK_DIM)], axis=1)
    kr_rot = jnp.concatenate([zeros(D_MODEL, NOPE), -w_kr[:, half:], w_kr[:, :half],
                              zeros(D_MODEL, HEAD_PAD - QK_DIM)], axis=1)
    wa = jnp.concatenate([col(0), col(1), kr_main, kr_rot], axis=1)
    wb = jnp.concatenate([col(3), col(4), col(5), col(6), col(7), col(8), zeros(D_MODEL, LANES - 4 * H_B)], axis=1)
    wc = jnp.concatenate([col(9), col(10)], axis=1)
    wuq = w_uq[0].reshape(Q_LORA, H_A, QK_DIM)
    nope, rope = wuq[:, :, :NOPE], wuq[:, :, NOPE:]
    zq = lambda c: jnp.zeros((Q_LORA, H_A, c), f32)
    q_main = jnp.concatenate([nope, rope, zq(HEAD_PAD - QK_DIM)], axis=-1).reshape(Q_LORA, H_A * HEAD_PAD)
    q_rot = jnp.concatenate([zq(NOPE), -rope[:, :, half:], rope[:, :, :half], zq(HEAD_PAD - QK_DIM)],
                            axis=-1).reshape(Q_LORA, H_A * HEAD_PAD)
    wq = jnp.concatenate([q_main, q_rot], axis=1)
    wukv = w_ukv[0].reshape(KV_LORA, H_A, NOPE + V_DIM)
    wk = jnp.concatenate([wukv[:, :, :NOPE], jnp.zeros((KV_LORA, H_A, HEAD_PAD - NOPE), f32)],
                         axis=-1).reshape(KV_LORA, H_A * HEAD_PAD)
    wv = wukv[:, :, NOPE:].reshape(KV_LORA, H_A * V_DIM)
    wkv = jnp.concatenate([wk, wv], axis=1)
    gbias = jnp.concatenate([b_igate[0].reshape(-1), b_fgate[0].reshape(-1), jnp.zeros((LANES - 4 * H_B,), f32)])
    cw = jnp.concatenate([conv_w[0], jnp.zeros((SUBLANES - CONV_W, D_FF), f32)], axis=0)
    return dict(
        norm_mix=norm_mix[0][None], wa=wa.astype(bf16), wb=wb.astype(bf16), wc=wc.astype(bf16),
        q_norm=q_norm[0][None], wq=wq.astype(bf16), kv_norm=kv_norm[0][None], wkv=wkv.astype(bf16),
        gbias=gbias[None], mlstm_norm=mlstm_norm[0][None], woa=w_o_attn[0].astype(bf16),
        wom=w_o_mlstm[0].astype(bf16), wout=w_out[0].astype(bf16), norm_ffn=norm_ffn[0][None],
        wup=w_up[0].astype(bf16), wgate=w_gate[0].astype(bf16), conv_w=cw, conv_b=conv_b[0][None],
        wdown=w_down[0].astype(bf16), norm_final=norm_final[None])


def _rope_tables(positions):
    inv = ROPE_THETA ** (-jnp.arange(0, ROPE, 2, dtype=f32) / ROPE)
    ang = positions.astype(f32)[:, None] * inv[None, :]
    n = positions.shape[0]
    cos, sin = jnp.cos(ang), jnp.sin(ang)
    cm = jnp.concatenate([jnp.ones((n, NOPE), f32), cos, cos, jnp.zeros((n, HEAD_PAD - QK_DIM), f32)], axis=1)
    sm = jnp.concatenate([jnp.zeros((n, NOPE), f32), sin, sin, jnp.zeros((n, HEAD_PAD - QK_DIM), f32)], axis=1)
    return cm, sm


def _tiles(batch, seq):
    tm = math.gcd(seq, 512)
    tm_tail = math.gcd(batch * TAIL, 512)
    tkb = math.gcd(seq, 4096)
    tqb = math.gcd(seq, 2048)
    return dict(tm=tm, tm_tail=tm_tail, tqb=tqb, tkb=tkb, rb=math.gcd(seq, 1024))


def _encode(x, meta_tokens, w):
    batch, seq, _ = x.shape
    t = _tiles(batch, seq)
    xf = x.reshape(batch * seq, D_MODEL)
    tail_one = jnp.concatenate([meta_tokens.astype(f32), jnp.zeros((TAIL - N_META, D_MODEL), f32)], axis=0)
    tailf = jnp.tile(tail_one, (batch, 1))

    cm_x, sm_x = _rope_tables(N_META + jnp.arange(seq))
    pos_t = jnp.tile(jnp.arange(TAIL), t["tm_tail"] // TAIL)
    cm_t, sm_t = _rope_tables(pos_t)
    nper = seq // t["tm"]
    xs = _inproj(xf, cm_x, sm_x, lambda i: (i % nper, 0), w, tm=t["tm"], cq=TQ, ck=KC, mask_tail=False)
    ts = _inproj(tailf, cm_t, sm_t, lambda i: (0, 0), w, tm=t["tm_tail"], cq=TAIL, ck=TAIL, mask_tail=True)
    qT, k, vT, mq, mkT, mv, so, g, gT, sga, sgb = xs
    qT_t, k_t, vT_t, mq_t, mkT_t, mv_t, so_t, g_t, gT_t, sga_t, sgb_t = ts

    attn = _attention(qT, k, vT, k_t, vT_t, batch=batch, seq=seq, tqb=t["tqb"], tkb=t["tkb"], q_is_tail=False)
    attn_t = _attention(qT_t, k, vT, k_t, vT_t, batch=batch, seq=seq, tqb=TAIL, tkb=t["tkb"], q_is_tail=True)

    x_arrs = (mq, mkT, mv, g, gT)
    t_arrs = (mq_t, mkT_t, mv_t, g_t, gT_t)
    fwd = _mlstm(x_arrs, t_arrs, batch=batch, seq=seq, rb=t["rb"])
    hm, hm_t = _mlstm(x_arrs, t_arrs, fwd, batch=batch, seq=seq, rb=t["rb"])

    h1 = _merge(attn, hm, so, sga, sgb, xf, w, tm=t["tm"])
    h1_t = _merge(attn_t, hm_t, so_t, sga_t, sgb_t, tailf, w, tm=t["tm_tail"])
    y = _ffn(h1, h1_t, w, seq=seq, tm=t["tm"])
    return y.reshape(batch, seq, D_MODEL)


def kernel(x_prompt, x_sample, meta_tokens, norm_mix, w_in, q_norm, w_uq, kv_norm, w_ukv, b_igate, b_fgate,
           mlstm_norm, w_o_attn, w_o_mlstm, w_out, norm_ffn, w_up, w_gate, conv_w, conv_b, w_down, norm_final):
    w = _prep_weights(norm_mix, w_in, q_norm, w_uq, kv_norm, w_ukv, b_igate, b_fgate, mlstm_norm, w_o_attn,
                      w_o_mlstm, w_out, norm_ffn, w_up, w_gate, conv_w, conv_b, w_down, norm_final)
    return (_encode(x_prompt, meta_tokens, w), _encode(x_sample, meta_tokens, w))
```

```python
import functools
import math

import jax
import jax.numpy as jnp
from jax import lax
from jax.experimental import pallas as pl
from jax.experimental.pallas import tpu as pltpu

D_MODEL = 1024
N_META = 16
H_A = 8
Q_LORA = 256
KV_LORA = 128
NOPE = 64
ROPE = 32
QK_DIM = NOPE + ROPE
V_DIM = 64
ROPE_THETA = 10000.0
H_B = 4
DH_B = 128
D_FF = 2816
CONV_W = 3
EPS = 1e-6
NEG = -1e30
IN_SIZES = (Q_LORA, KV_LORA, ROPE, H_B * DH_B, H_B * DH_B, H_B * DH_B, H_B * DH_B, 2 * H_B, 2 * H_B, D_MODEL, D_MODEL)

LANES = 128
SUBLANES = 8
TAIL = 128
HEAD_PAD = LANES
CHUNK = 128
TQ = 256
KC = 512
ATTN_UNROLL = 4
PV_LAG = 1
VMEM_LIMIT = 56 * 1024 * 1024
QSCALE = (QK_DIM ** -0.5) * math.log2(math.e)
FF_CHUNKS = tuple((lo, min(lo + 512, D_FF)) for lo in range(0, D_FF, 512))
FF_LOOKAHEAD = 2

f32 = jnp.float32
bf16 = jnp.bfloat16


def _rms(x, g):
    return x * lax.rsqrt(jnp.mean(x * x, axis=-1, keepdims=True) + EPS) * g


def _sigmoid(x):
    return 1.0 / (1.0 + jnp.exp(-x))


def _log_sigmoid(x):
    return jnp.minimum(x, 0.0) - jnp.log1p(jnp.exp(-jnp.abs(x)))


def _dot(a, b):
    return jnp.dot(a, b, preferred_element_type=f32)


def _const_spec(shape):
    zeros = (0,) * len(shape)
    return pl.BlockSpec(shape, lambda *_: zeros, pipeline_mode=pl.Buffered(1))


def _params(sem):
    return pltpu.CompilerParams(dimension_semantics=sem, vmem_limit_bytes=VMEM_LIMIT)


def _inproj_kernel(h_ref, cm_ref, sm_ref, nmix_ref, wa_ref, wb_ref, wc_ref, qn_ref, wq_ref, kvn_ref, wkv_ref,
                   gb_ref, qT_ref, k_ref, vT_ref, mq_ref, mkT_ref, mv_ref, so_ref, gT_ref, sga_ref, sgb_ref,
                   *, cq, ck, mask_tail):
    tm = h_ref.shape[0]
    u = _rms(h_ref[...], nmix_ref[...]).astype(bf16)
    cm = cm_ref[...]
    sm = sm_ref[...]

    za = _dot(u, wa_ref[...])
    qn = _rms(za[:, :Q_LORA], qn_ref[...]).astype(bf16)
    kvn = _rms(za[:, Q_LORA:Q_LORA + KV_LORA], kvn_ref[...]).astype(bf16)
    o_kr = Q_LORA + KV_LORA
    kr = za[:, o_kr:o_kr + HEAD_PAD] * cm + za[:, o_kr + HEAD_PAD:o_kr + 2 * HEAD_PAD] * sm
    cm8 = jnp.concatenate([cm] * H_A, axis=1)
    sm8 = jnp.concatenate([sm] * H_A, axis=1)
    q = (_dot(qn, wq_ref[:, :H_A * HEAD_PAD]) * cm8 + _dot(qn, wq_ref[:, H_A * HEAD_PAD:]) * sm8) * QSCALE
    qT = q.T
    for c in range(tm // cq):
        qT_ref[c] = qT[:, c * cq:(c + 1) * cq].astype(bf16)
    kv = _dot(kvn, wkv_ref[...])
    for h in range(H_A):
        kh = (kv[:, h * HEAD_PAD:(h + 1) * HEAD_PAD] + kr).astype(bf16)
        for c in range(tm // ck):
            k_ref[c, h] = kh[c * ck:(c + 1) * ck, :]
    vT = kv[:, H_A * HEAD_PAD:].T
    for c in range(tm // ck):
        vT_ref[c] = vT[:, c * ck:(c + 1) * ck].astype(bf16)

    zb = _dot(u, wb_ref[...])
    wm = H_B * DH_B
    mq_ref[...] = zb[:, 0:wm].astype(bf16)
    mkT = (zb[:, wm:2 * wm] * (DH_B ** -0.5)).T
    for c in range(tm // CHUNK):
        mkT_ref[c] = mkT[:, c * CHUNK:(c + 1) * CHUNK].astype(bf16)
    mv_ref[...] = zb[:, 2 * wm:3 * wm].astype(bf16)
    so_ref[...] = _sigmoid(zb[:, 3 * wm:4 * wm]).astype(bf16)
    gz = zb[:, 4 * wm:4 * wm + LANES] + gb_ref[...]
    lane = lax.broadcasted_iota(jnp.int32, gz.shape, 1)
    gates = jnp.where(lane < 2 * H_B, gz, _log_sigmoid(gz))
    if mask_tail:
        row = lax.broadcasted_iota(jnp.int32, gz.shape, 0)
        pad = (row % TAIL) >= N_META
        gates = jnp.where(pad, jnp.where(lane < 2 * H_B, NEG, 0.0), gates)
    gT = gates.T
    for c in range(tm // CHUNK):
        gT_ref[c] = gT[:4 * H_B, c * CHUNK:(c + 1) * CHUNK]

    zc = _dot(u, wc_ref[...])
    sga_ref[...] = _sigmoid(zc[:, :D_MODEL]).astype(bf16)
    sgb_ref[...] = _sigmoid(zc[:, D_MODEL:]).astype(bf16)


def _inproj(h, cm, sm, tab_map, w, *, tm, cq, ck, mask_tail):
    rows = h.shape[0]
    n = rows // tm
    row_spec = lambda width: pl.BlockSpec((tm, width), lambda i: (i, 0))
    chunk_spec = lambda height, cw: pl.BlockSpec((tm // cw, height, cw), lambda i: (i, 0, 0))
    out_shape = (
        jax.ShapeDtypeStruct((rows // cq, H_A * HEAD_PAD, cq), bf16),
        jax.ShapeDtypeStruct((rows // ck, H_A, ck, HEAD_PAD), bf16),
        jax.ShapeDtypeStruct((rows // ck, H_A * V_DIM, ck), bf16),
        jax.ShapeDtypeStruct((rows, H_B * DH_B), bf16),
        jax.ShapeDtypeStruct((rows // CHUNK, H_B * DH_B, CHUNK), bf16),
        jax.ShapeDtypeStruct((rows, H_B * DH_B), bf16),
        jax.ShapeDtypeStruct((rows, H_B * DH_B), bf16),
        jax.ShapeDtypeStruct((rows // CHUNK, 4 * H_B, CHUNK), f32),
        jax.ShapeDtypeStruct((rows, D_MODEL), bf16),
        jax.ShapeDtypeStruct((rows, D_MODEL), bf16),
    )
    out_specs = (
        chunk_spec(H_A * HEAD_PAD, cq),
        pl.BlockSpec((tm // ck, H_A, ck, HEAD_PAD), lambda i: (i, 0, 0, 0)), chunk_spec(H_A * V_DIM, ck),
        row_spec(H_B * DH_B), chunk_spec(H_B * DH_B, CHUNK), row_spec(H_B * DH_B), row_spec(H_B * DH_B),
        chunk_spec(4 * H_B, CHUNK), row_spec(D_MODEL), row_spec(D_MODEL),
    )
    in_specs = [
        row_spec(D_MODEL),
        pl.BlockSpec((tm, LANES), tab_map), pl.BlockSpec((tm, LANES), tab_map),
        _const_spec((1, D_MODEL)), _const_spec(w["wa"].shape), _const_spec(w["wb"].shape),
        _const_spec(w["wc"].shape), _const_spec((1, Q_LORA)), _const_spec(w["wq"].shape),
        _const_spec((1, KV_LORA)), _const_spec(w["wkv"].shape), _const_spec((1, LANES)),
    ]
    return pl.pallas_call(
        functools.partial(_inproj_kernel, cq=cq, ck=ck, mask_tail=mask_tail),
        grid=(n,), in_specs=in_specs, out_specs=out_specs, out_shape=out_shape,
        compiler_params=_params(("parallel",)), name="inproj",
    )(h, cm, sm, w["norm_mix"], w["wa"], w["wb"], w["wc"], w["q_norm"], w["wq"], w["kv_norm"], w["wkv"], w["gbias"])


def _attn_scores(qT_ref, k_ref, t, s_ref):
    nkc = k_ref.shape[0]
    if isinstance(t, int):
        qs, c = t // nkc, t % nkc
    else:
        qs = lax.div(t, nkc)
        c = t - qs * nkc

    def head(h):
        s_ref[h] = _dot(k_ref[c, h], qT_ref[qs, h * HEAD_PAD:(h + 1) * HEAD_PAD, :])
    return head


def _attn_update(vT_ref, t, s_ref, m_sc, l_sc, acc_sc, scores_head):
    nkc = vT_ref.shape[0]
    if isinstance(t, int):
        qs, c = t // nkc, t % nkc
    else:
        qs = lax.div(t, nkc)
        c = t - qs * nkc
    m_all = m_sc[qs]
    l_all = l_sc[qs]
    kc = s_ref.shape[1]
    half = kc // 2
    ones = jnp.ones((2 * SUBLANES, kc), bf16)
    ms, ls = [], []

    def value_product(h, p, alpha):
        v_aug = jnp.concatenate([vT_ref[c, h * V_DIM:(h + 1) * V_DIM, :], ones], axis=0)
        pv = _dot(v_aug, p)
        ls.append(alpha * l_all[h:h + 1, :] + pv[V_DIM:V_DIM + 1, :])
        acc_sc[qs, h * V_DIM:(h + 1) * V_DIM, :] = (alpha * acc_sc[qs, h * V_DIM:(h + 1) * V_DIM, :]
                                                    + pv[0:V_DIM, :])

    pending = []
    for h in range(H_A):
        if scores_head is not None:
            scores_head(h)
        if len(pending) == PV_LAG:
            value_product(*pending.pop(0))
        m_old = m_all[h:h + 1, :]
        m_new = jnp.maximum(m_old, jnp.max(s_ref[h], axis=0, keepdims=True))
        p = jnp.concatenate([jnp.exp2((s_ref[h, 0:half, :] - m_new).astype(bf16)),
                             jnp.exp2((s_ref[h, half:kc, :] - m_new).astype(bf16))], axis=0)
        ms.append(m_new)
        pending.append((h, p, jnp.exp2(m_old - m_new)))
    for item in pending:
        value_product(*item)
    m_sc[qs] = jnp.concatenate(ms, axis=0)
    l_sc[qs] = jnp.concatenate(ls, axis=0)


def _attn_kernel(qT_ref, k_ref, vT_ref, kt_ref, vtt_ref, o_ref, m_sc, l_sc, acc_sc, s0_sc, s1_sc, *, single_kv_block):
    ki = pl.program_id(2)
    nqs, _, tq = qT_ref.shape
    nkc = vT_ref.shape[0]
    n = nqs * nkc

    def init():
        zk = jnp.zeros((N_META, HEAD_PAD), bf16)
        k_bd = jnp.concatenate(
            [jnp.concatenate([kt_ref[0, h, 0:N_META, :] if j == h else zk for j in range(H_A)], axis=1)
             for h in range(H_A)], axis=0)
        rr = lax.broadcasted_iota(jnp.int32, (TAIL, H_A * N_META), 0)
        rc = lax.broadcasted_iota(jnp.int32, (TAIL, H_A * N_META), 1)
        rep = jnp.where((rr < N_META) & (rc % N_META == rr), 1.0, 0.0).astype(bf16)
        hv = lax.broadcasted_iota(jnp.int32, (H_A * V_DIM, H_A * N_META), 0) // V_DIM
        cv = lax.broadcasted_iota(jnp.int32, (H_A * V_DIM, H_A * N_META), 1) // N_META
        v_bd = jnp.where(hv == cv, _dot(vtt_ref[0], rep), 0.0).astype(bf16)
        for qs in range(nqs):
            s = _dot(k_bd, qT_ref[qs])
            ms, ls, ps = [], [], []
            for h in range(H_A):
                sh = s[h * N_META:(h + 1) * N_META, :]
                m = jnp.max(sh, axis=0, keepdims=True)
                p = jnp.exp2(sh - m)
                ms.append(m)
                ls.append(jnp.sum(p, axis=0, keepdims=True))
                ps.append(p)
            m_sc[qs] = jnp.concatenate(ms, axis=0)
            l_sc[qs] = jnp.concatenate(ls, axis=0)
            acc_sc[qs] = _dot(v_bd, jnp.concatenate(ps, axis=0).astype(bf16))

    def finalize():
        for qs in range(nqs):
            inv = 1.0 / l_sc[qs]
            outs = [acc_sc[qs, h * V_DIM:(h + 1) * V_DIM, :] * inv[h:h + 1, :] for h in range(H_A)]
            o = jnp.concatenate(outs, axis=0)
            o_ref[qs * tq:(qs + 1) * tq, :] = o.T.astype(bf16)

    if single_kv_block:
        init()
    else:
        pl.when(ki == 0)(init)

    first = _attn_scores(qT_ref, k_ref, 0, s0_sc)
    for h in range(H_A):
        first(h)

    bufs = (s0_sc, s1_sc)

    def group(t, last):
        for u in range(ATTN_UNROLL):
            nxt = None if (last and u == ATTN_UNROLL - 1) else _attn_scores(qT_ref, k_ref, t + u + 1, bufs[(u + 1) % 2])
            _attn_update(vT_ref, t + u, bufs[u % 2], m_sc, l_sc, acc_sc, nxt)

    def body(i, carry):
        group(ATTN_UNROLL * i, False)
        return carry
    lax.fori_loop(0, n // ATTN_UNROLL - 1, body, 0)
    group(n - ATTN_UNROLL, True)

    if single_kv_block:
        finalize()
    else:
        pl.when(ki == pl.num_programs(2) - 1)(finalize)


def _attention(qT, k, vT, k_tail, vT_tail, *, batch, seq, tqb, tkb, q_is_tail):
    tq = qT.shape[2]
    kc = vT.shape[2]
    nqb = (TAIL if q_is_tail else seq) // tqb
    nkb = seq // tkb
    nqs = tqb // tq
    assert (nqs * (tkb // kc)) % ATTN_UNROLL == 0, "the kernel consumes steps in groups of ATTN_UNROLL"
    q_rows = qT.shape[0] * tq
    in_specs = [
        pl.BlockSpec((nqs, H_A * HEAD_PAD, tq), lambda b, qi, ki: (b * nqb + qi, 0, 0)),
        pl.BlockSpec((tkb // kc, H_A, kc, HEAD_PAD), lambda b, qi, ki: (b * nkb + ki, 0, 0, 0)),
        pl.BlockSpec((tkb // kc, H_A * V_DIM, kc), lambda b, qi, ki: (b * nkb + ki, 0, 0)),
        pl.BlockSpec((1, H_A, TAIL, HEAD_PAD), lambda b, qi, ki: (b, 0, 0, 0)),
        pl.BlockSpec((1, H_A * V_DIM, TAIL), lambda b, qi, ki: (b, 0, 0)),
    ]
    return pl.pallas_call(
        functools.partial(_attn_kernel, single_kv_block=(nkb == 1)),
        grid=(batch, nqb, nkb), in_specs=in_specs,
        out_specs=pl.BlockSpec((tqb, H_A * V_DIM), lambda b, qi, ki: (b * nqb + qi, 0)),
        out_shape=jax.ShapeDtypeStruct((q_rows, H_A * V_DIM), bf16),
        scratch_shapes=[pltpu.VMEM((nqs, H_A, tq), f32), pltpu.VMEM((nqs, H_A, tq), f32),
                        pltpu.VMEM((nqs, H_A * V_DIM, tq), f32),
                        pltpu.VMEM((H_A, kc, tq), f32), pltpu.VMEM((H_A, kc, tq), f32)],
        compiler_params=_params(("parallel", "parallel", "arbitrary")), name="attention",
    )(qT, k, vT, k_tail, vT_tail)


def _mlstm_gates(gT, *, reverse):
    r = lax.broadcasted_iota(jnp.int32, (CHUNK, CHUNK), 0)
    c = lax.broadcasted_iota(jnp.int32, (CHUNK, CHUNK), 1)
    tri = jnp.where((r >= c) if reverse else (r <= c), 1.0, 0.0).astype(bf16)
    a = gT.astype(bf16)
    rem = gT - a.astype(f32)
    b = rem.astype(bf16)
    x = _dot(a, tri) + _dot(b, tri) + _dot((rem - b.astype(f32)).astype(bf16), tri)
    padded = jnp.concatenate([x, jnp.zeros((LANES - x.shape[0], CHUNK), f32)], axis=0)
    return padded.T, x


def _mlstm_head(h, qk, cums, q, kT, v, gT, c_sc, m_sc, *, reverse):
    d = 1 if reverse else 0
    bcol_all, brow_all = cums
    r = lax.broadcasted_iota(jnp.int32, (CHUNK, CHUNK), 0)
    c = lax.broadcasted_iota(jnp.int32, (CHUNK, CHUNK), 1)
    mask = (c >= r) if reverse else (c <= r)
    ones_blk = jnp.ones((CHUNK, LANES), bf16)
    tile_shape = (SUBLANES, LANES)
    ji = d * H_B + h
    jf = 2 * H_B + d * H_B + h
    bcb = jnp.broadcast_to(bcol_all[:, jf:jf + 1], (CHUNK, LANES))
    br = brow_all[jf:jf + 1, :]
    a_row = gT[ji:ji + 1, :] - br
    gtot = jnp.broadcast_to(br[:, 0:1] if reverse else br[:, CHUNK - 1:CHUNK], tile_shape)
    m_prev = m_sc[h]
    mpb = jnp.concatenate([m_prev] * (CHUNK // SUBLANES), axis=0)
    qh = q[:, h * DH_B:(h + 1) * DH_B]
    kTh = kT[h * DH_B:(h + 1) * DH_B, :]
    v_aug = jnp.concatenate([v[:, h * DH_B:(h + 1) * DH_B], ones_blk], axis=1)
    amask = jnp.where(mask, a_row, -jnp.inf)
    mt = jnp.maximum(jnp.broadcast_to(jnp.max(amask, axis=1, keepdims=True), (CHUNK, LANES)), mpb)
    s = (qk * jnp.exp(amask - mt)).astype(bf16)
    e_in = jnp.exp(mpb - mt)
    c_prev = c_sc[h]
    numden = _dot(s, v_aug) + jnp.concatenate([e_in, e_in], axis=1) * _dot(qh, c_prev.astype(bf16))
    den = numden[:, DH_B:]
    out = numden[:, :DH_B] * (1.0 / jnp.maximum(jnp.abs(den), jnp.exp(-(bcb + mt))))
    a_end = gtot[0:1, :] + a_row
    m_loc = jnp.broadcast_to(jnp.max(a_end, axis=1, keepdims=True), tile_shape)
    m_new = jnp.maximum(gtot + m_prev, m_loc)
    sp = jnp.exp(gtot + m_prev - m_new)
    kw = (kTh.astype(f32) * jnp.exp(a_end - m_new[0:1, :])).astype(bf16)
    rep = lambda x: jnp.concatenate([jnp.concatenate([x] * (DH_B // SUBLANES), axis=0)] * 2, axis=1)
    c_sc[h] = rep(sp) * c_prev + _dot(kw, v_aug)
    m_sc[h] = m_new
    return out


def _mlstm_chunks(loaders, stores, c_sc, m_sc, *, reverse):
    qk_head = lambda q, kT, h: _dot(q[:, h * DH_B:(h + 1) * DH_B], kT[h * DH_B:(h + 1) * DH_B, :])
    cur = loaders[0]()
    cums = _mlstm_gates(cur[3], reverse=reverse)
    qks = [qk_head(cur[0], cur[1], h) for h in range(H_B)]
    for i, store in enumerate(stores):
        q, kT, v, gT = cur
        nxt = loaders[i + 1]() if i + 1 < len(loaders) else None
        cums_n = _mlstm_gates(nxt[3], reverse=reverse) if nxt is not None else None
        qks_n, outs = [], []
        for h in range(H_B):
            if nxt is not None:
                qks_n.append(qk_head(nxt[0], nxt[1], h))
            outs.append(_mlstm_head(h, qks[h], cums, q, kT, v, gT, c_sc, m_sc, reverse=reverse))
        store(jnp.concatenate(outs, axis=1))
        cur, cums, qks = nxt, cums_n, qks_n


def _mlstm_kernel(q_ref, kT_ref, v_ref, gT_ref, qt_ref, kTt_ref, vt_ref, gTt_ref, *rest, reverse):
    if reverse:
        hf_ref, hft_ref, o_ref, ot_ref, c_sc, m_sc = rest
    else:
        o_ref, ot_ref, c_sc, m_sc = rest
    j = pl.program_id(1)
    nj = pl.num_programs(1)
    nch = kT_ref.shape[0]
    run = functools.partial(_mlstm_chunks, c_sc=c_sc, m_sc=m_sc, reverse=reverse)

    def tail_chunk():
        def store(o):
            ot_ref[...] = ((o + hft_ref[...].astype(f32)) if reverse else o).astype(bf16)
        run([lambda: (qt_ref[...], kTt_ref[0], vt_ref[...], gTt_ref[0])], [store])

    @pl.when(j == 0)
    def _():
        c_sc[...] = jnp.zeros(c_sc.shape, f32)
        m_sc[...] = jnp.full(m_sc.shape, NEG, f32)
        if not reverse:
            tail_chunk()

    def loader(c):
        rows = slice(c * CHUNK, (c + 1) * CHUNK)
        return lambda: (q_ref[rows, :], kT_ref[c], v_ref[rows, :], gT_ref[c])

    def storer(c):
        rows = slice(c * CHUNK, (c + 1) * CHUNK)

        def store(o):
            o_ref[rows, :] = ((o + hf_ref[rows, :].astype(f32)) if reverse else o).astype(bf16)
        return store

    order = list(range(nch - 1, -1, -1) if reverse else range(nch))
    run([loader(c) for c in order], [storer(c) for c in order])

    if reverse:
        @pl.when(j == nj - 1)
        def _():
            tail_chunk()


def _mlstm(x_arrs, t_arrs, fwd_out=None, *, batch, seq, rb):
    reverse = fwd_out is not None
    nj = seq // rb
    nch = rb // CHUNK
    blk = (lambda b, j: b * nj + (nj - 1 - j)) if reverse else (lambda b, j: b * nj + j)
    in_specs = [
        pl.BlockSpec((rb, H_B * DH_B), lambda b, j: (blk(b, j), 0)),
        pl.BlockSpec((nch, H_B * DH_B, CHUNK), lambda b, j: (blk(b, j), 0, 0)),
        pl.BlockSpec((rb, H_B * DH_B), lambda b, j: (blk(b, j), 0)),
        pl.BlockSpec((nch, 4 * H_B, CHUNK), lambda b, j: (blk(b, j), 0, 0)),
        pl.BlockSpec((TAIL, H_B * DH_B), lambda b, j: (b, 0)),
        pl.BlockSpec((1, H_B * DH_B, CHUNK), lambda b, j: (b, 0, 0)),
        pl.BlockSpec((TAIL, H_B * DH_B), lambda b, j: (b, 0)),
        pl.BlockSpec((1, 4 * H_B, CHUNK), lambda b, j: (b, 0, 0)),
    ]
    out_specs = (pl.BlockSpec((rb, H_B * DH_B), lambda b, j: (blk(b, j), 0)),
                 pl.BlockSpec((TAIL, H_B * DH_B), lambda b, j: (b, 0)))
    out_shape = (jax.ShapeDtypeStruct((batch * seq, H_B * DH_B), bf16),
                 jax.ShapeDtypeStruct((batch * TAIL, H_B * DH_B), bf16))
    operands = tuple(x_arrs) + tuple(t_arrs)
    if reverse:
        in_specs = in_specs + list(out_specs)
        operands = operands + tuple(fwd_out)
    return pl.pallas_call(
        functools.partial(_mlstm_kernel, reverse=reverse),
        grid=(batch, nj), in_specs=in_specs, out_specs=out_specs, out_shape=out_shape,
        scratch_shapes=[pltpu.VMEM((H_B, DH_B, 2 * DH_B), f32), pltpu.VMEM((H_B, SUBLANES, LANES), f32)],
        compiler_params=_params(("parallel", "arbitrary")), name="mlstm_bwd" if reverse else "mlstm_fwd",
    )(*operands)


def _merge_kernel(attn_ref, hm_ref, so_ref, sga_ref, sgb_ref, res_ref, mn_ref, woa_ref, wom_ref, wout_ref, o_ref):
    hm = hm_ref[...].astype(f32)
    mn = mn_ref[...]
    parts = [_rms(hm[:, h * DH_B:(h + 1) * DH_B], mn[:, h * DH_B:(h + 1) * DH_B]) for h in range(H_B)]
    hmn = (jnp.concatenate(parts, axis=1) * so_ref[...].astype(f32)).astype(bf16)
    y = (sga_ref[...].astype(f32) * _dot(attn_ref[...], woa_ref[...])
         + sgb_ref[...].astype(f32) * _dot(hmn, wom_ref[...]))
    o_ref[...] = res_ref[...] + _dot(y.astype(bf16), wout_ref[...])


def _merge(attn, hm, so, sga, sgb, res, w, *, tm):
    rows = attn.shape[0]
    row_spec = lambda width: pl.BlockSpec((tm, width), lambda i: (i, 0))
    in_specs = [row_spec(H_A * V_DIM), row_spec(H_B * DH_B), row_spec(H_B * DH_B),
                row_spec(D_MODEL), row_spec(D_MODEL), row_spec(D_MODEL),
                _const_spec((1, H_B * DH_B)), _const_spec(w["woa"].shape), _const_spec(w["wom"].shape),
                _const_spec(w["wout"].shape)]
    return pl.pallas_call(
        _merge_kernel, grid=(rows // tm,), in_specs=in_specs, out_specs=row_spec(D_MODEL),
        out_shape=jax.ShapeDtypeStruct((rows, D_MODEL), f32),
        compiler_params=_params(("parallel",)), name="merge",
    )(attn, hm, so, sga, sgb, res, w["mlstm_norm"], w["woa"], w["wom"], w["wout"])


def _gelu_tanh(x):
    return 0.5 * x * (1.0 + jnp.tanh(math.sqrt(2.0 / math.pi) * (x + 0.044715 * (x * x * x))))


def _ffn_kernel(h_ref, prev_ref, prevt_ref, next_ref, nf_ref, wup_ref, wgate_ref, cw_ref, cb_ref, wdown_ref,
                nfin_ref, o_ref, *, tiles_per_batch):
    i = pl.program_id(0)
    tm = h_ref.shape[0]
    first = (i % tiles_per_batch) == 0
    last = (i % tiles_per_batch) == tiles_per_batch - 1
    nf = nf_ref[...]
    h = h_ref[...]
    prev = jnp.where(first, prevt_ref[...], prev_ref[...])
    u_ext = _rms(jnp.concatenate([prev, h, next_ref[...]], axis=0), nf)
    u = u_ext[SUBLANES:tm + SUBLANES, :].astype(bf16)
    u_ext = u_ext.astype(bf16)
    row = lax.broadcasted_iota(jnp.int32, (tm, 1), 0)

    def up_gate(lo, hi):
        return _dot(u_ext, wup_ref[:, lo:hi]), _dot(u, wgate_ref[:, lo:hi])

    acc = h
    ahead = [up_gate(*FF_CHUNKS[j]) for j in range(FF_LOOKAHEAD)]
    for ci, (lo, hi) in enumerate(FF_CHUNKS):
        if ci + FF_LOOKAHEAD < len(FF_CHUNKS):
            ahead.append(up_gate(*FF_CHUNKS[ci + FF_LOOKAHEAD]))
        a_ext, gate = ahead.pop(0)
        a = a_ext[SUBLANES:tm + SUBLANES, :]
        a_prev = a_ext[SUBLANES - 1:SUBLANES, :]
        a_next = jnp.where(last, 0.0, a_ext[tm + SUBLANES:tm + SUBLANES + 1, :])
        a_dn = jnp.where(row == 0, a_prev, pltpu.roll(a, 1, axis=0))
        a_up = jnp.where(row == tm - 1, a_next, pltpu.roll(a, tm - 1, axis=0))
        cw = cw_ref[:, lo:hi]
        conv = a_dn * cw[0:1, :] + a * cw[1:2, :] + a_up * cw[2:3, :] + cb_ref[:, lo:hi]
        gated = (_gelu_tanh(conv) * gate).astype(bf16)
        acc = acc + _dot(gated, wdown_ref[lo:hi, :])
    o_ref[...] = _rms(acc, nfin_ref[...])


def _ffn(h1, h1_tail, w, *, seq, tm):
    rows = h1.shape[0]
    n = rows // tm
    tpb = seq // tm
    hb = tm // SUBLANES
    in_specs = [
        pl.BlockSpec((tm, D_MODEL), lambda i: (i, 0)),
        pl.BlockSpec((SUBLANES, D_MODEL), lambda i: (jnp.maximum(i * hb - 1, 0), 0)),
        pl.BlockSpec((SUBLANES, D_MODEL), lambda i: ((i // tpb) * (TAIL // SUBLANES) + N_META // SUBLANES - 1, 0)),
        pl.BlockSpec((SUBLANES, D_MODEL), lambda i: (jnp.minimum((i + 1) * hb, n * hb - 1), 0)),
        _const_spec((1, D_MODEL)), _const_spec(w["wup"].shape), _const_spec(w["wgate"].shape),
        _const_spec((SUBLANES, D_FF)), _const_spec((1, D_FF)), _const_spec(w["wdown"].shape),
        _const_spec((1, D_MODEL)),
    ]
    return pl.pallas_call(
        functools.partial(_ffn_kernel, tiles_per_batch=tpb),
        grid=(n,), in_specs=in_specs, out_specs=pl.BlockSpec((tm, D_MODEL), lambda i: (i, 0)),
        out_shape=jax.ShapeDtypeStruct((rows, D_MODEL), f32),
        compiler_params=_params(("parallel",)), name="conv_ffn",
    )(h1, h1, h1_tail, h1, w["norm_ffn"], w["wup"], w["wgate"], w["conv_w"], w["conv_b"], w["wdown"],
      w["norm_final"])


def _prep_weights(norm_mix, w_in, q_norm, w_uq, kv_norm, w_ukv, b_igate, b_fgate, mlstm_norm, w_o_attn, w_o_mlstm,
                  w_out, norm_ffn, w_up, w_gate, conv_w, conv_b, w_down, norm_final):
    w = w_in[0]
    offs = [0]
    for s in IN_SIZES:
        offs.append(offs[-1] + s)
    col = lambda j: w[:, offs[j]:offs[j + 1]]
    zeros = lambda r, c: jnp.zeros((r, c), f32)
    w_kr = col(2)
    half = ROPE // 2
    kr_main = jnp.concatenate([zeros(D_MODEL, NOPE), w_kr, zeros(D_MODEL, HEAD_PAD - QK_DIM)], axis=1)
    kr_rot = jnp.concatenate([zeros(D_MODEL, NOPE), -w_kr[:, half:], w_kr[:, :half],
                              zeros(D_MODEL, HEAD_PAD - QK_DIM)], axis=1)
    wa = jnp.concatenate([col(0), col(1), kr_main, kr_rot], axis=1)
    wb = jnp.concatenate([col(3), col(4), col(5), col(6), col(7), col(8), zeros(D_MODEL, LANES - 4 * H_B)], axis=1)
    wc = jnp.concatenate([col(9), col(10)], axis=1)
    wuq = w_uq[0].reshape(Q_LORA, H_A, QK_DIM)
    nope, rope = wuq[:, :, :NOPE], wuq[:, :, NOPE:]
    zq = lambda c: jnp.zeros((Q_LORA, H_A, c), f32)
    q_main = jnp.concatenate([nope, rope, zq(HEAD_PAD - QK_DIM)], axis=-1).reshape(Q_LORA, H_A * HEAD_PAD)
    q_rot = jnp.concatenate([zq(NOPE), -rope[:, :, half:], rope[:, :, :half], zq(HEAD_PAD - QK_DIM)],
                            axis=-1).reshape(Q_LORA, H_A * HEAD_PAD)
    wq = jnp.concatenate([q_main, q_rot], axis=1)
    wukv = w_ukv[0].reshape(KV_LORA, H_A, NOPE + V_DIM)
    wk = jnp.concatenate([wukv[:, :, :NOPE], jnp.zeros((KV_LORA, H_A, HEAD_PAD - NOPE), f32)],
                         axis=-1).reshape(KV_LORA, H_A * HEAD_PAD)
    wv = wukv[:, :, NOPE:].reshape(KV_LORA, H_A * V_DIM)
    wkv = jnp.concatenate([wk, wv], axis=1)
    gbias = jnp.concatenate([b_igate[0].reshape(-1), b_fgate[0].reshape(-1), jnp.zeros((LANES - 4 * H_B,), f32)])
    cw = jnp.concatenate([conv_w[0], jnp.zeros((SUBLANES - CONV_W, D_FF), f32)], axis=0)
    return dict(
        norm_mix=norm_mix[0][None], wa=wa.astype(bf16), wb=wb.astype(bf16), wc=wc.astype(bf16),
        q_norm=q_norm[0][None], wq=wq.astype(bf16), kv_norm=kv_norm[0][None], wkv=wkv.astype(bf16),
        gbias=gbias[None], mlstm_norm=mlstm_norm[0][None], woa=w_o_attn[0].astype(bf16),
        wom=w_o_mlstm[0].astype(bf16), wout=w_out[0].astype(bf16), norm_ffn=norm_ffn[0][None],
        wup=w_up[0].astype(bf16), wgate=w_gate[0].astype(bf16), conv_w=cw, conv_b=conv_b[0][None],
        wdown=w_down[0].astype(bf16), norm_final=norm_final[None])


def _rope_tables(positions):
    inv = ROPE_THETA ** (-jnp.arange(0, ROPE, 2, dtype=f32) / ROPE)
    ang = positions.astype(f32)[:, None] * inv[None, :]
    n = positions.shape[0]
    cos, sin = jnp.cos(ang), jnp.sin(ang)
    cm = jnp.concatenate([jnp.ones((n, NOPE), f32), cos, cos, jnp.zeros((n, HEAD_PAD - QK_DIM), f32)], axis=1)
    sm = jnp.concatenate([jnp.zeros((n, NOPE), f32), sin, sin, jnp.zeros((n, HEAD_PAD - QK_DIM), f32)], axis=1)
    return cm, sm


def _tiles(batch, seq):
    tm = math.gcd(seq, 512)
    tm_tail = math.gcd(batch * TAIL, 512)
    tkb = math.gcd(seq, 4096)
    tqb = math.gcd(seq, 2048)
    return dict(tm=tm, tm_wide=math.gcd(seq, 1024), tm_tail=tm_tail, tqb=tqb, tkb=tkb, rb=math.gcd(seq, 1024))


def _encode(x, meta_tokens, w):
    batch, seq, _ = x.shape
    t = _tiles(batch, seq)
    xf = x.reshape(batch * seq, D_MODEL)
    tail_one = jnp.concatenate([meta_tokens.astype(f32), jnp.zeros((TAIL - N_META, D_MODEL), f32)], axis=0)
    tailf = jnp.tile(tail_one, (batch, 1))

    cm_x, sm_x = _rope_tables(N_META + jnp.arange(seq))
    pos_t = jnp.tile(jnp.arange(TAIL), t["tm_tail"] // TAIL)
    cm_t, sm_t = _rope_tables(pos_t)
    nper = seq // t["tm"]
    xs = _inproj(xf, cm_x, sm_x, lambda i: (i % nper, 0), w, tm=t["tm"], cq=TQ, ck=KC, mask_tail=False)
    ts = _inproj(tailf, cm_t, sm_t, lambda i: (0, 0), w, tm=t["tm_tail"], cq=TAIL, ck=TAIL, mask_tail=True)
    qT, k, vT, mq, mkT, mv, so, gT, sga, sgb = xs
    qT_t, k_t, vT_t, mq_t, mkT_t, mv_t, so_t, gT_t, sga_t, sgb_t = ts

    attn = _attention(qT, k, vT, k_t, vT_t, batch=batch, seq=seq, tqb=t["tqb"], tkb=t["tkb"], q_is_tail=False)
    attn_t = _attention(qT_t, k, vT, k_t, vT_t, batch=batch, seq=seq, tqb=TAIL, tkb=t["tkb"], q_is_tail=True)

    x_arrs = (mq, mkT, mv, gT)
    t_arrs = (mq_t, mkT_t, mv_t, gT_t)
    fwd = _mlstm(x_arrs, t_arrs, batch=batch, seq=seq, rb=t["rb"])
    hm, hm_t = _mlstm(x_arrs, t_arrs, fwd, batch=batch, seq=seq, rb=t["rb"])

    h1 = _merge(attn, hm, so, sga, sgb, xf, w, tm=t["tm_wide"])
    h1_t = _merge(attn_t, hm_t, so_t, sga_t, sgb_t, tailf, w, tm=t["tm_tail"])
    y = _ffn(h1, h1_t, w, seq=seq, tm=t["tm_wide"])
    return y.reshape(batch, seq, D_MODEL)


def kernel(x_prompt, x_sample, meta_tokens, norm_mix, w_in, q_norm, w_uq, kv_norm, w_ukv, b_igate, b_fgate,
           mlstm_norm, w_o_attn, w_o_mlstm, w_out, norm_ffn, w_up, w_gate, conv_w, conv_b, w_down, norm_final):
    w = _prep_weights(norm_mix, w_in, q_norm, w_uq, kv_norm, w_ukv, b_igate, b_fgate, mlstm_norm, w_o_attn,
                      w_o_mlstm, w_out, norm_ffn, w_up, w_gate, conv_w, conv_b, w_down, norm_final)
    return (_encode(x_prompt, meta_tokens, w), _encode(x_sample, meta_tokens, w))
```

```python
import functools
import math

import jax
import jax.numpy as jnp
from jax import lax
from jax.experimental import pallas as pl
from jax.experimental.pallas import tpu as pltpu

D_MODEL = 1024
N_META = 16
H_A = 8
Q_LORA = 256
KV_LORA = 128
NOPE = 64
ROPE = 32
QK_DIM = NOPE + ROPE
V_DIM = 64
ROPE_THETA = 10000.0
H_B = 4
DH_B = 128
D_FF = 2816
CONV_W = 3
EPS = 1e-6
NEG = -1e30
IN_SIZES = (Q_LORA, KV_LORA, ROPE, H_B * DH_B, H_B * DH_B, H_B * DH_B, H_B * DH_B, 2 * H_B, 2 * H_B, D_MODEL, D_MODEL)

LANES = 128
SUBLANES = 8
TAIL = 128
HEAD_PAD = LANES
CHUNK = 128
TQ = 256
KC = 512
ROW_TILE = 512
ATTN_UNROLL = 4
PV_LAG = 1
VMEM_LIMIT = 56 * 1024 * 1024
QSCALE = (QK_DIM ** -0.5) * math.log2(math.e)
FF_CHUNKS = tuple((lo, min(lo + 512, D_FF)) for lo in range(0, D_FF, 512))
FF_LOOKAHEAD = 2

f32 = jnp.float32
bf16 = jnp.bfloat16


def _rms(x, g):
    return x * lax.rsqrt(jnp.mean(x * x, axis=-1, keepdims=True) + EPS) * g


def _sigmoid(x):
    return 1.0 / (1.0 + jnp.exp(-x))


def _log_sigmoid(x):
    return jnp.minimum(x, 0.0) - jnp.log1p(jnp.exp(-jnp.abs(x)))


def _dot(a, b):
    return jnp.dot(a, b, preferred_element_type=f32)


def _const_spec(shape):
    zeros = (0,) * len(shape)
    return pl.BlockSpec(shape, lambda *_: zeros, pipeline_mode=pl.Buffered(1))


def _params(sem):
    return pltpu.CompilerParams(dimension_semantics=sem, vmem_limit_bytes=VMEM_LIMIT)


def _inproj_kernel(h_ref, cm_ref, sm_ref, nmix_ref, wa_ref, wb_ref, wc_ref, qn_ref, wq_ref, kvn_ref, wkv_ref,
                   gb_ref, qT_ref, k_ref, vT_ref, mq_ref, mkT_ref, mv_ref, so_ref, gT_ref, sga_ref, sgb_ref,
                   *, cq, ck, mask_tail):
    tm = h_ref.shape[0]
    u = _rms(h_ref[...], nmix_ref[...]).astype(bf16)
    cm = cm_ref[...]
    sm = sm_ref[...]

    lane = lax.broadcasted_iota(jnp.int32, (tm, HEAD_PAD), 1)

    def rotary(x):
        partner = jnp.where(lane < NOPE + ROPE // 2, pltpu.roll(x, HEAD_PAD - ROPE // 2, axis=1),
                            pltpu.roll(x, ROPE // 2, axis=1))
        return x * cm + partner * sm

    za = _dot(u, wa_ref[...])
    qn = _rms(za[:, :Q_LORA], qn_ref[...]).astype(bf16)
    kvn = _rms(za[:, Q_LORA:Q_LORA + KV_LORA], kvn_ref[...]).astype(bf16)
    q = _dot(qn, wq_ref[...])
    q = jnp.concatenate([rotary(q[:, h * HEAD_PAD:(h + 1) * HEAD_PAD]) for h in range(H_A)], axis=1) * QSCALE
    qT = q.T
    for c in range(tm // cq):
        qT_ref[c] = qT[:, c * cq:(c + 1) * cq].astype(bf16)
    kv = _dot(kvn, wkv_ref[...])
    vT = kv[:, H_A * HEAD_PAD:].T
    for c in range(tm // ck):
        vT_ref[c] = vT[:, c * ck:(c + 1) * ck].astype(bf16)

    zb = _dot(u, wb_ref[...])
    wm = H_B * DH_B
    mq_ref[...] = zb[:, 0:wm].astype(bf16)
    mkT = (zb[:, wm:2 * wm] * (DH_B ** -0.5)).T
    for c in range(tm // CHUNK):
        mkT_ref[c] = mkT[:, c * CHUNK:(c + 1) * CHUNK].astype(bf16)
    mv_ref[...] = zb[:, 2 * wm:3 * wm].astype(bf16)
    so_ref[...] = _sigmoid(zb[:, 3 * wm:4 * wm]).astype(bf16)
    gz = zb[:, 4 * wm:4 * wm + LANES] + gb_ref[...]
    kr = jnp.where(lane >= NOPE, rotary(gz), 0.0)
    for h in range(H_A):
        kh = (kv[:, h * HEAD_PAD:(h + 1) * HEAD_PAD] + kr).astype(bf16)
        for c in range(tm // ck):
            k_ref[c, h] = kh[c * ck:(c + 1) * ck, :]
    gates = jnp.where(lane < 2 * H_B, gz, _log_sigmoid(gz))
    if mask_tail:
        row = lax.broadcasted_iota(jnp.int32, gz.shape, 0)
        pad = (row % TAIL) >= N_META
        gates = jnp.where(pad, jnp.where(lane < 2 * H_B, NEG, 0.0), gates)
    gT = gates.T
    for c in range(tm // CHUNK):
        gT_ref[c] = gT[:4 * H_B, c * CHUNK:(c + 1) * CHUNK]

    zc = _dot(u, wc_ref[...])
    sga_ref[...] = _sigmoid(zc[:, :D_MODEL]).astype(bf16)
    sgb_ref[...] = _sigmoid(zc[:, D_MODEL:]).astype(bf16)


def _inproj(h, cm, sm, tab_map, w, *, tm, cq, ck, mask_tail):
    rows = h.shape[0]
    n = rows // tm
    row_spec = lambda width: pl.BlockSpec((tm, width), lambda i: (i, 0))
    chunk_spec = lambda height, cw: pl.BlockSpec((tm // cw, height, cw), lambda i: (i, 0, 0))
    out_shape = (
        jax.ShapeDtypeStruct((rows // cq, H_A * HEAD_PAD, cq), bf16),
        jax.ShapeDtypeStruct((rows // ck, H_A, ck, HEAD_PAD), bf16),
        jax.ShapeDtypeStruct((rows // ck, H_A * V_DIM, ck), bf16),
        jax.ShapeDtypeStruct((rows, H_B * DH_B), bf16),
        jax.ShapeDtypeStruct((rows // CHUNK, H_B * DH_B, CHUNK), bf16),
        jax.ShapeDtypeStruct((rows, H_B * DH_B), bf16),
        jax.ShapeDtypeStruct((rows, H_B * DH_B), bf16),
        jax.ShapeDtypeStruct((rows // CHUNK, 4 * H_B, CHUNK), f32),
        jax.ShapeDtypeStruct((rows, D_MODEL), bf16),
        jax.ShapeDtypeStruct((rows, D_MODEL), bf16),
    )
    out_specs = (
        chunk_spec(H_A * HEAD_PAD, cq),
        pl.BlockSpec((tm // ck, H_A, ck, HEAD_PAD), lambda i: (i, 0, 0, 0)), chunk_spec(H_A * V_DIM, ck),
        row_spec(H_B * DH_B), chunk_spec(H_B * DH_B, CHUNK), row_spec(H_B * DH_B), row_spec(H_B * DH_B),
        chunk_spec(4 * H_B, CHUNK), row_spec(D_MODEL), row_spec(D_MODEL),
    )
    in_specs = [
        row_spec(D_MODEL),
        pl.BlockSpec((tm, LANES), tab_map), pl.BlockSpec((tm, LANES), tab_map),
        _const_spec((1, D_MODEL)), _const_spec(w["wa"].shape), _const_spec(w["wb"].shape),
        _const_spec(w["wc"].shape), _const_spec((1, Q_LORA)), _const_spec(w["wq"].shape),
        _const_spec((1, KV_LORA)), _const_spec(w["wkv"].shape), _const_spec((1, LANES)),
    ]
    return pl.pallas_call(
        functools.partial(_inproj_kernel, cq=cq, ck=ck, mask_tail=mask_tail),
        grid=(n,), in_specs=in_specs, out_specs=out_specs, out_shape=out_shape,
        compiler_params=_params(("parallel",)), name="inproj",
    )(h, cm, sm, w["norm_mix"], w["wa"], w["wb"], w["wc"], w["q_norm"], w["wq"], w["kv_norm"], w["wkv"], w["gbias"])


def _attn_scores(qT_ref, k_ref, t, s_ref):
    nkc = k_ref.shape[0]
    if isinstance(t, int):
        qs, c = t // nkc, t % nkc
    else:
        qs = lax.div(t, nkc)
        c = t - qs * nkc

    def head(h):
        s_ref[h] = _dot(k_ref[c, h], qT_ref[qs, h * HEAD_PAD:(h + 1) * HEAD_PAD, :])
    return head


def _attn_update(vT_ref, t, s_ref, m_sc, l_sc, acc_sc, scores_head):
    nkc = vT_ref.shape[0]
    if isinstance(t, int):
        qs, c = t // nkc, t % nkc
    else:
        qs = lax.div(t, nkc)
        c = t - qs * nkc
    m_all = m_sc[qs]
    l_all = l_sc[qs]
    kc = s_ref.shape[1]
    half = kc // 2
    ones = jnp.ones((2 * SUBLANES, kc), bf16)
    ms, ls = [], []

    def value_product(h, p, alpha):
        v_aug = jnp.concatenate([vT_ref[c, h * V_DIM:(h + 1) * V_DIM, :], ones], axis=0)
        pv = _dot(v_aug, p)
        ls.append(alpha * l_all[h:h + 1, :] + pv[V_DIM:V_DIM + 1, :])
        acc_sc[qs, h * V_DIM:(h + 1) * V_DIM, :] = (alpha * acc_sc[qs, h * V_DIM:(h + 1) * V_DIM, :]
                                                    + pv[0:V_DIM, :])

    pending = []
    for h in range(H_A):
        if scores_head is not None:
            scores_head(h)
        if len(pending) == PV_LAG:
            value_product(*pending.pop(0))
        m_old = m_all[h:h + 1, :]
        m_new = jnp.maximum(m_old, jnp.max(s_ref[h], axis=0, keepdims=True))
        p = jnp.concatenate([jnp.exp2((s_ref[h, 0:half, :] - m_new).astype(bf16)),
                             jnp.exp2((s_ref[h, half:kc, :] - m_new).astype(bf16))], axis=0)
        ms.append(m_new)
        pending.append((h, p, jnp.exp2(m_old - m_new)))
    for item in pending:
        value_product(*item)
    m_sc[qs] = jnp.concatenate(ms, axis=0)
    l_sc[qs] = jnp.concatenate(ls, axis=0)


def _attn_kernel(qT_ref, k_ref, vT_ref, kt_ref, vtt_ref, o_ref, m_sc, l_sc, acc_sc, s0_sc, s1_sc, *, single_kv_block):
    ki = pl.program_id(2)
    nqs, _, tq = qT_ref.shape
    nkc = vT_ref.shape[0]
    n = nqs * nkc

    def init():
        zk = jnp.zeros((N_META, HEAD_PAD), bf16)
        k_bd = jnp.concatenate(
            [jnp.concatenate([kt_ref[0, h, 0:N_META, :] if j == h else zk for j in range(H_A)], axis=1)
             for h in range(H_A)], axis=0)
        rr = lax.broadcasted_iota(jnp.int32, (TAIL, H_A * N_META), 0)
        rc = lax.broadcasted_iota(jnp.int32, (TAIL, H_A * N_META), 1)
        rep = jnp.where((rr < N_META) & (rc % N_META == rr), 1.0, 0.0).astype(bf16)
        hv = lax.broadcasted_iota(jnp.int32, (H_A * V_DIM, H_A * N_META), 0) // V_DIM
        cv = lax.broadcasted_iota(jnp.int32, (H_A * V_DIM, H_A * N_META), 1) // N_META
        v_bd = jnp.where(hv == cv, _dot(vtt_ref[0], rep), 0.0).astype(bf16)
        for qs in range(nqs):
            s = _dot(k_bd, qT_ref[qs])
            ms, ls, ps = [], [], []
            for h in range(H_A):
                sh = s[h * N_META:(h + 1) * N_META, :]
                m = jnp.max(sh, axis=0, keepdims=True)
                p = jnp.exp2(sh - m)
                ms.append(m)
                ls.append(jnp.sum(p, axis=0, keepdims=True))
                ps.append(p)
            m_sc[qs] = jnp.concatenate(ms, axis=0)
            l_sc[qs] = jnp.concatenate(ls, axis=0)
            acc_sc[qs] = _dot(v_bd, jnp.concatenate(ps, axis=0).astype(bf16))

    def finalize():
        for qs in range(nqs):
            inv = 1.0 / l_sc[qs]
            outs = [acc_sc[qs, h * V_DIM:(h + 1) * V_DIM, :] * inv[h:h + 1, :] for h in range(H_A)]
            o = jnp.concatenate(outs, axis=0)
            o_ref[qs * tq:(qs + 1) * tq, :] = o.T.astype(bf16)

    if single_kv_block:
        init()
    else:
        pl.when(ki == 0)(init)

    first = _attn_scores(qT_ref, k_ref, 0, s0_sc)
    for h in range(H_A):
        first(h)

    bufs = (s0_sc, s1_sc)

    def group(t, last):
        for u in range(ATTN_UNROLL):
            nxt = None if (last and u == ATTN_UNROLL - 1) else _attn_scores(qT_ref, k_ref, t + u + 1, bufs[(u + 1) % 2])
            _attn_update(vT_ref, t + u, bufs[u % 2], m_sc, l_sc, acc_sc, nxt)

    def body(i, carry):
        group(ATTN_UNROLL * i, False)
        return carry
    lax.fori_loop(0, n // ATTN_UNROLL - 1, body, 0)
    group(n - ATTN_UNROLL, True)

    if single_kv_block:
        finalize()
    else:
        pl.when(ki == pl.num_programs(2) - 1)(finalize)


def _attention(qT, k, vT, k_tail, vT_tail, *, batch, seq, tqb, tkb, q_is_tail):
    tq = qT.shape[2]
    kc = vT.shape[2]
    nqb = (TAIL if q_is_tail else seq) // tqb
    nkb = seq // tkb
    nqs = tqb // tq
    assert (nqs * (tkb // kc)) % ATTN_UNROLL == 0, "the kernel consumes steps in groups of ATTN_UNROLL"
    q_rows = qT.shape[0] * tq
    in_specs = [
        pl.BlockSpec((nqs, H_A * HEAD_PAD, tq), lambda b, qi, ki: (b * nqb + qi, 0, 0)),
        pl.BlockSpec((tkb // kc, H_A, kc, HEAD_PAD), lambda b, qi, ki: (b * nkb + ki, 0, 0, 0)),
        pl.BlockSpec((tkb // kc, H_A * V_DIM, kc), lambda b, qi, ki: (b * nkb + ki, 0, 0)),
        pl.BlockSpec((1, H_A, TAIL, HEAD_PAD), lambda b, qi, ki: (b, 0, 0, 0)),
        pl.BlockSpec((1, H_A * V_DIM, TAIL), lambda b, qi, ki: (b, 0, 0)),
    ]
    return pl.pallas_call(
        functools.partial(_attn_kernel, single_kv_block=(nkb == 1)),
        grid=(batch, nqb, nkb), in_specs=in_specs,
        out_specs=pl.BlockSpec((tqb, H_A * V_DIM), lambda b, qi, ki: (b * nqb + qi, 0)),
        out_shape=jax.ShapeDtypeStruct((q_rows, H_A * V_DIM), bf16),
        scratch_shapes=[pltpu.VMEM((nqs, H_A, tq), f32), pltpu.VMEM((nqs, H_A, tq), f32),
                        pltpu.VMEM((nqs, H_A * V_DIM, tq), f32),
                        pltpu.VMEM((H_A, kc, tq), f32), pltpu.VMEM((H_A, kc, tq), f32)],
        compiler_params=_params(("parallel", "parallel", "arbitrary")), name="attention",
    )(qT, k, vT, k_tail, vT_tail)


def _mlstm_gates(gT, *, reverse):
    r = lax.broadcasted_iota(jnp.int32, (CHUNK, CHUNK), 0)
    c = lax.broadcasted_iota(jnp.int32, (CHUNK, CHUNK), 1)
    tri = jnp.where((r >= c) if reverse else (r <= c), 1.0, 0.0).astype(bf16)
    a = gT.astype(bf16)
    rem = gT - a.astype(f32)
    b = rem.astype(bf16)
    x = _dot(a, tri) + _dot(b, tri) + _dot((rem - b.astype(f32)).astype(bf16), tri)
    padded = jnp.concatenate([x, jnp.zeros((LANES - x.shape[0], CHUNK), f32)], axis=0)
    return padded.T, x


def _mlstm_head(h, qk, cums, q, kT, v, gT, c_sc, m_sc, *, reverse):
    d = 1 if reverse else 0
    bcol_all, brow_all = cums
    r = lax.broadcasted_iota(jnp.int32, (CHUNK, CHUNK), 0)
    c = lax.broadcasted_iota(jnp.int32, (CHUNK, CHUNK), 1)
    mask = (c >= r) if reverse else (c <= r)
    ones_blk = jnp.ones((CHUNK, LANES), bf16)
    tile_shape = (SUBLANES, LANES)
    ji = d * H_B + h
    jf = 2 * H_B + d * H_B + h
    bcb = jnp.broadcast_to(bcol_all[:, jf:jf + 1], (CHUNK, LANES))
    br = brow_all[jf:jf + 1, :]
    a_row = gT[ji:ji + 1, :] - br
    gtot = jnp.broadcast_to(br[:, 0:1] if reverse else br[:, CHUNK - 1:CHUNK], tile_shape)
    m_prev = m_sc[h]
    mpb = jnp.concatenate([m_prev] * (CHUNK // SUBLANES), axis=0)
    qh = q[:, h * DH_B:(h + 1) * DH_B]
    kTh = kT[h * DH_B:(h + 1) * DH_B, :]
    v_aug = jnp.concatenate([v[:, h * DH_B:(h + 1) * DH_B], ones_blk], axis=1)
    amask = jnp.where(mask, a_row, -jnp.inf)
    mt = jnp.maximum(jnp.broadcast_to(jnp.max(amask, axis=1, keepdims=True), (CHUNK, LANES)), mpb)
    s = (qk * jnp.exp(amask - mt)).astype(bf16)
    e_in = jnp.exp(mpb - mt)
    c_prev = c_sc[h]
    numden = _dot(s, v_aug) + jnp.concatenate([e_in, e_in], axis=1) * _dot(qh, c_prev.astype(bf16))
    den = numden[:, DH_B:]
    out = numden[:, :DH_B] * (1.0 / jnp.maximum(jnp.abs(den), jnp.exp(-(bcb + mt))))
    a_end = gtot[0:1, :] + a_row
    m_loc = jnp.broadcast_to(jnp.max(a_end, axis=1, keepdims=True), tile_shape)
    m_new = jnp.maximum(gtot + m_prev, m_loc)
    sp = jnp.exp(gtot + m_prev - m_new)
    kw = (kTh.astype(f32) * jnp.exp(a_end - m_new[0:1, :])).astype(bf16)
    rep = lambda x: jnp.concatenate([jnp.concatenate([x] * (DH_B // SUBLANES), axis=0)] * 2, axis=1)
    c_sc[h] = rep(sp) * c_prev + _dot(kw, v_aug)
    m_sc[h] = m_new
    return out


def _mlstm_chunks(loaders, stores, c_sc, m_sc, *, reverse):
    qk_head = lambda q, kT, h: _dot(q[:, h * DH_B:(h + 1) * DH_B], kT[h * DH_B:(h + 1) * DH_B, :])
    cur = loaders[0]()
    cums = _mlstm_gates(cur[3], reverse=reverse)
    qks = [qk_head(cur[0], cur[1], h) for h in range(H_B)]
    for i, store in enumerate(stores):
        q, kT, v, gT = cur
        nxt = loaders[i + 1]() if i + 1 < len(loaders) else None
        cums_n = _mlstm_gates(nxt[3], reverse=reverse) if nxt is not None else None
        qks_n, outs = [], []
        for h in range(H_B):
            if nxt is not None:
                qks_n.append(qk_head(nxt[0], nxt[1], h))
            outs.append(_mlstm_head(h, qks[h], cums, q, kT, v, gT, c_sc, m_sc, reverse=reverse))
        store(jnp.concatenate(outs, axis=1))
        cur, cums, qks = nxt, cums_n, qks_n


def _mlstm_kernel(q_ref, kT_ref, v_ref, gT_ref, qt_ref, kTt_ref, vt_ref, gTt_ref, *rest, reverse):
    if reverse:
        hf_ref, hft_ref, o_ref, ot_ref, c_sc, m_sc = rest
    else:
        o_ref, ot_ref, c_sc, m_sc = rest
    j = pl.program_id(1)
    nj = pl.num_programs(1)
    nch = kT_ref.shape[0]
    run = functools.partial(_mlstm_chunks, c_sc=c_sc, m_sc=m_sc, reverse=reverse)

    def tail_chunk():
        def store(o):
            ot_ref[...] = ((o + hft_ref[...].astype(f32)) if reverse else o).astype(bf16)
        run([lambda: (qt_ref[...], kTt_ref[0], vt_ref[...], gTt_ref[0])], [store])

    @pl.when(j == 0)
    def _():
        c_sc[...] = jnp.zeros(c_sc.shape, f32)
        m_sc[...] = jnp.full(m_sc.shape, NEG, f32)
        if not reverse:
            tail_chunk()

    def loader(c):
        rows = slice(c * CHUNK, (c + 1) * CHUNK)
        return lambda: (q_ref[rows, :], kT_ref[c], v_ref[rows, :], gT_ref[c])

    def storer(c):
        rows = slice(c * CHUNK, (c + 1) * CHUNK)

        def store(o):
            o_ref[rows, :] = ((o + hf_ref[rows, :].astype(f32)) if reverse else o).astype(bf16)
        return store

    order = list(range(nch - 1, -1, -1) if reverse else range(nch))
    run([loader(c) for c in order], [storer(c) for c in order])

    if reverse:
        @pl.when(j == nj - 1)
        def _():
            tail_chunk()


def _mlstm(x_arrs, t_arrs, fwd_out=None, *, batch, seq, rb):
    reverse = fwd_out is not None
    nj = seq // rb
    nch = rb // CHUNK
    blk = (lambda b, j: b * nj + (nj - 1 - j)) if reverse else (lambda b, j: b * nj + j)
    in_specs = [
        pl.BlockSpec((rb, H_B * DH_B), lambda b, j: (blk(b, j), 0)),
        pl.BlockSpec((nch, H_B * DH_B, CHUNK), lambda b, j: (blk(b, j), 0, 0)),
        pl.BlockSpec((rb, H_B * DH_B), lambda b, j: (blk(b, j), 0)),
        pl.BlockSpec((nch, 4 * H_B, CHUNK), lambda b, j: (blk(b, j), 0, 0)),
        pl.BlockSpec((TAIL, H_B * DH_B), lambda b, j: (b, 0)),
        pl.BlockSpec((1, H_B * DH_B, CHUNK), lambda b, j: (b, 0, 0)),
        pl.BlockSpec((TAIL, H_B * DH_B), lambda b, j: (b, 0)),
        pl.BlockSpec((1, 4 * H_B, CHUNK), lambda b, j: (b, 0, 0)),
    ]
    out_specs = (pl.BlockSpec((rb, H_B * DH_B), lambda b, j: (blk(b, j), 0)),
                 pl.BlockSpec((TAIL, H_B * DH_B), lambda b, j: (b, 0)))
    out_shape = (jax.ShapeDtypeStruct((batch * seq, H_B * DH_B), bf16),
                 jax.ShapeDtypeStruct((batch * TAIL, H_B * DH_B), bf16))
    operands = tuple(x_arrs) + tuple(t_arrs)
    if reverse:
        in_specs = in_specs + list(out_specs)
        operands = operands + tuple(fwd_out)
    return pl.pallas_call(
        functools.partial(_mlstm_kernel, reverse=reverse),
        grid=(batch, nj), in_specs=in_specs, out_specs=out_specs, out_shape=out_shape,
        scratch_shapes=[pltpu.VMEM((H_B, DH_B, 2 * DH_B), f32), pltpu.VMEM((H_B, SUBLANES, LANES), f32)],
        compiler_params=_params(("parallel", "arbitrary")), name="mlstm_bwd" if reverse else "mlstm_fwd",
    )(*operands)


def _merge_kernel(attn_ref, hm_ref, so_ref, sga_ref, sgb_ref, res_ref, mn_ref, woa_ref, wom_ref, wout_ref, o_ref):
    hm = hm_ref[...].astype(f32)
    mn = mn_ref[...]
    parts = [_rms(hm[:, h * DH_B:(h + 1) * DH_B], mn[:, h * DH_B:(h + 1) * DH_B]) for h in range(H_B)]
    hmn = (jnp.concatenate(parts, axis=1) * so_ref[...].astype(f32)).astype(bf16)
    y = (sga_ref[...].astype(f32) * _dot(attn_ref[...], woa_ref[...])
         + sgb_ref[...].astype(f32) * _dot(hmn, wom_ref[...]))
    o_ref[...] = res_ref[...] + _dot(y.astype(bf16), wout_ref[...])


def _merge(attn, hm, so, sga, sgb, res, w, *, tm):
    rows = attn.shape[0]
    row_spec = lambda width: pl.BlockSpec((tm, width), lambda i: (i, 0))
    in_specs = [row_spec(H_A * V_DIM), row_spec(H_B * DH_B), row_spec(H_B * DH_B),
                row_spec(D_MODEL), row_spec(D_MODEL), row_spec(D_MODEL),
                _const_spec((1, H_B * DH_B)), _const_spec(w["woa"].shape), _const_spec(w["wom"].shape),
                _const_spec(w["wout"].shape)]
    return pl.pallas_call(
        _merge_kernel, grid=(rows // tm,), in_specs=in_specs, out_specs=row_spec(D_MODEL),
        out_shape=jax.ShapeDtypeStruct((rows, D_MODEL), f32),
        compiler_params=_params(("parallel",)), name="merge",
    )(attn, hm, so, sga, sgb, res, w["mlstm_norm"], w["woa"], w["wom"], w["wout"])


def _gelu_tanh(x):
    return 0.5 * x * (1.0 + jnp.tanh(math.sqrt(2.0 / math.pi) * (x + 0.044715 * (x * x * x))))


def _ffn_kernel(h_ref, prev_ref, prevt_ref, next_ref, nf_ref, wup_ref, wgate_ref, cw_ref, cb_ref, wdown_ref,
                nfin_ref, o_ref, *, tiles_per_batch):
    i = pl.program_id(0)
    tm = h_ref.shape[0]
    first = (i % tiles_per_batch) == 0
    last = (i % tiles_per_batch) == tiles_per_batch - 1
    nf = nf_ref[...]
    h = h_ref[...]
    prev = jnp.where(first, prevt_ref[...], prev_ref[...])
    u_ext = _rms(jnp.concatenate([prev, h, next_ref[...]], axis=0), nf)
    u = u_ext[SUBLANES:tm + SUBLANES, :].astype(bf16)
    u_ext = u_ext.astype(bf16)
    row = lax.broadcasted_iota(jnp.int32, (tm, 1), 0)

    def up_gate(lo, hi):
        return _dot(u_ext, wup_ref[:, lo:hi]), _dot(u, wgate_ref[:, lo:hi])

    acc = h
    ahead = [up_gate(*FF_CHUNKS[j]) for j in range(FF_LOOKAHEAD)]
    for ci, (lo, hi) in enumerate(FF_CHUNKS):
        if ci + FF_LOOKAHEAD < len(FF_CHUNKS):
            ahead.append(up_gate(*FF_CHUNKS[ci + FF_LOOKAHEAD]))
        a_ext, gate = ahead.pop(0)
        a = a_ext[SUBLANES:tm + SUBLANES, :]
        a_prev = a_ext[SUBLANES - 1:SUBLANES, :]
        a_next = jnp.where(last, 0.0, a_ext[tm + SUBLANES:tm + SUBLANES + 1, :])
        a_dn = jnp.where(row == 0, a_prev, pltpu.roll(a, 1, axis=0))
        a_up = jnp.where(row == tm - 1, a_next, pltpu.roll(a, tm - 1, axis=0))
        cw = cw_ref[:, lo:hi]
        conv = a_dn * cw[0:1, :] + a * cw[1:2, :] + a_up * cw[2:3, :] + cb_ref[:, lo:hi]
        gated = (_gelu_tanh(conv) * gate).astype(bf16)
        acc = acc + _dot(gated, wdown_ref[lo:hi, :])
    o_ref[...] = _rms(acc, nfin_ref[...])


def _ffn(h1, h1_tail, w, *, seq, tm):
    rows = h1.shape[0]
    n = rows // tm
    tpb = seq // tm
    hb = tm // SUBLANES
    in_specs = [
        pl.BlockSpec((tm, D_MODEL), lambda i: (i, 0)),
        pl.BlockSpec((SUBLANES, D_MODEL), lambda i: (jnp.maximum(i * hb - 1, 0), 0)),
        pl.BlockSpec((SUBLANES, D_MODEL), lambda i: ((i // tpb) * (TAIL // SUBLANES) + N_META // SUBLANES - 1, 0)),
        pl.BlockSpec((SUBLANES, D_MODEL), lambda i: (jnp.minimum((i + 1) * hb, n * hb - 1), 0)),
        _const_spec((1, D_MODEL)), _const_spec(w["wup"].shape), _const_spec(w["wgate"].shape),
        _const_spec((SUBLANES, D_FF)), _const_spec((1, D_FF)), _const_spec(w["wdown"].shape),
        _const_spec((1, D_MODEL)),
    ]
    return pl.pallas_call(
        functools.partial(_ffn_kernel, tiles_per_batch=tpb),
        grid=(n,), in_specs=in_specs, out_specs=pl.BlockSpec((tm, D_MODEL), lambda i: (i, 0)),
        out_shape=jax.ShapeDtypeStruct((rows, D_MODEL), f32),
        compiler_params=_params(("parallel",)), name="conv_ffn",
    )(h1, h1, h1_tail, h1, w["norm_ffn"], w["wup"], w["wgate"], w["conv_w"], w["conv_b"], w["wdown"],
      w["norm_final"])


def _prep_weights(norm_mix, w_in, q_norm, w_uq, kv_norm, w_ukv, b_igate, b_fgate, mlstm_norm, w_o_attn, w_o_mlstm,
                  w_out, norm_ffn, w_up, w_gate, conv_w, conv_b, w_down, norm_final):
    w = w_in[0]
    offs = [0]
    for s in IN_SIZES:
        offs.append(offs[-1] + s)
    col = lambda j: w[:, offs[j]:offs[j + 1]]
    zeros = lambda r, c: jnp.zeros((r, c), f32)
    wa = jnp.concatenate([col(0), col(1)], axis=1)
    wb = jnp.concatenate([col(3), col(4), col(5), col(6), col(7), col(8), zeros(D_MODEL, NOPE - 4 * H_B), col(2),
                          zeros(D_MODEL, HEAD_PAD - QK_DIM)], axis=1)
    wc = jnp.concatenate([col(9), col(10)], axis=1)
    wuq = w_uq[0].reshape(Q_LORA, H_A, QK_DIM)
    wq = jnp.concatenate([wuq, jnp.zeros((Q_LORA, H_A, HEAD_PAD - QK_DIM), f32)],
                         axis=-1).reshape(Q_LORA, H_A * HEAD_PAD)
    wukv = w_ukv[0].reshape(KV_LORA, H_A, NOPE + V_DIM)
    wk = jnp.concatenate([wukv[:, :, :NOPE], jnp.zeros((KV_LORA, H_A, HEAD_PAD - NOPE), f32)],
                         axis=-1).reshape(KV_LORA, H_A * HEAD_PAD)
    wv = wukv[:, :, NOPE:].reshape(KV_LORA, H_A * V_DIM)
    wkv = jnp.concatenate([wk, wv], axis=1)
    gbias = jnp.concatenate([b_igate[0].reshape(-1), b_fgate[0].reshape(-1), jnp.zeros((LANES - 4 * H_B,), f32)])
    cw = jnp.concatenate([conv_w[0], jnp.zeros((SUBLANES - CONV_W, D_FF), f32)], axis=0)
    return dict(
        norm_mix=norm_mix[0][None], wa=wa.astype(bf16), wb=wb.astype(bf16), wc=wc.astype(bf16),
        q_norm=q_norm[0][None], wq=wq.astype(bf16), kv_norm=kv_norm[0][None], wkv=wkv.astype(bf16),
        gbias=gbias[None], mlstm_norm=mlstm_norm[0][None], woa=w_o_attn[0].astype(bf16),
        wom=w_o_mlstm[0].astype(bf16), wout=w_out[0].astype(bf16), norm_ffn=norm_ffn[0][None],
        wup=w_up[0].astype(bf16), wgate=w_gate[0].astype(bf16), conv_w=cw, conv_b=conv_b[0][None],
        wdown=w_down[0].astype(bf16), norm_final=norm_final[None])


def _rope_tables(positions):
    inv = ROPE_THETA ** (-jnp.arange(0, ROPE, 2, dtype=f32) / ROPE)
    ang = positions.astype(f32)[:, None] * inv[None, :]
    n = positions.shape[0]
    cos, sin = jnp.cos(ang), jnp.sin(ang)
    cm = jnp.concatenate([jnp.ones((n, NOPE), f32), cos, cos, jnp.zeros((n, HEAD_PAD - QK_DIM), f32)], axis=1)
    sm = jnp.concatenate([jnp.zeros((n, NOPE), f32), -sin, sin, jnp.zeros((n, HEAD_PAD - QK_DIM), f32)], axis=1)
    return cm, sm


def _tiles(batch, seq):
    tm = math.gcd(seq, ROW_TILE)
    tm_tail = math.gcd(batch * TAIL, ROW_TILE)
    tkb = math.gcd(seq, 4096)
    tqb = math.gcd(seq, 2048)
    return dict(tm=tm, tm_wide=math.gcd(seq, 1024), tm_tail=tm_tail, tqb=tqb, tkb=tkb, rb=math.gcd(seq, 1024))


def _position_tables(max_seq):
    return _rope_tables(N_META + jnp.arange(max_seq)) + _rope_tables(jnp.tile(jnp.arange(TAIL), ROW_TILE // TAIL))


def _encode(x, meta_tokens, w, tables):
    batch, seq, _ = x.shape
    t = _tiles(batch, seq)
    xf = x.reshape(batch * seq, D_MODEL)
    tail_one = jnp.concatenate([meta_tokens.astype(f32), jnp.zeros((TAIL - N_META, D_MODEL), f32)], axis=0)
    tailf = jnp.tile(tail_one, (batch, 1))

    cm_x, sm_x, cm_t, sm_t = tables
    nper = seq // t["tm"]
    xs = _inproj(xf, cm_x, sm_x, lambda i: (i % nper, 0), w, tm=t["tm"], cq=TQ, ck=KC, mask_tail=False)
    ts = _inproj(tailf, cm_t, sm_t, lambda i: (0, 0), w, tm=t["tm_tail"], cq=TAIL, ck=TAIL, mask_tail=True)
    qT, k, vT, mq, mkT, mv, so, gT, sga, sgb = xs
    qT_t, k_t, vT_t, mq_t, mkT_t, mv_t, so_t, gT_t, sga_t, sgb_t = ts

    attn = _attention(qT, k, vT, k_t, vT_t, batch=batch, seq=seq, tqb=t["tqb"], tkb=t["tkb"], q_is_tail=False)
    attn_t = _attention(qT_t, k, vT, k_t, vT_t, batch=batch, seq=seq, tqb=TAIL, tkb=t["tkb"], q_is_tail=True)

    x_arrs = (mq, mkT, mv, gT)
    t_arrs = (mq_t, mkT_t, mv_t, gT_t)
    fwd = _mlstm(x_arrs, t_arrs, batch=batch, seq=seq, rb=t["rb"])
    hm, hm_t = _mlstm(x_arrs, t_arrs, fwd, batch=batch, seq=seq, rb=t["rb"])

    h1 = _merge(attn, hm, so, sga, sgb, xf, w, tm=t["tm_wide"])
    h1_t = _merge(attn_t, hm_t, so_t, sga_t, sgb_t, tailf, w, tm=t["tm_tail"])
    y = _ffn(h1, h1_t, w, seq=seq, tm=t["tm_wide"])
    return y.reshape(batch, seq, D_MODEL)


def kernel(x_prompt, x_sample, meta_tokens, norm_mix, w_in, q_norm, w_uq, kv_norm, w_ukv, b_igate, b_fgate,
           mlstm_norm, w_o_attn, w_o_mlstm, w_out, norm_ffn, w_up, w_gate, conv_w, conv_b, w_down, norm_final):
    w = _prep_weights(norm_mix, w_in, q_norm, w_uq, kv_norm, w_ukv, b_igate, b_fgate, mlstm_norm, w_o_attn,
                      w_o_mlstm, w_out, norm_ffn, w_up, w_gate, conv_w, conv_b, w_down, norm_final)
    tables = _position_tables(max(x_prompt.shape[1], x_sample.shape[1]))
    return (_encode(x_prompt, meta_tokens, w, tables), _encode(x_sample, meta_tokens, w, tables))
```

```python
import functools
import math

import jax
import jax.numpy as jnp
from jax import lax
from jax.experimental import pallas as pl
from jax.experimental.pallas import tpu as pltpu

D_MODEL = 1024
N_META = 16
H_A = 8
Q_LORA = 256
KV_LORA = 128
NOPE = 64
ROPE = 32
QK_DIM = NOPE + ROPE
V_DIM = 64
ROPE_THETA = 10000.0
H_B = 4
DH_B = 128
D_FF = 2816
CONV_W = 3
EPS = 1e-6
NEG = -1e30
IN_SIZES = (Q_LORA, KV_LORA, ROPE, H_B * DH_B, H_B * DH_B, H_B * DH_B, H_B * DH_B, 2 * H_B, 2 * H_B, D_MODEL, D_MODEL)

LANES = 128
SUBLANES = 8
TAIL = 128
HEAD_PAD = LANES
CHUNK = 128
TQ = 256
KC = 512
ROW_TILE = 512
PV_LAG = 1
VMEM_LIMIT = 56 * 1024 * 1024
QSCALE = (QK_DIM ** -0.5) * math.log2(math.e)
FF_CHUNKS = tuple((lo, min(lo + 512, D_FF)) for lo in range(0, D_FF, 512))
FF_LOOKAHEAD = 2

f32 = jnp.float32
bf16 = jnp.bfloat16


def _rms(x, g):
    return x * lax.rsqrt(jnp.mean(x * x, axis=-1, keepdims=True) + EPS) * g


def _sigmoid(x):
    return 1.0 / (1.0 + jnp.exp(-x))


def _log_sigmoid(x):
    return jnp.minimum(x, 0.0) - jnp.log1p(jnp.exp(-jnp.abs(x)))


def _dot(a, b):
    return jnp.dot(a, b, preferred_element_type=f32)


def _const_spec(shape):
    zeros = (0,) * len(shape)
    return pl.BlockSpec(shape, lambda *_: zeros, pipeline_mode=pl.Buffered(1))


def _params(sem):
    return pltpu.CompilerParams(dimension_semantics=sem, vmem_limit_bytes=VMEM_LIMIT)


def _inproj_kernel(h_ref, cm_ref, sm_ref, nmix_ref, wa_ref, wb_ref, wc_ref, qn_ref, wq_ref, kvn_ref, wkv_ref,
                   gb_ref, qT_ref, k_ref, vT_ref, mq_ref, mkT_ref, mv_ref, so_ref, gT_ref, sga_ref, sgb_ref,
                   *, cq, ck, mask_tail):
    tm = h_ref.shape[0]
    u = _rms(h_ref[...], nmix_ref[...]).astype(bf16)
    cm = cm_ref[...]
    sm = sm_ref[...]

    za = _dot(u, wa_ref[...])
    qn = _rms(za[:, :Q_LORA], qn_ref[...]).astype(bf16)
    kvn = _rms(za[:, Q_LORA:Q_LORA + KV_LORA], kvn_ref[...]).astype(bf16)
    o_kr = Q_LORA + KV_LORA
    kr = za[:, o_kr:o_kr + HEAD_PAD] * cm + za[:, o_kr + HEAD_PAD:o_kr + 2 * HEAD_PAD] * sm
    cm8 = jnp.concatenate([cm] * H_A, axis=1)
    sm8 = jnp.concatenate([sm] * H_A, axis=1)
    q = (_dot(qn, wq_ref[:, :H_A * HEAD_PAD]) * cm8 + _dot(qn, wq_ref[:, H_A * HEAD_PAD:]) * sm8) * QSCALE
    qT = q.T
    for c in range(tm // cq):
        qT_ref[c] = qT[:, c * cq:(c + 1) * cq].astype(bf16)
    kv = _dot(kvn, wkv_ref[...])
    for h in range(H_A):
        kh = (kv[:, h * HEAD_PAD:(h + 1) * HEAD_PAD] + kr).astype(bf16)
        for c in range(tm // ck):
            k_ref[c, h] = kh[c * ck:(c + 1) * ck, :]
    vT = kv[:, H_A * HEAD_PAD:].T
    for c in range(tm // ck):
        vT_ref[c] = vT[:, c * ck:(c + 1) * ck].astype(bf16)

    zb = _dot(u, wb_ref[...])
    wm = H_B * DH_B
    mq_ref[...] = zb[:, 0:wm].astype(bf16)
    mkT = (zb[:, wm:2 * wm] * (DH_B ** -0.5)).T
    for c in range(tm // CHUNK):
        mkT_ref[c] = mkT[:, c * CHUNK:(c + 1) * CHUNK].astype(bf16)
    mv_ref[...] = zb[:, 2 * wm:3 * wm].astype(bf16)
    so_ref[...] = _sigmoid(zb[:, 3 * wm:4 * wm]).astype(bf16)
    gz = zb[:, 4 * wm:4 * wm + LANES] + gb_ref[...]
    lane = lax.broadcasted_iota(jnp.int32, gz.shape, 1)
    gates = jnp.where(lane < 2 * H_B, gz, _log_sigmoid(gz))
    if mask_tail:
        row = lax.broadcasted_iota(jnp.int32, gz.shape, 0)
        pad = (row % TAIL) >= N_META
        gates = jnp.where(pad, jnp.where(lane < 2 * H_B, NEG, 0.0), gates)
    gT = gates.T
    for c in range(tm // CHUNK):
        gT_ref[c] = gT[:4 * H_B, c * CHUNK:(c + 1) * CHUNK]

    zc = _dot(u, wc_ref[...])
    sga_ref[...] = _sigmoid(zc[:, :D_MODEL]).astype(bf16)
    sgb_ref[...] = _sigmoid(zc[:, D_MODEL:]).astype(bf16)


def _inproj(h, cm, sm, tab_map, w, *, tm, cq, ck, mask_tail):
    rows = h.shape[0]
    n = rows // tm
    row_spec = lambda width: pl.BlockSpec((tm, width), lambda i: (i, 0))
    chunk_spec = lambda height, cw: pl.BlockSpec((tm // cw, height, cw), lambda i: (i, 0, 0))
    out_shape = (
        jax.ShapeDtypeStruct((rows // cq, H_A * HEAD_PAD, cq), bf16),
        jax.ShapeDtypeStruct((rows // ck, H_A, ck, HEAD_PAD), bf16),
        jax.ShapeDtypeStruct((rows // ck, H_A * V_DIM, ck), bf16),
        jax.ShapeDtypeStruct((rows, H_B * DH_B), bf16),
        jax.ShapeDtypeStruct((rows // CHUNK, H_B * DH_B, CHUNK), bf16),
        jax.ShapeDtypeStruct((rows, H_B * DH_B), bf16),
        jax.ShapeDtypeStruct((rows, H_B * DH_B), bf16),
        jax.ShapeDtypeStruct((rows // CHUNK, 4 * H_B, CHUNK), f32),
        jax.ShapeDtypeStruct((rows, D_MODEL), bf16),
        jax.ShapeDtypeStruct((rows, D_MODEL), bf16),
    )
    out_specs = (
        chunk_spec(H_A * HEAD_PAD, cq),
        pl.BlockSpec((tm // ck, H_A, ck, HEAD_PAD), lambda i: (i, 0, 0, 0)), chunk_spec(H_A * V_DIM, ck),
        row_spec(H_B * DH_B), chunk_spec(H_B * DH_B, CHUNK), row_spec(H_B * DH_B), row_spec(H_B * DH_B),
        chunk_spec(4 * H_B, CHUNK), row_spec(D_MODEL), row_spec(D_MODEL),
    )
    in_specs = [
        row_spec(D_MODEL),
        pl.BlockSpec((tm, LANES), tab_map), pl.BlockSpec((tm, LANES), tab_map),
        _const_spec((1, D_MODEL)), _const_spec(w["wa"].shape), _const_spec(w["wb"].shape),
        _const_spec(w["wc"].shape), _const_spec((1, Q_LORA)), _const_spec(w["wq"].shape),
        _const_spec((1, KV_LORA)), _const_spec(w["wkv"].shape), _const_spec((1, LANES)),
    ]
    return pl.pallas_call(
        functools.partial(_inproj_kernel, cq=cq, ck=ck, mask_tail=mask_tail),
        grid=(n,), in_specs=in_specs, out_specs=out_specs, out_shape=out_shape,
        compiler_params=_params(("parallel",)), name="inproj",
    )(h, cm, sm, w["norm_mix"], w["wa"], w["wb"], w["wc"], w["q_norm"], w["wq"], w["kv_norm"], w["wkv"], w["gbias"])


def _attn_scores(qT_ref, k_ref, step, s_ref):
    qs, c = step

    def head(h):
        s_ref[h] = _dot(k_ref[c, h], qT_ref[qs, h * HEAD_PAD:(h + 1) * HEAD_PAD, :])
    return head


def _attn_update(vT_ref, step, s_ref, m_sc, l_sc, acc_sc, scores_head):
    qs, c = step
    m_all = m_sc[qs]
    l_all = l_sc[qs]
    kc = s_ref.shape[1]
    half = kc // 2
    ones = jnp.ones((2 * SUBLANES, kc), bf16)
    ms, ls = [], []

    def value_product(h, p, alpha):
        v_aug = jnp.concatenate([vT_ref[c, h * V_DIM:(h + 1) * V_DIM, :], ones], axis=0)
        pv = _dot(v_aug, p)
        ls.append(alpha * l_all[h:h + 1, :] + pv[V_DIM:V_DIM + 1, :])
        acc_sc[qs, h * V_DIM:(h + 1) * V_DIM, :] = (alpha * acc_sc[qs, h * V_DIM:(h + 1) * V_DIM, :]
                                                    + pv[0:V_DIM, :])

    pending = []
    for h in range(H_A):
        if scores_head is not None:
            scores_head(h)
        if len(pending) == PV_LAG:
            value_product(*pending.pop(0))
        m_old = m_all[h:h + 1, :]
        m_new = jnp.maximum(m_old, jnp.max(s_ref[h], axis=0, keepdims=True))
        p = jnp.concatenate([jnp.exp2((s_ref[h, 0:half, :] - m_new).astype(bf16)),
                             jnp.exp2((s_ref[h, half:kc, :] - m_new).astype(bf16))], axis=0)
        ms.append(m_new)
        pending.append((h, p, jnp.exp2(m_old - m_new)))
    for item in pending:
        value_product(*item)
    m_sc[qs] = jnp.concatenate(ms, axis=0)
    l_sc[qs] = jnp.concatenate(ls, axis=0)


def _attn_kernel(qT_ref, k_ref, vT_ref, kt_ref, vtt_ref, o_ref, m_sc, l_sc, acc_sc, s0_sc, s1_sc, *, single_kv_block):
    ki = pl.program_id(2)
    nqs, _, tq = qT_ref.shape
    nkc = vT_ref.shape[0]

    def init():
        zk = jnp.zeros((N_META, HEAD_PAD), bf16)
        k_bd = jnp.concatenate(
            [jnp.concatenate([kt_ref[0, h, 0:N_META, :] if j == h else zk for j in range(H_A)], axis=1)
             for h in range(H_A)], axis=0)
        rr = lax.broadcasted_iota(jnp.int32, (TAIL, H_A * N_META), 0)
        rc = lax.broadcasted_iota(jnp.int32, (TAIL, H_A * N_META), 1)
        rep = jnp.where((rr < N_META) & (rc % N_META == rr), 1.0, 0.0).astype(bf16)
        hv = lax.broadcasted_iota(jnp.int32, (H_A * V_DIM, H_A * N_META), 0) // V_DIM
        cv = lax.broadcasted_iota(jnp.int32, (H_A * V_DIM, H_A * N_META), 1) // N_META
        v_bd = jnp.where(hv == cv, _dot(vtt_ref[0], rep), 0.0).astype(bf16)
        for qs in range(nqs):
            s = _dot(k_bd, qT_ref[qs])
            ms, ls, ps = [], [], []
            for h in range(H_A):
                sh = s[h * N_META:(h + 1) * N_META, :]
                m = jnp.max(sh, axis=0, keepdims=True)
                p = jnp.exp2(sh - m)
                ms.append(m)
                ls.append(jnp.sum(p, axis=0, keepdims=True))
                ps.append(p)
            m_sc[qs] = jnp.concatenate(ms, axis=0)
            l_sc[qs] = jnp.concatenate(ls, axis=0)
            acc_sc[qs] = _dot(v_bd, jnp.concatenate(ps, axis=0).astype(bf16))

    def finalize():
        for qs in range(nqs):
            inv = 1.0 / l_sc[qs]
            outs = [acc_sc[qs, h * V_DIM:(h + 1) * V_DIM, :] * inv[h:h + 1, :] for h in range(H_A)]
            o = jnp.concatenate(outs, axis=0)
            o_ref[qs * tq:(qs + 1) * tq, :] = o.T.astype(bf16)

    if single_kv_block:
        init()
    else:
        pl.when(ki == 0)(init)

    first = _attn_scores(qT_ref, k_ref, (0, 0), s0_sc)
    for h in range(H_A):
        first(h)

    bufs = (s0_sc, s1_sc)

    def group(qs, last):
        for c in range(nkc):
            nxt_step = (qs, c + 1) if c + 1 < nkc else (qs + 1, 0)
            nxt = None if (last and c == nkc - 1) else _attn_scores(qT_ref, k_ref, nxt_step, bufs[(c + 1) % 2])
            _attn_update(vT_ref, (qs, c), bufs[c % 2], m_sc, l_sc, acc_sc, nxt)

    def body(qs, carry):
        group(qs, False)
        return carry
    lax.fori_loop(0, nqs - 1, body, 0)
    group(nqs - 1, True)

    if single_kv_block:
        finalize()
    else:
        pl.when(ki == pl.num_programs(2) - 1)(finalize)


def _attention(qT, k, vT, k_tail, vT_tail, *, batch, seq, tqb, tkb, q_is_tail):
    tq = qT.shape[2]
    kc = vT.shape[2]
    nqb = (TAIL if q_is_tail else seq) // tqb
    nkb = seq // tkb
    nqs = tqb // tq
    assert (tkb // kc) % 2 == 0, "score buffers alternate by key-chunk parity"
    q_rows = qT.shape[0] * tq
    in_specs = [
        pl.BlockSpec((nqs, H_A * HEAD_PAD, tq), lambda b, qi, ki: (b * nqb + qi, 0, 0)),
        pl.BlockSpec((tkb // kc, H_A, kc, HEAD_PAD), lambda b, qi, ki: (b * nkb + ki, 0, 0, 0)),
        pl.BlockSpec((tkb // kc, H_A * V_DIM, kc), lambda b, qi, ki: (b * nkb + ki, 0, 0)),
        pl.BlockSpec((1, H_A, TAIL, HEAD_PAD), lambda b, qi, ki: (b, 0, 0, 0)),
        pl.BlockSpec((1, H_A * V_DIM, TAIL), lambda b, qi, ki: (b, 0, 0)),
    ]
    return pl.pallas_call(
        functools.partial(_attn_kernel, single_kv_block=(nkb == 1)),
        grid=(batch, nqb, nkb), in_specs=in_specs,
        out_specs=pl.BlockSpec((tqb, H_A * V_DIM), lambda b, qi, ki: (b * nqb + qi, 0)),
        out_shape=jax.ShapeDtypeStruct((q_rows, H_A * V_DIM), bf16),
        scratch_shapes=[pltpu.VMEM((nqs, H_A, tq), f32), pltpu.VMEM((nqs, H_A, tq), f32),
                        pltpu.VMEM((nqs, H_A * V_DIM, tq), f32),
                        pltpu.VMEM((H_A, kc, tq), f32), pltpu.VMEM((H_A, kc, tq), f32)],
        compiler_params=_params(("parallel", "parallel", "arbitrary")), name="attention",
    )(qT, k, vT, k_tail, vT_tail)


def _mlstm_gates(gT, *, reverse):
    r = lax.broadcasted_iota(jnp.int32, (CHUNK, CHUNK), 0)
    c = lax.broadcasted_iota(jnp.int32, (CHUNK, CHUNK), 1)
    tri = jnp.where((r >= c) if reverse else (r <= c), 1.0, 0.0).astype(bf16)
    a = gT.astype(bf16)
    rem = gT - a.astype(f32)
    b = rem.astype(bf16)
    x = _dot(a, tri) + _dot(b, tri) + _dot((rem - b.astype(f32)).astype(bf16), tri)
    padded = jnp.concatenate([x, jnp.zeros((LANES - x.shape[0], CHUNK), f32)], axis=0)
    return padded.T, x


def _mlstm_head(h, qk, cums, q, kT, v, gT, c_sc, m_sc, *, reverse):
    d = 1 if reverse else 0
    bcol_all, brow_all = cums
    r = lax.broadcasted_iota(jnp.int32, (CHUNK, CHUNK), 0)
    c = lax.broadcasted_iota(jnp.int32, (CHUNK, CHUNK), 1)
    mask = (c >= r) if reverse else (c <= r)
    ones_blk = jnp.ones((CHUNK, LANES), bf16)
    tile_shape = (SUBLANES, LANES)
    ji = d * H_B + h
    jf = 2 * H_B + d * H_B + h
    bcb = jnp.broadcast_to(bcol_all[:, jf:jf + 1], (CHUNK, LANES))
    br = brow_all[jf:jf + 1, :]
    a_row = gT[ji:ji + 1, :] - br
    gtot = jnp.broadcast_to(br[:, 0:1] if reverse else br[:, CHUNK - 1:CHUNK], tile_shape)
    m_prev = m_sc[h]
    mpb = jnp.concatenate([m_prev] * (CHUNK // SUBLANES), axis=0)
    qh = q[:, h * DH_B:(h + 1) * DH_B]
    kTh = kT[h * DH_B:(h + 1) * DH_B, :]
    v_aug = jnp.concatenate([v[:, h * DH_B:(h + 1) * DH_B], ones_blk], axis=1)
    amask = jnp.where(mask, a_row, -jnp.inf)
    mt = jnp.maximum(jnp.broadcast_to(jnp.max(amask, axis=1, keepdims=True), (CHUNK, LANES)), mpb)
    s = (qk * jnp.exp(amask - mt)).astype(bf16)
    e_in = jnp.exp(mpb - mt)
    c_prev = c_sc[h]
    numden = _dot(s, v_aug) + jnp.concatenate([e_in, e_in], axis=1) * _dot(qh, c_prev.astype(bf16))
    den = numden[:, DH_B:]
    out = numden[:, :DH_B] * (1.0 / jnp.maximum(jnp.abs(den), jnp.exp(-(bcb + mt))))
    a_end = gtot[0:1, :] + a_row
    m_loc = jnp.broadcast_to(jnp.max(a_end, axis=1, keepdims=True), tile_shape)
    m_new = jnp.maximum(gtot + m_prev, m_loc)
    sp = jnp.exp(gtot + m_prev - m_new)
    kw = (kTh.astype(f32) * jnp.exp(a_end - m_new[0:1, :])).astype(bf16)
    rep = lambda x: jnp.concatenate([jnp.concatenate([x] * (DH_B // SUBLANES), axis=0)] * 2, axis=1)
    c_sc[h] = rep(sp) * c_prev + _dot(kw, v_aug)
    m_sc[h] = m_new
    return out


def _mlstm_chunks(loaders, stores, c_sc, m_sc, *, reverse):
    qk_head = lambda q, kT, h: _dot(q[:, h * DH_B:(h + 1) * DH_B], kT[h * DH_B:(h + 1) * DH_B, :])
    cur = loaders[0]()
    cums = _mlstm_gates(cur[3], reverse=reverse)
    qks = [qk_head(cur[0], cur[1], h) for h in range(H_B)]
    for i, store in enumerate(stores):
        q, kT, v, gT = cur
        nxt = loaders[i + 1]() if i + 1 < len(loaders) else None
        cums_n = _mlstm_gates(nxt[3], reverse=reverse) if nxt is not None else None
        qks_n, outs = [], []
        for h in range(H_B):
            if nxt is not None:
                qks_n.append(qk_head(nxt[0], nxt[1], h))
            outs.append(_mlstm_head(h, qks[h], cums, q, kT, v, gT, c_sc, m_sc, reverse=reverse))
        store(jnp.concatenate(outs, axis=1))
        cur, cums, qks = nxt, cums_n, qks_n


def _mlstm_kernel(q_ref, kT_ref, v_ref, gT_ref, qt_ref, kTt_ref, vt_ref, gTt_ref, *rest, reverse):
    if reverse:
        hf_ref, hft_ref, o_ref, ot_ref, c_sc, m_sc = rest
    else:
        o_ref, ot_ref, c_sc, m_sc = rest
    j = pl.program_id(1)
    nj = pl.num_programs(1)
    nch = kT_ref.shape[0]
    run = functools.partial(_mlstm_chunks, c_sc=c_sc, m_sc=m_sc, reverse=reverse)

    def tail_chunk():
        def store(o):
            ot_ref[...] = ((o + hft_ref[...].astype(f32)) if reverse else o).astype(bf16)
        run([lambda: (qt_ref[...], kTt_ref[0], vt_ref[...], gTt_ref[0])], [store])

    @pl.when(j == 0)
    def _():
        c_sc[...] = jnp.zeros(c_sc.shape, f32)
        m_sc[...] = jnp.full(m_sc.shape, NEG, f32)
        if not reverse:
            tail_chunk()

    def loader(c):
        rows = slice(c * CHUNK, (c + 1) * CHUNK)
        return lambda: (q_ref[rows, :], kT_ref[c], v_ref[rows, :], gT_ref[c])

    def storer(c):
        rows = slice(c * CHUNK, (c + 1) * CHUNK)

        def store(o):
            o_ref[rows, :] = ((o + hf_ref[rows, :].astype(f32)) if reverse else o).astype(bf16)
        return store

    order = list(range(nch - 1, -1, -1) if reverse else range(nch))
    run([loader(c) for c in order], [storer(c) for c in order])

    if reverse:
        @pl.when(j == nj - 1)
        def _():
            tail_chunk()


def _mlstm(x_arrs, t_arrs, fwd_out=None, *, batch, seq, rb):
    reverse = fwd_out is not None
    nj = seq // rb
    nch = rb // CHUNK
    blk = (lambda b, j: b * nj + (nj - 1 - j)) if reverse else (lambda b, j: b * nj + j)
    in_specs = [
        pl.BlockSpec((rb, H_B * DH_B), lambda b, j: (blk(b, j), 0)),
        pl.BlockSpec((nch, H_B * DH_B, CHUNK), lambda b, j: (blk(b, j), 0, 0)),
        pl.BlockSpec((rb, H_B * DH_B), lambda b, j: (blk(b, j), 0)),
        pl.BlockSpec((nch, 4 * H_B, CHUNK), lambda b, j: (blk(b, j), 0, 0)),
        pl.BlockSpec((TAIL, H_B * DH_B), lambda b, j: (b, 0)),
        pl.BlockSpec((1, H_B * DH_B, CHUNK), lambda b, j: (b, 0, 0)),
        pl.BlockSpec((TAIL, H_B * DH_B), lambda b, j: (b, 0)),
        pl.BlockSpec((1, 4 * H_B, CHUNK), lambda b, j: (b, 0, 0)),
    ]
    out_specs = (pl.BlockSpec((rb, H_B * DH_B), lambda b, j: (blk(b, j), 0)),
                 pl.BlockSpec((TAIL, H_B * DH_B), lambda b, j: (b, 0)))
    out_shape = (jax.ShapeDtypeStruct((batch * seq, H_B * DH_B), bf16),
                 jax.ShapeDtypeStruct((batch * TAIL, H_B * DH_B), bf16))
    operands = tuple(x_arrs) + tuple(t_arrs)
    if reverse:
        in_specs = in_specs + list(out_specs)
        operands = operands + tuple(fwd_out)
    return pl.pallas_call(
        functools.partial(_mlstm_kernel, reverse=reverse),
        grid=(batch, nj), in_specs=in_specs, out_specs=out_specs, out_shape=out_shape,
        scratch_shapes=[pltpu.VMEM((H_B, DH_B, 2 * DH_B), f32), pltpu.VMEM((H_B, SUBLANES, LANES), f32)],
        compiler_params=_params(("parallel", "arbitrary")), name="mlstm_bwd" if reverse else "mlstm_fwd",
    )(*operands)


def _merge_kernel(attn_ref, hm_ref, so_ref, sga_ref, sgb_ref, res_ref, mn_ref, woa_ref, wom_ref, wout_ref, o_ref):
    hm = hm_ref[...].astype(f32)
    mn = mn_ref[...]
    parts = [_rms(hm[:, h * DH_B:(h + 1) * DH_B], mn[:, h * DH_B:(h + 1) * DH_B]) for h in range(H_B)]
    hmn = (jnp.concatenate(parts, axis=1) * so_ref[...].astype(f32)).astype(bf16)
    y = (sga_ref[...].astype(f32) * _dot(attn_ref[...], woa_ref[...])
         + sgb_ref[...].astype(f32) * _dot(hmn, wom_ref[...]))
    o_ref[...] = res_ref[...] + _dot(y.astype(bf16), wout_ref[...])


def _merge(attn, hm, so, sga, sgb, res, w, *, tm):
    rows = attn.shape[0]
    row_spec = lambda width: pl.BlockSpec((tm, width), lambda i: (i, 0))
    in_specs = [row_spec(H_A * V_DIM), row_spec(H_B * DH_B), row_spec(H_B * DH_B),
                row_spec(D_MODEL), row_spec(D_MODEL), row_spec(D_MODEL),
                _const_spec((1, H_B * DH_B)), _const_spec(w["woa"].shape), _const_spec(w["wom"].shape),
                _const_spec(w["wout"].shape)]
    return pl.pallas_call(
        _merge_kernel, grid=(rows // tm,), in_specs=in_specs, out_specs=row_spec(D_MODEL),
        out_shape=jax.ShapeDtypeStruct((rows, D_MODEL), f32),
        compiler_params=_params(("parallel",)), name="merge",
    )(attn, hm, so, sga, sgb, res, w["mlstm_norm"], w["woa"], w["wom"], w["wout"])


def _gelu_tanh(x):
    return 0.5 * x * (1.0 + jnp.tanh(math.sqrt(2.0 / math.pi) * (x + 0.044715 * (x * x * x))))


def _ffn_kernel(h_ref, prev_ref, prevt_ref, next_ref, nf_ref, wup_ref, wgate_ref, cw_ref, cb_ref, wdown_ref,
                nfin_ref, o_ref, *, tiles_per_batch):
    i = pl.program_id(0)
    tm = h_ref.shape[0]
    first = (i % tiles_per_batch) == 0
    last = (i % tiles_per_batch) == tiles_per_batch - 1
    nf = nf_ref[...]
    h = h_ref[...]
    prev = jnp.where(first, prevt_ref[...], prev_ref[...])
    u_ext = _rms(jnp.concatenate([prev, h, next_ref[...]], axis=0), nf)
    u = u_ext[SUBLANES:tm + SUBLANES, :].astype(bf16)
    u_ext = u_ext.astype(bf16)
    row = lax.broadcasted_iota(jnp.int32, (tm, 1), 0)

    def up_gate(lo, hi):
        return _dot(u_ext, wup_ref[:, lo:hi]), _dot(u, wgate_ref[:, lo:hi])

    acc = h
    ahead = [up_gate(*FF_CHUNKS[j]) for j in range(FF_LOOKAHEAD)]
    for ci, (lo, hi) in enumerate(FF_CHUNKS):
        if ci + FF_LOOKAHEAD < len(FF_CHUNKS):
            ahead.append(up_gate(*FF_CHUNKS[ci + FF_LOOKAHEAD]))
        a_ext, gate = ahead.pop(0)
        a = a_ext[SUBLANES:tm + SUBLANES, :]
        a_prev = a_ext[SUBLANES - 1:SUBLANES, :]
        a_next = jnp.where(last, 0.0, a_ext[tm + SUBLANES:tm + SUBLANES + 1, :])
        a_dn = jnp.where(row == 0, a_prev, pltpu.roll(a, 1, axis=0))
        a_up = jnp.where(row == tm - 1, a_next, pltpu.roll(a, tm - 1, axis=0))
        cw = cw_ref[:, lo:hi]
        conv = a_dn * cw[0:1, :] + a * cw[1:2, :] + a_up * cw[2:3, :] + cb_ref[:, lo:hi]
        gated = (_gelu_tanh(conv) * gate).astype(bf16)
        acc = acc + _dot(gated, wdown_ref[lo:hi, :])
    o_ref[...] = _rms(acc, nfin_ref[...])


def _ffn(h1, h1_tail, w, *, seq, tm):
    rows = h1.shape[0]
    n = rows // tm
    tpb = seq // tm
    hb = tm // SUBLANES
    in_specs = [
        pl.BlockSpec((tm, D_MODEL), lambda i: (i, 0)),
        pl.BlockSpec((SUBLANES, D_MODEL), lambda i: (jnp.maximum(i * hb - 1, 0), 0)),
        pl.BlockSpec((SUBLANES, D_MODEL), lambda i: ((i // tpb) * (TAIL // SUBLANES) + N_META // SUBLANES - 1, 0)),
        pl.BlockSpec((SUBLANES, D_MODEL), lambda i: (jnp.minimum((i + 1) * hb, n * hb - 1), 0)),
        _const_spec((1, D_MODEL)), _const_spec(w["wup"].shape), _const_spec(w["wgate"].shape),
        _const_spec((SUBLANES, D_FF)), _const_spec((1, D_FF)), _const_spec(w["wdown"].shape),
        _const_spec((1, D_MODEL)),
    ]
    return pl.pallas_call(
        functools.partial(_ffn_kernel, tiles_per_batch=tpb),
        grid=(n,), in_specs=in_specs, out_specs=pl.BlockSpec((tm, D_MODEL), lambda i: (i, 0)),
        out_shape=jax.ShapeDtypeStruct((rows, D_MODEL), f32),
        compiler_params=_params(("parallel",)), name="conv_ffn",
    )(h1, h1, h1_tail, h1, w["norm_ffn"], w["wup"], w["wgate"], w["conv_w"], w["conv_b"], w["wdown"],
      w["norm_final"])


def _prep_weights(norm_mix, w_in, q_norm, w_uq, kv_norm, w_ukv, b_igate, b_fgate, mlstm_norm, w_o_attn, w_o_mlstm,
                  w_out, norm_ffn, w_up, w_gate, conv_w, conv_b, w_down, norm_final):
    w = w_in[0]
    offs = [0]
    for s in IN_SIZES:
        offs.append(offs[-1] + s)
    col = lambda j: w[:, offs[j]:offs[j + 1]]
    zeros = lambda r, c: jnp.zeros((r, c), f32)
    w_kr = col(2)
    half = ROPE // 2
    kr_main = jnp.concatenate([zeros(D_MODEL, NOPE), w_kr, zeros(D_MODEL, HEAD_PAD - QK_DIM)], axis=1)
    kr_rot = jnp.concatenate([zeros(D_MODEL, NOPE), -w_kr[:, half:], w_kr[:, :half],
                              zeros(D_MODEL, HEAD_PAD - QK_DIM)], axis=1)
    wa = jnp.concatenate([col(0), col(1), kr_main, kr_rot], axis=1)
    wb = jnp.concatenate([col(3), col(4), col(5), col(6), col(7), col(8), zeros(D_MODEL, LANES - 4 * H_B)], axis=1)
    wc = jnp.concatenate([col(9), col(10)], axis=1)
    wuq = w_uq[0].reshape(Q_LORA, H_A, QK_DIM)
    nope, rope = wuq[:, :, :NOPE], wuq[:, :, NOPE:]
    zq = lambda c: jnp.zeros((Q_LORA, H_A, c), f32)
    q_main = jnp.concatenate([nope, rope, zq(HEAD_PAD - QK_DIM)], axis=-1).reshape(Q_LORA, H_A * HEAD_PAD)
    q_rot = jnp.concatenate([zq(NOPE), -rope[:, :, half:], rope[:, :, :half], zq(HEAD_PAD - QK_DIM)],
                            axis=-1).reshape(Q_LORA, H_A * HEAD_PAD)
    wq = jnp.concatenate([q_main, q_rot], axis=1)
    wukv = w_ukv[0].reshape(KV_LORA, H_A, NOPE + V_DIM)
    wk = jnp.concatenate([wukv[:, :, :NOPE], jnp.zeros((KV_LORA, H_A, HEAD_PAD - NOPE), f32)],
                         axis=-1).reshape(KV_LORA, H_A * HEAD_PAD)
    wv = wukv[:, :, NOPE:].reshape(KV_LORA, H_A * V_DIM)
    wkv = jnp.concatenate([wk, wv], axis=1)
    gbias = jnp.concatenate([b_igate[0].reshape(-1), b_fgate[0].reshape(-1), jnp.zeros((LANES - 4 * H_B,), f32)])
    cw = jnp.concatenate([conv_w[0], jnp.zeros((SUBLANES - CONV_W, D_FF), f32)], axis=0)
    return dict(
        norm_mix=norm_mix[0][None], wa=wa.astype(bf16), wb=wb.astype(bf16), wc=wc.astype(bf16),
        q_norm=q_norm[0][None], wq=wq.astype(bf16), kv_norm=kv_norm[0][None], wkv=wkv.astype(bf16),
        gbias=gbias[None], mlstm_norm=mlstm_norm[0][None], woa=w_o_attn[0].astype(bf16),
        wom=w_o_mlstm[0].astype(bf16), wout=w_out[0].astype(bf16), norm_ffn=norm_ffn[0][None],
        wup=w_up[0].astype(bf16), wgate=w_gate[0].astype(bf16), conv_w=cw, conv_b=conv_b[0][None],
        wdown=w_down[0].astype(bf16), norm_final=norm_final[None])


def _rope_tables(positions):
    inv = ROPE_THETA ** (-jnp.arange(0, ROPE, 2, dtype=f32) / ROPE)
    ang = positions.astype(f32)[:, None] * inv[None, :]
    n = positions.shape[0]
    cos, sin = jnp.cos(ang), jnp.sin(ang)
    cm = jnp.concatenate([jnp.ones((n, NOPE), f32), cos, cos, jnp.zeros((n, HEAD_PAD - QK_DIM), f32)], axis=1)
    sm = jnp.concatenate([jnp.zeros((n, NOPE), f32), sin, sin, jnp.zeros((n, HEAD_PAD - QK_DIM), f32)], axis=1)
    return cm, sm


def _tiles(batch, seq):
    tm = math.gcd(seq, ROW_TILE)
    tm_tail = math.gcd(batch * TAIL, ROW_TILE)
    tkb = math.gcd(seq, 2048)
    tqb = math.gcd(seq, 2048)
    return dict(tm=tm, tm_wide=math.gcd(seq, 1024), tm_tail=tm_tail, tqb=tqb, tkb=tkb, rb=math.gcd(seq, 1024))


def _position_tables(max_seq):
    return _rope_tables(N_META + jnp.arange(max_seq)) + _rope_tables(jnp.tile(jnp.arange(TAIL), ROW_TILE // TAIL))


def _encode(x, meta_tokens, w, tables):
    batch, seq, _ = x.shape
    t = _tiles(batch, seq)
    xf = x.reshape(batch * seq, D_MODEL)
    tail_one = jnp.concatenate([meta_tokens.astype(f32), jnp.zeros((TAIL - N_META, D_MODEL), f32)], axis=0)
    tailf = jnp.tile(tail_one, (batch, 1))

    cm_x, sm_x, cm_t, sm_t = tables
    nper = seq // t["tm"]
    xs = _inproj(xf, cm_x, sm_x, lambda i: (i % nper, 0), w, tm=t["tm"], cq=TQ, ck=KC, mask_tail=False)
    ts = _inproj(tailf, cm_t, sm_t, lambda i: (0, 0), w, tm=t["tm_tail"], cq=TAIL, ck=TAIL, mask_tail=True)
    qT, k, vT, mq, mkT, mv, so, gT, sga, sgb = xs
    qT_t, k_t, vT_t, mq_t, mkT_t, mv_t, so_t, gT_t, sga_t, sgb_t = ts

    attn = _attention(qT, k, vT, k_t, vT_t, batch=batch, seq=seq, tqb=t["tqb"], tkb=t["tkb"], q_is_tail=False)
    attn_t = _attention(qT_t, k, vT, k_t, vT_t, batch=batch, seq=seq, tqb=TAIL, tkb=t["tkb"], q_is_tail=True)

    x_arrs = (mq, mkT, mv, gT)
    t_arrs = (mq_t, mkT_t, mv_t, gT_t)
    fwd = _mlstm(x_arrs, t_arrs, batch=batch, seq=seq, rb=t["rb"])
    hm, hm_t = _mlstm(x_arrs, t_arrs, fwd, batch=batch, seq=seq, rb=t["rb"])

    h1 = _merge(attn, hm, so, sga, sgb, xf, w, tm=t["tm_wide"])
    h1_t = _merge(attn_t, hm_t, so_t, sga_t, sgb_t, tailf, w, tm=t["tm_tail"])
    y = _ffn(h1, h1_t, w, seq=seq, tm=t["tm_wide"])
    return y.reshape(batch, seq, D_MODEL)


def kernel(x_prompt, x_sample, meta_tokens, norm_mix, w_in, q_norm, w_uq, kv_norm, w_ukv, b_igate, b_fgate,
           mlstm_norm, w_o_attn, w_o_mlstm, w_out, norm_ffn, w_up, w_gate, conv_w, conv_b, w_down, norm_final):
    w = _prep_weights(norm_mix, w_in, q_norm, w_uq, kv_norm, w_ukv, b_igate, b_fgate, mlstm_norm, w_o_attn,
                      w_o_mlstm, w_out, norm_ffn, w_up, w_gate, conv_w, conv_b, w_down, norm_final)
    tables = _position_tables(max(x_prompt.shape[1], x_sample.shape[1]))
    return (_encode(x_prompt, meta_tokens, w, tables), _encode(x_sample, meta_tokens, w, tables))
```

```python
import functools
import math

import jax
import jax.numpy as jnp
from jax import lax
from jax.experimental import pallas as pl
from jax.experimental.pallas import tpu as pltpu

D_MODEL = 1024
N_META = 16
H_A = 8
Q_LORA = 256
KV_LORA = 128
NOPE = 64
ROPE = 32
QK_DIM = NOPE + ROPE
V_DIM = 64
ROPE_THETA = 10000.0
H_B = 4
DH_B = 128
D_FF = 2816
CONV_W = 3
EPS = 1e-6
NEG = -1e30
IN_SIZES = (Q_LORA, KV_LORA, ROPE, H_B * DH_B, H_B * DH_B, H_B * DH_B, H_B * DH_B, 2 * H_B, 2 * H_B, D_MODEL, D_MODEL)

LANES = 128
SUBLANES = 8
TAIL = 128
HEAD_PAD = LANES
CHUNK = 128
TQ = 256
KC = 512
ROW_TILE = 512
PV_LAG = 1
VMEM_LIMIT = 56 * 1024 * 1024
QSCALE = (QK_DIM ** -0.5) * math.log2(math.e)
FF_CHUNKS = tuple((lo, min(lo + 512, D_FF)) for lo in range(0, D_FF, 512))
FF_LOOKAHEAD = 4

f32 = jnp.float32
bf16 = jnp.bfloat16


def _rms(x, g):
    return x * lax.rsqrt(jnp.mean(x * x, axis=-1, keepdims=True) + EPS) * g


def _sigmoid(x):
    return 1.0 / (1.0 + jnp.exp(-x))


def _log_sigmoid(x):
    return jnp.minimum(x, 0.0) - jnp.log1p(jnp.exp(-jnp.abs(x)))


def _dot(a, b):
    return jnp.dot(a, b, preferred_element_type=f32)


def _const_spec(shape):
    zeros = (0,) * len(shape)
    return pl.BlockSpec(shape, lambda *_: zeros, pipeline_mode=pl.Buffered(1))


def _params(sem):
    return pltpu.CompilerParams(dimension_semantics=sem, vmem_limit_bytes=VMEM_LIMIT)


def _inproj_kernel(h_ref, cm_ref, sm_ref, nmix_ref, wa_ref, wb_ref, wc_ref, qn_ref, wq_ref, kvn_ref, wkv_ref,
                   gb_ref, qT_ref, k_ref, vT_ref, mq_ref, mkT_ref, mv_ref, so_ref, gT_ref, sga_ref, sgb_ref,
                   *, cq, ck, mask_tail):
    tm = h_ref.shape[0]
    u = _rms(h_ref[...], nmix_ref[...]).astype(bf16)
    cm = cm_ref[...]
    sm = sm_ref[...]

    za = _dot(u, wa_ref[...])
    qn = _rms(za[:, :Q_LORA], qn_ref[...]).astype(bf16)
    kvn = _rms(za[:, Q_LORA:Q_LORA + KV_LORA], kvn_ref[...]).astype(bf16)
    o_kr = Q_LORA + KV_LORA
    kr = za[:, o_kr:o_kr + HEAD_PAD] * cm + za[:, o_kr + HEAD_PAD:o_kr + 2 * HEAD_PAD] * sm
    cm8 = jnp.concatenate([cm] * H_A, axis=1)
    sm8 = jnp.concatenate([sm] * H_A, axis=1)
    q = (_dot(qn, wq_ref[:, :H_A * HEAD_PAD]) * cm8 + _dot(qn, wq_ref[:, H_A * HEAD_PAD:]) * sm8) * QSCALE
    qT = q.T
    for c in range(tm // cq):
        qT_ref[c] = qT[:, c * cq:(c + 1) * cq].astype(bf16)
    kv = _dot(kvn, wkv_ref[...])
    for h in range(H_A):
        kh = (kv[:, h * HEAD_PAD:(h + 1) * HEAD_PAD] + kr).astype(bf16)
        for c in range(tm // ck):
            k_ref[c, h] = kh[c * ck:(c + 1) * ck, :]
    vT = kv[:, H_A * HEAD_PAD:].T
    for c in range(tm // ck):
        vT_ref[c] = vT[:, c * ck:(c + 1) * ck].astype(bf16)

    zb = _dot(u, wb_ref[...])
    wm = H_B * DH_B
    mq_ref[...] = zb[:, 0:wm].astype(bf16)
    mkT = (zb[:, wm:2 * wm] * (DH_B ** -0.5)).T
    for c in range(tm // CHUNK):
        mkT_ref[c] = mkT[:, c * CHUNK:(c + 1) * CHUNK].astype(bf16)
    mv_ref[...] = zb[:, 2 * wm:3 * wm].astype(bf16)
    so_ref[...] = _sigmoid(zb[:, 3 * wm:4 * wm]).astype(bf16)
    gz = zb[:, 4 * wm:4 * wm + LANES] + gb_ref[...]
    lane = lax.broadcasted_iota(jnp.int32, gz.shape, 1)
    gates = jnp.where(lane < 2 * H_B, gz, _log_sigmoid(gz))
    if mask_tail:
        row = lax.broadcasted_iota(jnp.int32, gz.shape, 0)
        pad = (row % TAIL) >= N_META
        gates = jnp.where(pad, jnp.where(lane < 2 * H_B, NEG, 0.0), gates)
    gT = gates.T
    for c in range(tm // CHUNK):
        gT_ref[c] = gT[:4 * H_B, c * CHUNK:(c + 1) * CHUNK]

    zc = _dot(u, wc_ref[...])
    sga_ref[...] = _sigmoid(zc[:, :D_MODEL]).astype(bf16)
    sgb_ref[...] = _sigmoid(zc[:, D_MODEL:]).astype(bf16)


def _inproj(h, cm, sm, tab_map, w, *, rows, tm, cq, ck, mask_tail):
    n = rows // tm
    row_spec = lambda width: pl.BlockSpec((tm, width), lambda i: (i, 0))
    h_spec = row_spec(D_MODEL) if h.shape[0] == rows else pl.BlockSpec((tm, D_MODEL), lambda i: (0, 0))
    chunk_spec = lambda height, cw: pl.BlockSpec((tm // cw, height, cw), lambda i: (i, 0, 0))
    out_shape = (
        jax.ShapeDtypeStruct((rows // cq, H_A * HEAD_PAD, cq), bf16),
        jax.ShapeDtypeStruct((rows // ck, H_A, ck, HEAD_PAD), bf16),
        jax.ShapeDtypeStruct((rows // ck, H_A * V_DIM, ck), bf16),
        jax.ShapeDtypeStruct((rows, H_B * DH_B), bf16),
        jax.ShapeDtypeStruct((rows // CHUNK, H_B * DH_B, CHUNK), bf16),
        jax.ShapeDtypeStruct((rows, H_B * DH_B), bf16),
        jax.ShapeDtypeStruct((rows, H_B * DH_B), bf16),
        jax.ShapeDtypeStruct((rows // CHUNK, 4 * H_B, CHUNK), f32),
        jax.ShapeDtypeStruct((rows, D_MODEL), bf16),
        jax.ShapeDtypeStruct((rows, D_MODEL), bf16),
    )
    out_specs = (
        chunk_spec(H_A * HEAD_PAD, cq),
        pl.BlockSpec((tm // ck, H_A, ck, HEAD_PAD), lambda i: (i, 0, 0, 0)), chunk_spec(H_A * V_DIM, ck),
        row_spec(H_B * DH_B), chunk_spec(H_B * DH_B, CHUNK), row_spec(H_B * DH_B), row_spec(H_B * DH_B),
        chunk_spec(4 * H_B, CHUNK), row_spec(D_MODEL), row_spec(D_MODEL),
    )
    in_specs = [
        h_spec,
        pl.BlockSpec((tm, LANES), tab_map), pl.BlockSpec((tm, LANES), tab_map),
        _const_spec((1, D_MODEL)), _const_spec(w["wa"].shape), _const_spec(w["wb"].shape),
        _const_spec(w["wc"].shape), _const_spec((1, Q_LORA)), _const_spec(w["wq"].shape),
        _const_spec((1, KV_LORA)), _const_spec(w["wkv"].shape), _const_spec((1, LANES)),
    ]
    return pl.pallas_call(
        functools.partial(_inproj_kernel, cq=cq, ck=ck, mask_tail=mask_tail),
        grid=(n,), in_specs=in_specs, out_specs=out_specs, out_shape=out_shape,
        compiler_params=_params(("parallel",)), name="inproj",
    )(h, cm, sm, w["norm_mix"], w["wa"], w["wb"], w["wc"], w["q_norm"], w["wq"], w["kv_norm"], w["wkv"], w["gbias"])


def _attn_scores(qT_ref, k_ref, step, s_ref):
    qs, c = step

    def head(h):
        s_ref[h] = _dot(k_ref[c, h], qT_ref[qs, h * HEAD_PAD:(h + 1) * HEAD_PAD, :])
    return head


def _attn_update(vT_ref, step, s_ref, m_sc, l_sc, acc_sc, scores_head):
    qs, c = step
    m_all = m_sc[qs]
    l_all = l_sc[qs]
    kc = s_ref.shape[1]
    half = kc // 2
    ones = jnp.ones((2 * SUBLANES, kc), bf16)
    ms, ls = [], []

    def value_product(h, p, alpha):
        v_aug = jnp.concatenate([vT_ref[c, h * V_DIM:(h + 1) * V_DIM, :], ones], axis=0)
        pv = _dot(v_aug, p)
        ls.append(alpha * l_all[h:h + 1, :] + pv[V_DIM:V_DIM + 1, :])
        acc_sc[qs, h * V_DIM:(h + 1) * V_DIM, :] = (alpha * acc_sc[qs, h * V_DIM:(h + 1) * V_DIM, :]
                                                    + pv[0:V_DIM, :])

    pending = []
    for h in range(H_A):
        if scores_head is not None:
            scores_head(h)
        if PV_LAG and len(pending) == PV_LAG:
            value_product(*pending.pop(0))
        m_old = m_all[h:h + 1, :]
        m_new = jnp.maximum(m_old, jnp.max(s_ref[h], axis=0, keepdims=True))
        p = jnp.concatenate([jnp.exp2((s_ref[h, 0:half, :] - m_new).astype(bf16)),
                             jnp.exp2((s_ref[h, half:kc, :] - m_new).astype(bf16))], axis=0)
        ms.append(m_new)
        pending.append((h, p, jnp.exp2(m_old - m_new)))
        if not PV_LAG:
            value_product(*pending.pop(0))
    for item in pending:
        value_product(*item)
    m_sc[qs] = jnp.concatenate(ms, axis=0)
    l_sc[qs] = jnp.concatenate(ls, axis=0)


def _attn_kernel(qT_ref, k_ref, vT_ref, kt_ref, vtt_ref, o_ref, m_sc, l_sc, acc_sc, s0_sc, s1_sc, *, single_kv_block):
    ki = pl.program_id(2)
    nqs, _, tq = qT_ref.shape
    nkc = vT_ref.shape[0]

    def init():
        zk = jnp.zeros((N_META, HEAD_PAD), bf16)
        k_bd = jnp.concatenate(
            [jnp.concatenate([kt_ref[0, h, 0:N_META, :] if j == h else zk for j in range(H_A)], axis=1)
             for h in range(H_A)], axis=0)
        rr = lax.broadcasted_iota(jnp.int32, (TAIL, H_A * N_META), 0)
        rc = lax.broadcasted_iota(jnp.int32, (TAIL, H_A * N_META), 1)
        rep = jnp.where((rr < N_META) & (rc % N_META == rr), 1.0, 0.0).astype(bf16)
        hv = lax.broadcasted_iota(jnp.int32, (H_A * V_DIM, H_A * N_META), 0) // V_DIM
        cv = lax.broadcasted_iota(jnp.int32, (H_A * V_DIM, H_A * N_META), 1) // N_META
        v_bd = jnp.where(hv == cv, _dot(vtt_ref[0], rep), 0.0).astype(bf16)
        for qs in range(nqs):
            s = _dot(k_bd, qT_ref[qs])
            ms, ls, ps = [], [], []
            for h in range(H_A):
                sh = s[h * N_META:(h + 1) * N_META, :]
                m = jnp.max(sh, axis=0, keepdims=True)
                p = jnp.exp2(sh - m)
                ms.append(m)
                ls.append(jnp.sum(p, axis=0, keepdims=True))
                ps.append(p)
            m_sc[qs] = jnp.concatenate(ms, axis=0)
            l_sc[qs] = jnp.concatenate(ls, axis=0)
            acc_sc[qs] = _dot(v_bd, jnp.concatenate(ps, axis=0).astype(bf16))

    def finalize():
        for qs in range(nqs):
            inv = 1.0 / l_sc[qs]
            outs = [acc_sc[qs, h * V_DIM:(h + 1) * V_DIM, :] * inv[h:h + 1, :] for h in range(H_A)]
            o = jnp.concatenate(outs, axis=0)
            o_ref[qs * tq:(qs + 1) * tq, :] = o.T.astype(bf16)

    if single_kv_block:
        init()
    else:
        pl.when(ki == 0)(init)

    first = _attn_scores(qT_ref, k_ref, (0, 0), s0_sc)
    for h in range(H_A):
        first(h)

    bufs = (s0_sc, s1_sc)

    def group(qs, last):
        for c in range(nkc):
            nxt_step = (qs, c + 1) if c + 1 < nkc else (qs + 1, 0)
            nxt = None if (last and c == nkc - 1) else _attn_scores(qT_ref, k_ref, nxt_step, bufs[(c + 1) % 2])
            _attn_update(vT_ref, (qs, c), bufs[c % 2], m_sc, l_sc, acc_sc, nxt)

    def body(qs, carry):
        group(qs, False)
        return carry
    lax.fori_loop(0, nqs - 1, body, 0)
    group(nqs - 1, True)

    if single_kv_block:
        finalize()
    else:
        pl.when(ki == pl.num_programs(2) - 1)(finalize)


def _attention(qT, k, vT, k_tail, vT_tail, *, batch, seq, tqb, tkb, q_is_tail):
    tq = qT.shape[2]
    kc = vT.shape[2]
    nqb = (TAIL if q_is_tail else seq) // tqb
    nkb = seq // tkb
    nqs = tqb // tq
    assert (tkb // kc) % 2 == 0, "score buffers alternate by key-chunk parity"
    q_rows = qT.shape[0] * tq
    in_specs = [
        pl.BlockSpec((nqs, H_A * HEAD_PAD, tq), lambda b, qi, ki: (b * nqb + qi, 0, 0)),
        pl.BlockSpec((tkb // kc, H_A, kc, HEAD_PAD), lambda b, qi, ki: (b * nkb + ki, 0, 0, 0)),
        pl.BlockSpec((tkb // kc, H_A * V_DIM, kc), lambda b, qi, ki: (b * nkb + ki, 0, 0)),
        pl.BlockSpec((1, H_A, TAIL, HEAD_PAD), lambda b, qi, ki: (b, 0, 0, 0)),
        pl.BlockSpec((1, H_A * V_DIM, TAIL), lambda b, qi, ki: (b, 0, 0)),
    ]
    return pl.pallas_call(
        functools.partial(_attn_kernel, single_kv_block=(nkb == 1)),
        grid=(batch, nqb, nkb), in_specs=in_specs,
        out_specs=pl.BlockSpec((tqb, H_A * V_DIM), lambda b, qi, ki: (b * nqb + qi, 0)),
        out_shape=jax.ShapeDtypeStruct((q_rows, H_A * V_DIM), bf16),
        scratch_shapes=[pltpu.VMEM((nqs, H_A, tq), f32), pltpu.VMEM((nqs, H_A, tq), f32),
                        pltpu.VMEM((nqs, H_A * V_DIM, tq), f32),
                        pltpu.VMEM((H_A, kc, tq), f32), pltpu.VMEM((H_A, kc, tq), f32)],
        compiler_params=_params(("parallel", "parallel", "arbitrary")), name="attention",
    )(qT, k, vT, k_tail, vT_tail)


def _mlstm_gates(gT, *, reverse):
    r = lax.broadcasted_iota(jnp.int32, (CHUNK, CHUNK), 0)
    c = lax.broadcasted_iota(jnp.int32, (CHUNK, CHUNK), 1)
    tri = jnp.where((r >= c) if reverse else (r <= c), 1.0, 0.0).astype(bf16)
    a = gT.astype(bf16)
    rem = gT - a.astype(f32)
    b = rem.astype(bf16)
    x = _dot(a, tri) + _dot(b, tri) + _dot((rem - b.astype(f32)).astype(bf16), tri)
    padded = jnp.concatenate([x, jnp.zeros((LANES - x.shape[0], CHUNK), f32)], axis=0)
    return padded.T, x


def _mlstm_head(h, qk, cums, q, kT, v, gT, c_sc, m_sc, *, reverse):
    d = 1 if reverse else 0
    bcol_all, brow_all = cums
    r = lax.broadcasted_iota(jnp.int32, (CHUNK, CHUNK), 0)
    c = lax.broadcasted_iota(jnp.int32, (CHUNK, CHUNK), 1)
    mask = (c >= r) if reverse else (c <= r)
    ones_blk = jnp.ones((CHUNK, LANES), bf16)
    tile_shape = (SUBLANES, LANES)
    ji = d * H_B + h
    jf = 2 * H_B + d * H_B + h
    bcb = jnp.broadcast_to(bcol_all[:, jf:jf + 1], (CHUNK, LANES))
    br = brow_all[jf:jf + 1, :]
    a_row = gT[ji:ji + 1, :] - br
    gtot = jnp.broadcast_to(br[:, 0:1] if reverse else br[:, CHUNK - 1:CHUNK], tile_shape)
    m_prev = m_sc[h]
    mpb = jnp.concatenate([m_prev] * (CHUNK // SUBLANES), axis=0)
    qh = q[:, h * DH_B:(h + 1) * DH_B]
    kTh = kT[h * DH_B:(h + 1) * DH_B, :]
    v_aug = jnp.concatenate([v[:, h * DH_B:(h + 1) * DH_B], ones_blk], axis=1)
    amask = jnp.where(mask, a_row, -jnp.inf)
    mt = jnp.maximum(jnp.broadcast_to(jnp.max(amask, axis=1, keepdims=True), (CHUNK, LANES)), mpb)
    s = (qk * jnp.exp(amask - mt)).astype(bf16)
    e_in = jnp.exp(mpb - mt)
    c_prev = c_sc[h]
    numden = _dot(s, v_aug) + jnp.concatenate([e_in, e_in], axis=1) * _dot(qh, c_prev.astype(bf16))
    den = numden[:, DH_B:]
    out = numden[:, :DH_B] * (1.0 / jnp.maximum(jnp.abs(den), jnp.exp(-(bcb + mt))))
    a_end = gtot[0:1, :] + a_row
    m_loc = jnp.broadcast_to(jnp.max(a_end, axis=1, keepdims=True), tile_shape)
    m_new = jnp.maximum(gtot + m_prev, m_loc)
    sp = jnp.exp(gtot + m_prev - m_new)
    kw = (kTh.astype(f32) * jnp.exp(a_end - m_new[0:1, :])).astype(bf16)
    rep = lambda x: jnp.concatenate([jnp.concatenate([x] * (DH_B // SUBLANES), axis=0)] * 2, axis=1)
    c_sc[h] = rep(sp) * c_prev + _dot(kw, v_aug)
    m_sc[h] = m_new
    return out


def _mlstm_chunks(loaders, stores, c_sc, m_sc, *, reverse):
    qk_head = lambda q, kT, h: _dot(q[:, h * DH_B:(h + 1) * DH_B], kT[h * DH_B:(h + 1) * DH_B, :])
    cur = loaders[0]()
    cums = _mlstm_gates(cur[3], reverse=reverse)
    qks = [qk_head(cur[0], cur[1], h) for h in range(H_B)]
    for i, store in enumerate(stores):
        q, kT, v, gT = cur
        nxt = loaders[i + 1]() if i + 1 < len(loaders) else None
        cums_n = _mlstm_gates(nxt[3], reverse=reverse) if nxt is not None else None
        qks_n, outs = [], []
        for h in range(H_B):
            if nxt is not None:
                qks_n.append(qk_head(nxt[0], nxt[1], h))
            outs.append(_mlstm_head(h, qks[h], cums, q, kT, v, gT, c_sc, m_sc, reverse=reverse))
        store(jnp.concatenate(outs, axis=1))
        cur, cums, qks = nxt, cums_n, qks_n


def _mlstm_kernel(q_ref, kT_ref, v_ref, gT_ref, qt_ref, kTt_ref, vt_ref, gTt_ref, *rest, reverse):
    if reverse:
        hf_ref, hft_ref, o_ref, ot_ref, c_sc, m_sc = rest
    else:
        o_ref, ot_ref, c_sc, m_sc = rest
    j = pl.program_id(1)
    nj = pl.num_programs(1)
    nch = kT_ref.shape[0]
    run = functools.partial(_mlstm_chunks, c_sc=c_sc, m_sc=m_sc, reverse=reverse)

    def tail_chunk():
        def store(o):
            ot_ref[...] = ((o + hft_ref[...].astype(f32)) if reverse else o).astype(bf16)
        run([lambda: (qt_ref[...], kTt_ref[0], vt_ref[...], gTt_ref[0])], [store])

    @pl.when(j == 0)
    def _():
        c_sc[...] = jnp.zeros(c_sc.shape, f32)
        m_sc[...] = jnp.full(m_sc.shape, NEG, f32)
        if not reverse:
            tail_chunk()

    def loader(c):
        rows = slice(c * CHUNK, (c + 1) * CHUNK)
        return lambda: (q_ref[rows, :], kT_ref[c], v_ref[rows, :], gT_ref[c])

    def storer(c):
        rows = slice(c * CHUNK, (c + 1) * CHUNK)

        def store(o):
            o_ref[rows, :] = ((o + hf_ref[rows, :].astype(f32)) if reverse else o).astype(bf16)
        return store

    order = list(range(nch - 1, -1, -1) if reverse else range(nch))
    run([loader(c) for c in order], [storer(c) for c in order])

    if reverse:
        @pl.when(j == nj - 1)
        def _():
            tail_chunk()


def _mlstm(x_arrs, t_arrs, fwd_out=None, *, batch, seq, rb):
    reverse = fwd_out is not None
    nj = seq // rb
    nch = rb // CHUNK
    blk = (lambda b, j: b * nj + (nj - 1 - j)) if reverse else (lambda b, j: b * nj + j)
    in_specs = [
        pl.BlockSpec((rb, H_B * DH_B), lambda b, j: (blk(b, j), 0)),
        pl.BlockSpec((nch, H_B * DH_B, CHUNK), lambda b, j: (blk(b, j), 0, 0)),
        pl.BlockSpec((rb, H_B * DH_B), lambda b, j: (blk(b, j), 0)),
        pl.BlockSpec((nch, 4 * H_B, CHUNK), lambda b, j: (blk(b, j), 0, 0)),
        pl.BlockSpec((TAIL, H_B * DH_B), lambda b, j: (b, 0)),
        pl.BlockSpec((1, H_B * DH_B, CHUNK), lambda b, j: (b, 0, 0)),
        pl.BlockSpec((TAIL, H_B * DH_B), lambda b, j: (b, 0)),
        pl.BlockSpec((1, 4 * H_B, CHUNK), lambda b, j: (b, 0, 0)),
    ]
    out_specs = (pl.BlockSpec((rb, H_B * DH_B), lambda b, j: (blk(b, j), 0)),
                 pl.BlockSpec((TAIL, H_B * DH_B), lambda b, j: (b, 0)))
    out_shape = (jax.ShapeDtypeStruct((batch * seq, H_B * DH_B), bf16),
                 jax.ShapeDtypeStruct((batch * TAIL, H_B * DH_B), bf16))
    operands = tuple(x_arrs) + tuple(t_arrs)
    if reverse:
        in_specs = in_specs + list(out_specs)
        operands = operands + tuple(fwd_out)
    return pl.pallas_call(
        functools.partial(_mlstm_kernel, reverse=reverse),
        grid=(batch, nj), in_specs=in_specs, out_specs=out_specs, out_shape=out_shape,
        scratch_shapes=[pltpu.VMEM((H_B, DH_B, 2 * DH_B), f32), pltpu.VMEM((H_B, SUBLANES, LANES), f32)],
        compiler_params=_params(("parallel", "arbitrary")), name="mlstm_bwd" if reverse else "mlstm_fwd",
    )(*operands)


def _merge_kernel(attn_ref, hm_ref, so_ref, sga_ref, sgb_ref, res_ref, mn_ref, woa_ref, wom_ref, wout_ref, o_ref):
    hm = hm_ref[...].astype(f32)
    mn = mn_ref[...]
    parts = [_rms(hm[:, h * DH_B:(h + 1) * DH_B], mn[:, h * DH_B:(h + 1) * DH_B]) for h in range(H_B)]
    hmn = (jnp.concatenate(parts, axis=1) * so_ref[...].astype(f32)).astype(bf16)
    y = (sga_ref[...].astype(f32) * _dot(attn_ref[...], woa_ref[...])
         + sgb_ref[...].astype(f32) * _dot(hmn, wom_ref[...]))
    o_ref[...] = res_ref[...] + _dot(y.astype(bf16), wout_ref[...])


def _merge(attn, hm, so, sga, sgb, res, w, *, tm):
    rows = attn.shape[0]
    row_spec = lambda width: pl.BlockSpec((tm, width), lambda i: (i, 0))
    res_spec = row_spec(D_MODEL) if res.shape[0] == rows else pl.BlockSpec((tm, D_MODEL), lambda i: (0, 0))
    in_specs = [row_spec(H_A * V_DIM), row_spec(H_B * DH_B), row_spec(H_B * DH_B),
                row_spec(D_MODEL), row_spec(D_MODEL), res_spec,
                _const_spec((1, H_B * DH_B)), _const_spec(w["woa"].shape), _const_spec(w["wom"].shape),
                _const_spec(w["wout"].shape)]
    return pl.pallas_call(
        _merge_kernel, grid=(rows // tm,), in_specs=in_specs, out_specs=row_spec(D_MODEL),
        out_shape=jax.ShapeDtypeStruct((rows, D_MODEL), f32),
        compiler_params=_params(("parallel",)), name="merge",
    )(attn, hm, so, sga, sgb, res, w["mlstm_norm"], w["woa"], w["wom"], w["wout"])


def _gelu_tanh(x):
    return 0.5 * x * (1.0 + jnp.tanh(math.sqrt(2.0 / math.pi) * (x + 0.044715 * (x * x * x))))


def _ffn_kernel(h_ref, prev_ref, prevt_ref, next_ref, nf_ref, wup_ref, wgate_ref, cw_ref, cb_ref, wdown_ref,
                nfin_ref, o_ref, *, tiles_per_batch):
    i = pl.program_id(0)
    tm = h_ref.shape[0]
    first = (i % tiles_per_batch) == 0
    last = (i % tiles_per_batch) == tiles_per_batch - 1
    nf = nf_ref[...]
    h = h_ref[...]
    prev = jnp.where(first, prevt_ref[...], prev_ref[...])
    u_ext = _rms(jnp.concatenate([prev, h, next_ref[...]], axis=0), nf)
    u = u_ext[SUBLANES:tm + SUBLANES, :].astype(bf16)
    u_ext = u_ext.astype(bf16)
    row = lax.broadcasted_iota(jnp.int32, (tm, 1), 0)

    def up_gate(lo, hi):
        return _dot(u_ext, wup_ref[:, lo:hi]), _dot(u, wgate_ref[:, lo:hi])

    acc = h
    ahead = [up_gate(*FF_CHUNKS[j]) for j in range(FF_LOOKAHEAD)]
    for ci, (lo, hi) in enumerate(FF_CHUNKS):
        if ci + FF_LOOKAHEAD < len(FF_CHUNKS):
            ahead.append(up_gate(*FF_CHUNKS[ci + FF_LOOKAHEAD]))
        a_ext, gate = ahead.pop(0)
        a = a_ext[SUBLANES:tm + SUBLANES, :]
        a_prev = a_ext[SUBLANES - 1:SUBLANES, :]
        a_next = jnp.where(last, 0.0, a_ext[tm + SUBLANES:tm + SUBLANES + 1, :])
        a_dn = jnp.where(row == 0, a_prev, pltpu.roll(a, 1, axis=0))
        a_up = jnp.where(row == tm - 1, a_next, pltpu.roll(a, tm - 1, axis=0))
        cw = cw_ref[:, lo:hi]
        conv = a_dn * cw[0:1, :] + a * cw[1:2, :] + a_up * cw[2:3, :] + cb_ref[:, lo:hi]
        gated = (_gelu_tanh(conv) * gate).astype(bf16)
        acc = acc + _dot(gated, wdown_ref[lo:hi, :])
    o_ref[...] = _rms(acc, nfin_ref[...])


def _ffn(h1, h1_tail, w, *, seq, tm):
    rows = h1.shape[0]
    n = rows // tm
    tpb = seq // tm
    hb = tm // SUBLANES
    in_specs = [
        pl.BlockSpec((tm, D_MODEL), lambda i: (i, 0)),
        pl.BlockSpec((SUBLANES, D_MODEL), lambda i: (jnp.maximum(i * hb - 1, 0), 0)),
        pl.BlockSpec((SUBLANES, D_MODEL), lambda i: ((i // tpb) * (TAIL // SUBLANES) + N_META // SUBLANES - 1, 0)),
        pl.BlockSpec((SUBLANES, D_MODEL), lambda i: (jnp.minimum((i + 1) * hb, n * hb - 1), 0)),
        _const_spec((1, D_MODEL)), _const_spec(w["wup"].shape), _const_spec(w["wgate"].shape),
        _const_spec((SUBLANES, D_FF)), _const_spec((1, D_FF)), _const_spec(w["wdown"].shape),
        _const_spec((1, D_MODEL)),
    ]
    return pl.pallas_call(
        functools.partial(_ffn_kernel, tiles_per_batch=tpb),
        grid=(n,), in_specs=in_specs, out_specs=pl.BlockSpec((tm, D_MODEL), lambda i: (i, 0)),
        out_shape=jax.ShapeDtypeStruct((rows, D_MODEL), f32),
        compiler_params=_params(("parallel",)), name="conv_ffn",
    )(h1, h1, h1_tail, h1, w["norm_ffn"], w["wup"], w["wgate"], w["conv_w"], w["conv_b"], w["wdown"],
      w["norm_final"])


def _prep_weights(norm_mix, w_in, q_norm, w_uq, kv_norm, w_ukv, b_igate, b_fgate, mlstm_norm, w_o_attn, w_o_mlstm,
                  w_out, norm_ffn, w_up, w_gate, conv_w, conv_b, w_down, norm_final):
    w = w_in[0]
    offs = [0]
    for s in IN_SIZES:
        offs.append(offs[-1] + s)
    col = lambda j: w[:, offs[j]:offs[j + 1]]
    zeros = lambda r, c: jnp.zeros((r, c), f32)
    w_kr = col(2)
    half = ROPE // 2
    kr_main = jnp.concatenate([zeros(D_MODEL, NOPE), w_kr, zeros(D_MODEL, HEAD_PAD - QK_DIM)], axis=1)
    kr_rot = jnp.concatenate([zeros(D_MODEL, NOPE), -w_kr[:, half:], w_kr[:, :half],
                              zeros(D_MODEL, HEAD_PAD - QK_DIM)], axis=1)
    wa = jnp.concatenate([col(0), col(1), kr_main, kr_rot], axis=1)
    wb = jnp.concatenate([col(3), col(4), col(5), col(6), col(7), col(8), zeros(D_MODEL, LANES - 4 * H_B)], axis=1)
    wc = jnp.concatenate([col(9), col(10)], axis=1)
    wuq = w_uq[0].reshape(Q_LORA, H_A, QK_DIM)
    nope, rope = wuq[:, :, :NOPE], wuq[:, :, NOPE:]
    zq = lambda c: jnp.zeros((Q_LORA, H_A, c), f32)
    q_main = jnp.concatenate([nope, rope, zq(HEAD_PAD - QK_DIM)], axis=-1).reshape(Q_LORA, H_A * HEAD_PAD)
    q_rot = jnp.concatenate([zq(NOPE), -rope[:, :, half:], rope[:, :, :half], zq(HEAD_PAD - QK_DIM)],
                            axis=-1).reshape(Q_LORA, H_A * HEAD_PAD)
    wq = jnp.concatenate([q_main, q_rot], axis=1)
    wukv = w_ukv[0].reshape(KV_LORA, H_A, NOPE + V_DIM)
    wk = jnp.concatenate([wukv[:, :, :NOPE], jnp.zeros((KV_LORA, H_A, HEAD_PAD - NOPE), f32)],
                         axis=-1).reshape(KV_LORA, H_A * HEAD_PAD)
    wv = wukv[:, :, NOPE:].reshape(KV_LORA, H_A * V_DIM)
    wkv = jnp.concatenate([wk, wv], axis=1)
    gbias = jnp.concatenate([b_igate[0].reshape(-1), b_fgate[0].reshape(-1), jnp.zeros((LANES - 4 * H_B,), f32)])
    cw = jnp.concatenate([conv_w[0], jnp.zeros((SUBLANES - CONV_W, D_FF), f32)], axis=0)
    return dict(
        norm_mix=norm_mix[0][None], wa=wa.astype(bf16), wb=wb.astype(bf16), wc=wc.astype(bf16),
        q_norm=q_norm[0][None], wq=wq.astype(bf16), kv_norm=kv_norm[0][None], wkv=wkv.astype(bf16),
        gbias=gbias[None], mlstm_norm=mlstm_norm[0][None], woa=w_o_attn[0].astype(bf16),
        wom=w_o_mlstm[0].astype(bf16), wout=w_out[0].astype(bf16), norm_ffn=norm_ffn[0][None],
        wup=w_up[0].astype(bf16), wgate=w_gate[0].astype(bf16), conv_w=cw, conv_b=conv_b[0][None],
        wdown=w_down[0].astype(bf16), norm_final=norm_final[None])


def _rope_tables(positions):
    inv = ROPE_THETA ** (-jnp.arange(0, ROPE, 2, dtype=f32) / ROPE)
    ang = positions.astype(f32)[:, None] * inv[None, :]
    n = positions.shape[0]
    cos, sin = jnp.cos(ang), jnp.sin(ang)
    cm = jnp.concatenate([jnp.ones((n, NOPE), f32), cos, cos, jnp.zeros((n, HEAD_PAD - QK_DIM), f32)], axis=1)
    sm = jnp.concatenate([jnp.zeros((n, NOPE), f32), sin, sin, jnp.zeros((n, HEAD_PAD - QK_DIM), f32)], axis=1)
    return cm, sm


def _tiles(batch, seq):
    tm = math.gcd(seq, ROW_TILE)
    tm_tail = math.gcd(batch * TAIL, ROW_TILE)
    tkb = math.gcd(seq, 2048)
    tqb = math.gcd(seq, 2048)
    return dict(tm=tm, tm_wide=math.gcd(seq, 1024), tm_tail=tm_tail, tqb=tqb, tkb=tkb, rb=math.gcd(seq, 2048))


def _position_tables(max_seq):
    return _rope_tables(N_META + jnp.arange(max_seq)) + _rope_tables(jnp.tile(jnp.arange(TAIL), ROW_TILE // TAIL))


def _encode(x, meta_tokens, w, tables):
    batch, seq, _ = x.shape
    t = _tiles(batch, seq)
    xf = x.reshape(batch * seq, D_MODEL)
    tail_one = jnp.concatenate([meta_tokens.astype(f32), jnp.zeros((TAIL - N_META, D_MODEL), f32)], axis=0)
    tail_tile = jnp.tile(tail_one, (t["tm_tail"] // TAIL, 1))

    cm_x, sm_x, cm_t, sm_t = tables
    nper = seq // t["tm"]
    xs = _inproj(xf, cm_x, sm_x, lambda i: (i % nper, 0), w, rows=batch * seq, tm=t["tm"], cq=TQ, ck=KC,
                 mask_tail=False)
    ts = _inproj(tail_tile, cm_t, sm_t, lambda i: (0, 0), w, rows=batch * TAIL, tm=t["tm_tail"], cq=TAIL, ck=TAIL,
                 mask_tail=True)
    qT, k, vT, mq, mkT, mv, so, gT, sga, sgb = xs
    qT_t, k_t, vT_t, mq_t, mkT_t, mv_t, so_t, gT_t, sga_t, sgb_t = ts

    attn = _attention(qT, k, vT, k_t, vT_t, batch=batch, seq=seq, tqb=t["tqb"], tkb=t["tkb"], q_is_tail=False)
    attn_t = _attention(qT_t, k, vT, k_t, vT_t, batch=batch, seq=seq, tqb=TAIL, tkb=t["tkb"], q_is_tail=True)

    x_arrs = (mq, mkT, mv, gT)
    t_arrs = (mq_t, mkT_t, mv_t, gT_t)
    fwd = _mlstm(x_arrs, t_arrs, batch=batch, seq=seq, rb=t["rb"])
    hm, hm_t = _mlstm(x_arrs, t_arrs, fwd, batch=batch, seq=seq, rb=t["rb"])

    h1 = _merge(attn, hm, so, sga, sgb, xf, w, tm=t["tm_wide"])
    h1_t = _merge(attn_t, hm_t, so_t, sga_t, sgb_t, tail_tile, w, tm=t["tm_tail"])
    y = _ffn(h1, h1_t, w, seq=seq, tm=t["tm_wide"])
    return y.reshape(batch, seq, D_MODEL)


def kernel(x_prompt, x_sample, meta_tokens, norm_mix, w_in, q_norm, w_uq, kv_norm, w_ukv, b_igate, b_fgate,
           mlstm_norm, w_o_attn, w_o_mlstm, w_out, norm_ffn, w_up, w_gate, conv_w, conv_b, w_down, norm_final):
    w = _prep_weights(norm_mix, w_in, q_norm, w_uq, kv_norm, w_ukv, b_igate, b_fgate, mlstm_norm, w_o_attn,
                      w_o_mlstm, w_out, norm_ffn, w_up, w_gate, conv_w, conv_b, w_down, norm_final)
    tables = _position_tables(max(x_prompt.shape[1], x_sample.shape[1]))
    return (_encode(x_prompt, meta_tokens, w, tables), _encode(x_sample, meta_tokens, w, tables))
```

```python
import functools
import math

import jax
import jax.numpy as jnp
from jax import lax
from jax.experimental import pallas as pl
from jax.experimental.pallas import tpu as pltpu

D_MODEL = 1024
N_META = 16
H_A = 8
Q_LORA = 256
KV_LORA = 128
NOPE = 64
ROPE = 32
QK_DIM = NOPE + ROPE
V_DIM = 64
ROPE_THETA = 10000.0
H_B = 4
DH_B = 128
D_FF = 2816
CONV_W = 3
EPS = 1e-6
NEG = -1e30
IN_SIZES = (Q_LORA, KV_LORA, ROPE, H_B * DH_B, H_B * DH_B, H_B * DH_B, H_B * DH_B, 2 * H_B, 2 * H_B, D_MODEL, D_MODEL)

LANES = 128
SUBLANES = 8
TAIL = 128
HEAD_PAD = LANES
CHUNK = 128
TQ = 256
KC = 256
ATTN_RING = 4
ATTN_LOOKAHEAD = 2
ROW_TILE = 512
PV_LAG = 1
VMEM_LIMIT = 56 * 1024 * 1024
QSCALE = (QK_DIM ** -0.5) * math.log2(math.e)
FF_CHUNKS = tuple((lo, min(lo + 512, D_FF)) for lo in range(0, D_FF, 512))
FF_LOOKAHEAD = 4

f32 = jnp.float32
bf16 = jnp.bfloat16


def _rms(x, g):
    return x * lax.rsqrt(jnp.mean(x * x, axis=-1, keepdims=True) + EPS) * g


def _sigmoid(x):
    return 1.0 / (1.0 + jnp.exp(-x))


def _log_sigmoid(x):
    return jnp.minimum(x, 0.0) - jnp.log1p(jnp.exp(-jnp.abs(x)))


def _dot(a, b):
    return jnp.dot(a, b, preferred_element_type=f32)


def _const_spec(shape):
    zeros = (0,) * len(shape)
    return pl.BlockSpec(shape, lambda *_: zeros, pipeline_mode=pl.Buffered(1))


def _params(sem):
    return pltpu.CompilerParams(dimension_semantics=sem, vmem_limit_bytes=VMEM_LIMIT)


def _inproj_kernel(h_ref, cm_ref, sm_ref, nmix_ref, wa_ref, wb_ref, wc_ref, qn_ref, wq_ref, kvn_ref, wkv_ref,
                   gb_ref, qT_ref, k_ref, vT_ref, mq_ref, mkT_ref, mv_ref, so_ref, gT_ref, sga_ref, sgb_ref,
                   *, cq, ck, mask_tail):
    tm = h_ref.shape[0]
    u = _rms(h_ref[...], nmix_ref[...]).astype(bf16)
    cm = cm_ref[...]
    sm = sm_ref[...]

    za = _dot(u, wa_ref[...])
    qn = _rms(za[:, :Q_LORA], qn_ref[...]).astype(bf16)
    kvn = _rms(za[:, Q_LORA:Q_LORA + KV_LORA], kvn_ref[...]).astype(bf16)
    o_kr = Q_LORA + KV_LORA
    kr = za[:, o_kr:o_kr + HEAD_PAD] * cm + za[:, o_kr + HEAD_PAD:o_kr + 2 * HEAD_PAD] * sm
    cm8 = jnp.concatenate([cm] * H_A, axis=1)
    sm8 = jnp.concatenate([sm] * H_A, axis=1)
    q = (_dot(qn, wq_ref[:, :H_A * HEAD_PAD]) * cm8 + _dot(qn, wq_ref[:, H_A * HEAD_PAD:]) * sm8) * QSCALE
    qT = q.T
    for c in range(tm // cq):
        qT_ref[c] = qT[:, c * cq:(c + 1) * cq].astype(bf16)
    kv = _dot(kvn, wkv_ref[...])
    for h in range(H_A):
        kh = (kv[:, h * HEAD_PAD:(h + 1) * HEAD_PAD] + kr).astype(bf16)
        for c in range(tm // ck):
            k_ref[c, h] = kh[c * ck:(c + 1) * ck, :]
    vT = kv[:, H_A * HEAD_PAD:].T
    for c in range(tm // ck):
        vT_ref[c] = vT[:, c * ck:(c + 1) * ck].astype(bf16)

    zb = _dot(u, wb_ref[...])
    wm = H_B * DH_B
    mq_ref[...] = zb[:, 0:wm].astype(bf16)
    mkT = (zb[:, wm:2 * wm] * (DH_B ** -0.5)).T
    for c in range(tm // CHUNK):
        mkT_ref[c] = mkT[:, c * CHUNK:(c + 1) * CHUNK].astype(bf16)
    mv_ref[...] = zb[:, 2 * wm:3 * wm].astype(bf16)
    so_ref[...] = _sigmoid(zb[:, 3 * wm:4 * wm]).astype(bf16)
    gz = zb[:, 4 * wm:4 * wm + LANES] + gb_ref[...]
    lane = lax.broadcasted_iota(jnp.int32, gz.shape, 1)
    gates = jnp.where(lane < 2 * H_B, gz, _log_sigmoid(gz))
    if mask_tail:
        row = lax.broadcasted_iota(jnp.int32, gz.shape, 0)
        pad = (row % TAIL) >= N_META
        gates = jnp.where(pad, jnp.where(lane < 2 * H_B, NEG, 0.0), gates)
    gT = gates.T
    for c in range(tm // CHUNK):
        gT_ref[c] = gT[:4 * H_B, c * CHUNK:(c + 1) * CHUNK]

    zc = _dot(u, wc_ref[...])
    sga_ref[...] = _sigmoid(zc[:, :D_MODEL]).astype(bf16)
    sgb_ref[...] = _sigmoid(zc[:, D_MODEL:]).astype(bf16)


def _inproj(h, cm, sm, tab_map, w, *, rows, tm, cq, ck, mask_tail):
    n = rows // tm
    row_spec = lambda width: pl.BlockSpec((tm, width), lambda i: (i, 0))
    h_spec = row_spec(D_MODEL) if h.shape[0] == rows else pl.BlockSpec((tm, D_MODEL), lambda i: (0, 0))
    chunk_spec = lambda height, cw: pl.BlockSpec((tm // cw, height, cw), lambda i: (i, 0, 0))
    out_shape = (
        jax.ShapeDtypeStruct((rows // cq, H_A * HEAD_PAD, cq), bf16),
        jax.ShapeDtypeStruct((rows // ck, H_A, ck, HEAD_PAD), bf16),
        jax.ShapeDtypeStruct((rows // ck, H_A * V_DIM, ck), bf16),
        jax.ShapeDtypeStruct((rows, H_B * DH_B), bf16),
        jax.ShapeDtypeStruct((rows // CHUNK, H_B * DH_B, CHUNK), bf16),
        jax.ShapeDtypeStruct((rows, H_B * DH_B), bf16),
        jax.ShapeDtypeStruct((rows, H_B * DH_B), bf16),
        jax.ShapeDtypeStruct((rows // CHUNK, 4 * H_B, CHUNK), f32),
        jax.ShapeDtypeStruct((rows, D_MODEL), bf16),
        jax.ShapeDtypeStruct((rows, D_MODEL), bf16),
    )
    out_specs = (
        chunk_spec(H_A * HEAD_PAD, cq),
        pl.BlockSpec((tm // ck, H_A, ck, HEAD_PAD), lambda i: (i, 0, 0, 0)), chunk_spec(H_A * V_DIM, ck),
        row_spec(H_B * DH_B), chunk_spec(H_B * DH_B, CHUNK), row_spec(H_B * DH_B), row_spec(H_B * DH_B),
        chunk_spec(4 * H_B, CHUNK), row_spec(D_MODEL), row_spec(D_MODEL),
    )
    in_specs = [
        h_spec,
        pl.BlockSpec((tm, LANES), tab_map), pl.BlockSpec((tm, LANES), tab_map),
        _const_spec((1, D_MODEL)), _const_spec(w["wa"].shape), _const_spec(w["wb"].shape),
        _const_spec(w["wc"].shape), _const_spec((1, Q_LORA)), _const_spec(w["wq"].shape),
        _const_spec((1, KV_LORA)), _const_spec(w["wkv"].shape), _const_spec((1, LANES)),
    ]
    return pl.pallas_call(
        functools.partial(_inproj_kernel, cq=cq, ck=ck, mask_tail=mask_tail),
        grid=(n,), in_specs=in_specs, out_specs=out_specs, out_shape=out_shape,
        compiler_params=_params(("parallel",)), name="inproj",
    )(h, cm, sm, w["norm_mix"], w["wa"], w["wb"], w["wc"], w["q_norm"], w["wq"], w["kv_norm"], w["wkv"], w["gbias"])


def _attn_scores(qT_ref, k_ref, step, s_ref):
    qs, c = step

    def head(h):
        s_ref[h] = _dot(k_ref[c, h], qT_ref[qs, h * HEAD_PAD:(h + 1) * HEAD_PAD, :])
    return head


def _attn_update(vT_ref, step, s_ref, m_sc, l_sc, acc_sc, scores_head):
    qs, c = step
    m_all = m_sc[qs]
    l_all = l_sc[qs]
    kc = s_ref.shape[1]
    half = kc // 2
    ones = jnp.ones((2 * SUBLANES, kc), bf16)
    ms, ls = [], []

    def value_product(h, p, alpha):
        v_aug = jnp.concatenate([vT_ref[c, h * V_DIM:(h + 1) * V_DIM, :], ones], axis=0)
        pv = _dot(v_aug, p)
        ls.append(alpha * l_all[h:h + 1, :] + pv[V_DIM:V_DIM + 1, :])
        acc_sc[qs, h * V_DIM:(h + 1) * V_DIM, :] = (alpha * acc_sc[qs, h * V_DIM:(h + 1) * V_DIM, :]
                                                    + pv[0:V_DIM, :])

    pending = []
    for h in range(H_A):
        if scores_head is not None:
            scores_head(h)
        if PV_LAG and len(pending) == PV_LAG:
            value_product(*pending.pop(0))
        m_old = m_all[h:h + 1, :]
        m_new = jnp.maximum(m_old, jnp.max(s_ref[h], axis=0, keepdims=True))
        p = jnp.concatenate([jnp.exp2((s_ref[h, 0:half, :] - m_new).astype(bf16)),
                             jnp.exp2((s_ref[h, half:kc, :] - m_new).astype(bf16))], axis=0)
        ms.append(m_new)
        pending.append((h, p, jnp.exp2(m_old - m_new)))
        if not PV_LAG:
            value_product(*pending.pop(0))
    for item in pending:
        value_product(*item)
    m_sc[qs] = jnp.concatenate(ms, axis=0)
    l_sc[qs] = jnp.concatenate(ls, axis=0)


def _attn_kernel(qT_ref, k_ref, vT_ref, kt_ref, vtt_ref, o_ref, m_sc, l_sc, acc_sc, *s_bufs, single_kv_block):
    ki = pl.program_id(2)
    nqs, _, tq = qT_ref.shape
    nkc = vT_ref.shape[0]

    def init():
        zk = jnp.zeros((N_META, HEAD_PAD), bf16)
        k_bd = jnp.concatenate(
            [jnp.concatenate([kt_ref[0, h, 0:N_META, :] if j == h else zk for j in range(H_A)], axis=1)
             for h in range(H_A)], axis=0)
        rr = lax.broadcasted_iota(jnp.int32, (TAIL, H_A * N_META), 0)
        rc = lax.broadcasted_iota(jnp.int32, (TAIL, H_A * N_META), 1)
        rep = jnp.where((rr < N_META) & (rc % N_META == rr), 1.0, 0.0).astype(bf16)
        hv = lax.broadcasted_iota(jnp.int32, (H_A * V_DIM, H_A * N_META), 0) // V_DIM
        cv = lax.broadcasted_iota(jnp.int32, (H_A * V_DIM, H_A * N_META), 1) // N_META
        v_bd = jnp.where(hv == cv, _dot(vtt_ref[0], rep), 0.0).astype(bf16)
        for qs in range(nqs):
            s = _dot(k_bd, qT_ref[qs])
            ms, ls, ps = [], [], []
            for h in range(H_A):
                sh = s[h * N_META:(h + 1) * N_META, :]
                m = jnp.max(sh, axis=0, keepdims=True)
                p = jnp.exp2(sh - m)
                ms.append(m)
                ls.append(jnp.sum(p, axis=0, keepdims=True))
                ps.append(p)
            m_sc[qs] = jnp.concatenate(ms, axis=0)
            l_sc[qs] = jnp.concatenate(ls, axis=0)
            acc_sc[qs] = _dot(v_bd, jnp.concatenate(ps, axis=0).astype(bf16))

    def finalize():
        for qs in range(nqs):
            inv = 1.0 / l_sc[qs]
            outs = [acc_sc[qs, h * V_DIM:(h + 1) * V_DIM, :] * inv[h:h + 1, :] for h in range(H_A)]
            o = jnp.concatenate(outs, axis=0)
            o_ref[qs * tq:(qs + 1) * tq, :] = o.T.astype(bf16)

    if single_kv_block:
        init()
    else:
        pl.when(ki == 0)(init)

    bufs = tuple(s_bufs)
    for d in range(ATTN_LOOKAHEAD):
        first = _attn_scores(qT_ref, k_ref, (d // nkc, d % nkc), bufs[d % len(bufs)])
        for h in range(H_A):
            first(h)

    def group(qs, last):
        for c in range(nkc):
            ahead = c + ATTN_LOOKAHEAD
            nxt = None if (last and ahead >= nkc) else _attn_scores(
                qT_ref, k_ref, (qs + ahead // nkc, ahead % nkc), bufs[ahead % len(bufs)])
            _attn_update(vT_ref, (qs, c), bufs[c % len(bufs)], m_sc, l_sc, acc_sc, nxt)

    def body(qs, carry):
        group(qs, False)
        return carry
    lax.fori_loop(0, nqs - 1, body, 0)
    group(nqs - 1, True)

    if single_kv_block:
        finalize()
    else:
        pl.when(ki == pl.num_programs(2) - 1)(finalize)


def _attention(qT, k, vT, k_tail, vT_tail, *, batch, seq, tqb, tkb, q_is_tail):
    tq = qT.shape[2]
    kc = vT.shape[2]
    nqb = (TAIL if q_is_tail else seq) // tqb
    nkb = seq // tkb
    nqs = tqb // tq
    assert (tkb // kc) % ATTN_RING == 0, "a step's score buffer is its key chunk index modulo the ring size"
    assert ATTN_LOOKAHEAD < ATTN_RING and (nqs > 1 or ATTN_LOOKAHEAD <= tkb // kc)
    q_rows = qT.shape[0] * tq
    in_specs = [
        pl.BlockSpec((nqs, H_A * HEAD_PAD, tq), lambda b, qi, ki: (b * nqb + qi, 0, 0)),
        pl.BlockSpec((tkb // kc, H_A, kc, HEAD_PAD), lambda b, qi, ki: (b * nkb + ki, 0, 0, 0)),
        pl.BlockSpec((tkb // kc, H_A * V_DIM, kc), lambda b, qi, ki: (b * nkb + ki, 0, 0)),
        pl.BlockSpec((1, H_A, TAIL, HEAD_PAD), lambda b, qi, ki: (b, 0, 0, 0)),
        pl.BlockSpec((1, H_A * V_DIM, TAIL), lambda b, qi, ki: (b, 0, 0)),
    ]
    return pl.pallas_call(
        functools.partial(_attn_kernel, single_kv_block=(nkb == 1)),
        grid=(batch, nqb, nkb), in_specs=in_specs,
        out_specs=pl.BlockSpec((tqb, H_A * V_DIM), lambda b, qi, ki: (b * nqb + qi, 0)),
        out_shape=jax.ShapeDtypeStruct((q_rows, H_A * V_DIM), bf16),
        scratch_shapes=[pltpu.VMEM((nqs, H_A, tq), f32), pltpu.VMEM((nqs, H_A, tq), f32),
                        pltpu.VMEM((nqs, H_A * V_DIM, tq), f32),
                        ] + [pltpu.VMEM((H_A, kc, tq), f32)] * ATTN_RING,
        compiler_params=_params(("parallel", "parallel", "arbitrary")), name="attention",
    )(qT, k, vT, k_tail, vT_tail)


def _mlstm_gates(gT, *, reverse):
    r = lax.broadcasted_iota(jnp.int32, (CHUNK, CHUNK), 0)
    c = lax.broadcasted_iota(jnp.int32, (CHUNK, CHUNK), 1)
    tri = jnp.where((r >= c) if reverse else (r <= c), 1.0, 0.0).astype(bf16)
    a = gT.astype(bf16)
    rem = gT - a.astype(f32)
    b = rem.astype(bf16)
    x = _dot(a, tri) + _dot(b, tri) + _dot((rem - b.astype(f32)).astype(bf16), tri)
    padded = jnp.concatenate([x, jnp.zeros((LANES - x.shape[0], CHUNK), f32)], axis=0)
    return padded.T, x


def _mlstm_head(h, qk, cums, q, kT, v, gT, c_sc, m_sc, *, reverse):
    d = 1 if reverse else 0
    bcol_all, brow_all = cums
    r = lax.broadcasted_iota(jnp.int32, (CHUNK, CHUNK), 0)
    c = lax.broadcasted_iota(jnp.int32, (CHUNK, CHUNK), 1)
    mask = (c >= r) if reverse else (c <= r)
    ones_blk = jnp.ones((CHUNK, LANES), bf16)
    tile_shape = (SUBLANES, LANES)
    ji = d * H_B + h
    jf = 2 * H_B + d * H_B + h
    bcb = jnp.broadcast_to(bcol_all[:, jf:jf + 1], (CHUNK, LANES))
    br = brow_all[jf:jf + 1, :]
    a_row = gT[ji:ji + 1, :] - br
    gtot = jnp.broadcast_to(br[:, 0:1] if reverse else br[:, CHUNK - 1:CHUNK], tile_shape)
    m_prev = m_sc[h]
    mpb = jnp.concatenate([m_prev] * (CHUNK // SUBLANES), axis=0)
    qh = q[:, h * DH_B:(h + 1) * DH_B]
    kTh = kT[h * DH_B:(h + 1) * DH_B, :]
    v_aug = jnp.concatenate([v[:, h * DH_B:(h + 1) * DH_B], ones_blk], axis=1)
    amask = jnp.where(mask, a_row, -jnp.inf)
    mt = jnp.maximum(jnp.broadcast_to(jnp.max(amask, axis=1, keepdims=True), (CHUNK, LANES)), mpb)
    s = (qk * jnp.exp(amask - mt)).astype(bf16)
    e_in = jnp.exp(mpb - mt)
    c_prev = c_sc[h]
    numden = _dot(s, v_aug) + jnp.concatenate([e_in, e_in], axis=1) * _dot(qh, c_prev.astype(bf16))
    den = numden[:, DH_B:]
    out = numden[:, :DH_B] * (1.0 / jnp.maximum(jnp.abs(den), jnp.exp(-(bcb + mt))))
    a_end = gtot[0:1, :] + a_row
    m_loc = jnp.broadcast_to(jnp.max(a_end, axis=1, keepdims=True), tile_shape)
    m_new = jnp.maximum(gtot + m_prev, m_loc)
    sp = jnp.exp(gtot + m_prev - m_new)
    kw = (kTh.astype(f32) * jnp.exp(a_end - m_new[0:1, :])).astype(bf16)
    rep = lambda x: jnp.concatenate([jnp.concatenate([x] * (DH_B // SUBLANES), axis=0)] * 2, axis=1)
    c_sc[h] = rep(sp) * c_prev + _dot(kw, v_aug)
    m_sc[h] = m_new
    return out


def _mlstm_chunks(loaders, stores, c_sc, m_sc, *, reverse):
    qk_head = lambda q, kT, h: _dot(q[:, h * DH_B:(h + 1) * DH_B], kT[h * DH_B:(h + 1) * DH_B, :])
    cur = loaders[0]()
    cums = _mlstm_gates(cur[3], reverse=reverse)
    qks = [qk_head(cur[0], cur[1], h) for h in range(H_B)]
    for i, store in enumerate(stores):
        q, kT, v, gT = cur
        nxt = loaders[i + 1]() if i + 1 < len(loaders) else None
        cums_n = _mlstm_gates(nxt[3], reverse=reverse) if nxt is not None else None
        qks_n, outs = [], []
        for h in range(H_B):
            if nxt is not None:
                qks_n.append(qk_head(nxt[0], nxt[1], h))
            outs.append(_mlstm_head(h, qks[h], cums, q, kT, v, gT, c_sc, m_sc, reverse=reverse))
        store(jnp.concatenate(outs, axis=1))
        cur, cums, qks = nxt, cums_n, qks_n


def _mlstm_kernel(q_ref, kT_ref, v_ref, gT_ref, qt_ref, kTt_ref, vt_ref, gTt_ref, *rest, reverse):
    if reverse:
        hf_ref, hft_ref, o_ref, ot_ref, c_sc, m_sc = rest
    else:
        o_ref, ot_ref, c_sc, m_sc = rest
    j = pl.program_id(1)
    nj = pl.num_programs(1)
    nch = kT_ref.shape[0]
    run = functools.partial(_mlstm_chunks, c_sc=c_sc, m_sc=m_sc, reverse=reverse)

    def tail_chunk():
        def store(o):
            ot_ref[...] = ((o + hft_ref[...].astype(f32)) if reverse else o).astype(bf16)
        run([lambda: (qt_ref[...], kTt_ref[0], vt_ref[...], gTt_ref[0])], [store])

    @pl.when(j == 0)
    def _():
        c_sc[...] = jnp.zeros(c_sc.shape, f32)
        m_sc[...] = jnp.full(m_sc.shape, NEG, f32)
        if not reverse:
            tail_chunk()

    def loader(c):
        rows = slice(c * CHUNK, (c + 1) * CHUNK)
        return lambda: (q_ref[rows, :], kT_ref[c], v_ref[rows, :], gT_ref[c])

    def storer(c):
        rows = slice(c * CHUNK, (c + 1) * CHUNK)

        def store(o):
            o_ref[rows, :] = ((o + hf_ref[rows, :].astype(f32)) if reverse else o).astype(bf16)
        return store

    order = list(range(nch - 1, -1, -1) if reverse else range(nch))
    run([loader(c) for c in order], [storer(c) for c in order])

    if reverse:
        @pl.when(j == nj - 1)
        def _():
            tail_chunk()


def _mlstm(x_arrs, t_arrs, fwd_out=None, *, batch, seq, rb):
    reverse = fwd_out is not None
    nj = seq // rb
    nch = rb // CHUNK
    blk = (lambda b, j: b * nj + (nj - 1 - j)) if reverse else (lambda b, j: b * nj + j)
    in_specs = [
        pl.BlockSpec((rb, H_B * DH_B), lambda b, j: (blk(b, j), 0)),
        pl.BlockSpec((nch, H_B * DH_B, CHUNK), lambda b, j: (blk(b, j), 0, 0)),
        pl.BlockSpec((rb, H_B * DH_B), lambda b, j: (blk(b, j), 0)),
        pl.BlockSpec((nch, 4 * H_B, CHUNK), lambda b, j: (blk(b, j), 0, 0)),
        pl.BlockSpec((TAIL, H_B * DH_B), lambda b, j: (b, 0)),
        pl.BlockSpec((1, H_B * DH_B, CHUNK), lambda b, j: (b, 0, 0)),
        pl.BlockSpec((TAIL, H_B * DH_B), lambda b, j: (b, 0)),
        pl.BlockSpec((1, 4 * H_B, CHUNK), lambda b, j: (b, 0, 0)),
    ]
    out_specs = (pl.BlockSpec((rb, H_B * DH_B), lambda b, j: (blk(b, j), 0)),
                 pl.BlockSpec((TAIL, H_B * DH_B), lambda b, j: (b, 0)))
    out_shape = (jax.ShapeDtypeStruct((batch * seq, H_B * DH_B), bf16),
                 jax.ShapeDtypeStruct((batch * TAIL, H_B * DH_B), bf16))
    operands = tuple(x_arrs) + tuple(t_arrs)
    if reverse:
        in_specs = in_specs + list(out_specs)
        operands = operands + tuple(fwd_out)
    return pl.pallas_call(
        functools.partial(_mlstm_kernel, reverse=reverse),
        grid=(batch, nj), in_specs=in_specs, out_specs=out_specs, out_shape=out_shape,
        scratch_shapes=[pltpu.VMEM((H_B, DH_B, 2 * DH_B), f32), pltpu.VMEM((H_B, SUBLANES, LANES), f32)],
        compiler_params=_params(("parallel", "arbitrary")), name="mlstm_bwd" if reverse else "mlstm_fwd",
    )(*operands)


def _merge_kernel(attn_ref, hm_ref, so_ref, sga_ref, sgb_ref, res_ref, mn_ref, woa_ref, wom_ref, wout_ref, o_ref):
    hm = hm_ref[...].astype(f32)
    mn = mn_ref[...]
    parts = [_rms(hm[:, h * DH_B:(h + 1) * DH_B], mn[:, h * DH_B:(h + 1) * DH_B]) for h in range(H_B)]
    hmn = (jnp.concatenate(parts, axis=1) * so_ref[...].astype(f32)).astype(bf16)
    y = (sga_ref[...].astype(f32) * _dot(attn_ref[...], woa_ref[...])
         + sgb_ref[...].astype(f32) * _dot(hmn, wom_ref[...]))
    o_ref[...] = res_ref[...] + _dot(y.astype(bf16), wout_ref[...])


def _merge(attn, hm, so, sga, sgb, res, w, *, tm):
    rows = attn.shape[0]
    row_spec = lambda width: pl.BlockSpec((tm, width), lambda i: (i, 0))
    res_spec = row_spec(D_MODEL) if res.shape[0] == rows else pl.BlockSpec((tm, D_MODEL), lambda i: (0, 0))
    in_specs = [row_spec(H_A * V_DIM), row_spec(H_B * DH_B), row_spec(H_B * DH_B),
                row_spec(D_MODEL), row_spec(D_MODEL), res_spec,
                _const_spec((1, H_B * DH_B)), _const_spec(w["woa"].shape), _const_spec(w["wom"].shape),
                _const_spec(w["wout"].shape)]
    return pl.pallas_call(
        _merge_kernel, grid=(rows // tm,), in_specs=in_specs, out_specs=row_spec(D_MODEL),
        out_shape=jax.ShapeDtypeStruct((rows, D_MODEL), f32),
        compiler_params=_params(("parallel",)), name="merge",
    )(attn, hm, so, sga, sgb, res, w["mlstm_norm"], w["woa"], w["wom"], w["wout"])


def _gelu_tanh(x):
    return 0.5 * x * (1.0 + jnp.tanh(math.sqrt(2.0 / math.pi) * (x + 0.044715 * (x * x * x))))


def _ffn_kernel(h_ref, prev_ref, prevt_ref, next_ref, nf_ref, wup_ref, wgate_ref, cw_ref, cb_ref, wdown_ref,
                nfin_ref, o_ref, *, tiles_per_batch):
    i = pl.program_id(0)
    tm = h_ref.shape[0]
    first = (i % tiles_per_batch) == 0
    last = (i % tiles_per_batch) == tiles_per_batch - 1
    nf = nf_ref[...]
    h = h_ref[...]
    prev = jnp.where(first, prevt_ref[...], prev_ref[...])
    u_ext = _rms(jnp.concatenate([prev, h, next_ref[...]], axis=0), nf)
    u = u_ext[SUBLANES:tm + SUBLANES, :].astype(bf16)
    u_ext = u_ext.astype(bf16)
    row = lax.broadcasted_iota(jnp.int32, (tm, 1), 0)

    def up_gate(lo, hi):
        return _dot(u_ext, wup_ref[:, lo:hi]), _dot(u, wgate_ref[:, lo:hi])

    acc = h
    ahead = [up_gate(*FF_CHUNKS[j]) for j in range(FF_LOOKAHEAD)]
    for ci, (lo, hi) in enumerate(FF_CHUNKS):
        if ci + FF_LOOKAHEAD < len(FF_CHUNKS):
            ahead.append(up_gate(*FF_CHUNKS[ci + FF_LOOKAHEAD]))
        a_ext, gate = ahead.pop(0)
        a = a_ext[SUBLANES:tm + SUBLANES, :]
        a_prev = a_ext[SUBLANES - 1:SUBLANES, :]
        a_next = jnp.where(last, 0.0, a_ext[tm + SUBLANES:tm + SUBLANES + 1, :])
        a_dn = jnp.where(row == 0, a_prev, pltpu.roll(a, 1, axis=0))
        a_up = jnp.where(row == tm - 1, a_next, pltpu.roll(a, tm - 1, axis=0))
        cw = cw_ref[:, lo:hi]
        conv = a_dn * cw[0:1, :] + a * cw[1:2, :] + a_up * cw[2:3, :] + cb_ref[:, lo:hi]
        gated = (_gelu_tanh(conv) * gate).astype(bf16)
        acc = acc + _dot(gated, wdown_ref[lo:hi, :])
    o_ref[...] = _rms(acc, nfin_ref[...])


def _ffn(h1, h1_tail, w, *, seq, tm):
    rows = h1.shape[0]
    n = rows // tm
    tpb = seq // tm
    hb = tm // SUBLANES
    in_specs = [
        pl.BlockSpec((tm, D_MODEL), lambda i: (i, 0)),
        pl.BlockSpec((SUBLANES, D_MODEL), lambda i: (jnp.maximum(i * hb - 1, 0), 0)),
        pl.BlockSpec((SUBLANES, D_MODEL), lambda i: ((i // tpb) * (TAIL // SUBLANES) + N_META // SUBLANES - 1, 0)),
        pl.BlockSpec((SUBLANES, D_MODEL), lambda i: (jnp.minimum((i + 1) * hb, n * hb - 1), 0)),
        _const_spec((1, D_MODEL)), _const_spec(w["wup"].shape), _const_spec(w["wgate"].shape),
        _const_spec((SUBLANES, D_FF)), _const_spec((1, D_FF)), _const_spec(w["wdown"].shape),
        _const_spec((1, D_MODEL)),
    ]
    return pl.pallas_call(
        functools.partial(_ffn_kernel, tiles_per_batch=tpb),
        grid=(n,), in_specs=in_specs, out_specs=pl.BlockSpec((tm, D_MODEL), lambda i: (i, 0)),
        out_shape=jax.ShapeDtypeStruct((rows, D_MODEL), f32),
        compiler_params=_params(("parallel",)), name="conv_ffn",
    )(h1, h1, h1_tail, h1, w["norm_ffn"], w["wup"], w["wgate"], w["conv_w"], w["conv_b"], w["wdown"],
      w["norm_final"])


def _prep_weights(norm_mix, w_in, q_norm, w_uq, kv_norm, w_ukv, b_igate, b_fgate, mlstm_norm, w_o_attn, w_o_mlstm,
                  w_out, norm_ffn, w_up, w_gate, conv_w, conv_b, w_down, norm_final):
    w = w_in[0]
    offs = [0]
    for s in IN_SIZES:
        offs.append(offs[-1] + s)
    col = lambda j: w[:, offs[j]:offs[j + 1]]
    zeros = lambda r, c: jnp.zeros((r, c), f32)
    w_kr = col(2)
    half = ROPE // 2
    kr_main = jnp.concatenate([zeros(D_MODEL, NOPE), w_kr, zeros(D_MODEL, HEAD_PAD - QK_DIM)], axis=1)
    kr_rot = jnp.concatenate([zeros(D_MODEL, NOPE), -w_kr[:, half:], w_kr[:, :half],
                              zeros(D_MODEL, HEAD_PAD - QK_DIM)], axis=1)
    wa = jnp.concatenate([col(0), col(1), kr_main, kr_rot], axis=1)
    wb = jnp.concatenate([col(3), col(4), col(5), col(6), col(7), col(8), zeros(D_MODEL, LANES - 4 * H_B)], axis=1)
    wc = jnp.concatenate([col(9), col(10)], axis=1)
    wuq = w_uq[0].reshape(Q_LORA, H_A, QK_DIM)
    nope, rope = wuq[:, :, :NOPE], wuq[:, :, NOPE:]
    zq = lambda c: jnp.zeros((Q_LORA, H_A, c), f32)
    q_main = jnp.concatenate([nope, rope, zq(HEAD_PAD - QK_DIM)], axis=-1).reshape(Q_LORA, H_A * HEAD_PAD)
    q_rot = jnp.concatenate([zq(NOPE), -rope[:, :, half:], rope[:, :, :half], zq(HEAD_PAD - QK_DIM)],
                            axis=-1).reshape(Q_LORA, H_A * HEAD_PAD)
    wq = jnp.concatenate([q_main, q_rot], axis=1)
    wukv = w_ukv[0].reshape(KV_LORA, H_A, NOPE + V_DIM)
    wk = jnp.concatenate([wukv[:, :, :NOPE], jnp.zeros((KV_LORA, H_A, HEAD_PAD - NOPE), f32)],
                         axis=-1).reshape(KV_LORA, H_A * HEAD_PAD)
    wv = wukv[:, :, NOPE:].reshape(KV_LORA, H_A * V_DIM)
    wkv = jnp.concatenate([wk, wv], axis=1)
    gbias = jnp.concatenate([b_igate[0].reshape(-1), b_fgate[0].reshape(-1), jnp.zeros((LANES - 4 * H_B,), f32)])
    cw = jnp.concatenate([conv_w[0], jnp.zeros((SUBLANES - CONV_W, D_FF), f32)], axis=0)
    return dict(
        norm_mix=norm_mix[0][None], wa=wa.astype(bf16), wb=wb.astype(bf16), wc=wc.astype(bf16),
        q_norm=q_norm[0][None], wq=wq.astype(bf16), kv_norm=kv_norm[0][None], wkv=wkv.astype(bf16),
        gbias=gbias[None], mlstm_norm=mlstm_norm[0][None], woa=w_o_attn[0].astype(bf16),
        wom=w_o_mlstm[0].astype(bf16), wout=w_out[0].astype(bf16), norm_ffn=norm_ffn[0][None],
        wup=w_up[0].astype(bf16), wgate=w_gate[0].astype(bf16), conv_w=cw, conv_b=conv_b[0][None],
        wdown=w_down[0].astype(bf16), norm_final=norm_final[None])


def _rope_tables(positions):
    inv = ROPE_THETA ** (-jnp.arange(0, ROPE, 2, dtype=f32) / ROPE)
    ang = positions.astype(f32)[:, None] * inv[None, :]
    n = positions.shape[0]
    cos, sin = jnp.cos(ang), jnp.sin(ang)
    cm = jnp.concatenate([jnp.ones((n, NOPE), f32), cos, cos, jnp.zeros((n, HEAD_PAD - QK_DIM), f32)], axis=1)
    sm = jnp.concatenate([jnp.zeros((n, NOPE), f32), sin, sin, jnp.zeros((n, HEAD_PAD - QK_DIM), f32)], axis=1)
    return cm, sm


def _tiles(batch, seq):
    tm = math.gcd(seq, ROW_TILE)
    tm_tail = math.gcd(batch * TAIL, ROW_TILE)
    tkb = math.gcd(seq, 2048)
    tqb = math.gcd(seq, 2048)
    return dict(tm=tm, tm_wide=math.gcd(seq, 1024), tm_tail=tm_tail, tqb=tqb, tkb=tkb, rb=math.gcd(seq, 2048))


def _position_tables(max_seq):
    return _rope_tables(N_META + jnp.arange(max_seq)) + _rope_tables(jnp.tile(jnp.arange(TAIL), ROW_TILE // TAIL))


def _encode(x, meta_tokens, w, tables):
    batch, seq, _ = x.shape
    t = _tiles(batch, seq)
    xf = x.reshape(batch * seq, D_MODEL)
    tail_one = jnp.concatenate([meta_tokens.astype(f32), jnp.zeros((TAIL - N_META, D_MODEL), f32)], axis=0)
    tail_tile = jnp.tile(tail_one, (t["tm_tail"] // TAIL, 1))

    cm_x, sm_x, cm_t, sm_t = tables
    nper = seq // t["tm"]
    xs = _inproj(xf, cm_x, sm_x, lambda i: (i % nper, 0), w, rows=batch * seq, tm=t["tm"], cq=TQ, ck=KC,
                 mask_tail=False)
    ts = _inproj(tail_tile, cm_t, sm_t, lambda i: (0, 0), w, rows=batch * TAIL, tm=t["tm_tail"], cq=TAIL, ck=TAIL,
                 mask_tail=True)
    qT, k, vT, mq, mkT, mv, so, gT, sga, sgb = xs
    qT_t, k_t, vT_t, mq_t, mkT_t, mv_t, so_t, gT_t, sga_t, sgb_t = ts

    attn = _attention(qT, k, vT, k_t, vT_t, batch=batch, seq=seq, tqb=t["tqb"], tkb=t["tkb"], q_is_tail=False)
    attn_t = _attention(qT_t, k, vT, k_t, vT_t, batch=batch, seq=seq, tqb=TAIL, tkb=t["tkb"], q_is_tail=True)

    x_arrs = (mq, mkT, mv, gT)
    t_arrs = (mq_t, mkT_t, mv_t, gT_t)
    fwd = _mlstm(x_arrs, t_arrs, batch=batch, seq=seq, rb=t["rb"])
    hm, hm_t = _mlstm(x_arrs, t_arrs, fwd, batch=batch, seq=seq, rb=t["rb"])

    h1 = _merge(attn, hm, so, sga, sgb, xf, w, tm=t["tm_wide"])
    h1_t = _merge(attn_t, hm_t, so_t, sga_t, sgb_t, tail_tile, w, tm=t["tm_tail"])
    y = _ffn(h1, h1_t, w, seq=seq, tm=t["tm_wide"])
    return y.reshape(batch, seq, D_MODEL)


def kernel(x_prompt, x_sample, meta_tokens, norm_mix, w_in, q_norm, w_uq, kv_norm, w_ukv, b_igate, b_fgate,
           mlstm_norm, w_o_attn, w_o_mlstm, w_out, norm_ffn, w_up, w_gate, conv_w, conv_b, w_down, norm_final):
    w = _prep_weights(norm_mix, w_in, q_norm, w_uq, kv_norm, w_ukv, b_igate, b_fgate, mlstm_norm, w_o_attn,
                      w_o_mlstm, w_out, norm_ffn, w_up, w_gate, conv_w, conv_b, w_down, norm_final)
    tables = _position_tables(max(x_prompt.shape[1], x_sample.shape[1]))
    return (_encode(x_prompt, meta_tokens, w, tables), _encode(x_sample, meta_tokens, w, tables))
```

```python
import functools
import math

import jax
import jax.numpy as jnp
from jax import lax
from jax.experimental import pallas as pl
from jax.experimental.pallas import tpu as pltpu

D_MODEL = 1024
N_META = 16
H_A = 8
Q_LORA = 256
KV_LORA = 128
NOPE = 64
ROPE = 32
QK_DIM = NOPE + ROPE
V_DIM = 64
ROPE_THETA = 10000.0
H_B = 4
DH_B = 128
D_FF = 2816
CONV_W = 3
EPS = 1e-6
NEG = -1e30
IN_SIZES = (Q_LORA, KV_LORA, ROPE, H_B * DH_B, H_B * DH_B, H_B * DH_B, H_B * DH_B, 2 * H_B, 2 * H_B, D_MODEL, D_MODEL)

LANES = 128
SUBLANES = 8
TAIL = 128
HEAD_PAD = LANES
CHUNK = 128
TQ = 256
KC = 256
MLSTM_LOOKAHEAD = 2
ATTN_RING = 4
ATTN_LOOKAHEAD = 2
ROW_TILE = 512
PV_LAG = 1
VMEM_LIMIT = 56 * 1024 * 1024
QSCALE = (QK_DIM ** -0.5) * math.log2(math.e)
FF_CHUNKS = tuple((lo, min(lo + 512, D_FF)) for lo in range(0, D_FF, 512))
FF_LOOKAHEAD = 4

f32 = jnp.float32
bf16 = jnp.bfloat16


def _rms(x, g):
    return x * lax.rsqrt(jnp.mean(x * x, axis=-1, keepdims=True) + EPS) * g


def _sigmoid(x):
    return 1.0 / (1.0 + jnp.exp(-x))


def _log_sigmoid(x):
    return jnp.minimum(x, 0.0) - jnp.log1p(jnp.exp(-jnp.abs(x)))


def _dot(a, b):
    return jnp.dot(a, b, preferred_element_type=f32)


def _const_spec(shape):
    zeros = (0,) * len(shape)
    return pl.BlockSpec(shape, lambda *_: zeros, pipeline_mode=pl.Buffered(1))


def _params(sem):
    return pltpu.CompilerParams(dimension_semantics=sem, vmem_limit_bytes=VMEM_LIMIT)


def _inproj_kernel(h_ref, cm_ref, sm_ref, nmix_ref, wa_ref, wb_ref, wc_ref, qn_ref, wq_ref, kvn_ref, wkv_ref,
                   gb_ref, qT_ref, k_ref, vT_ref, mq_ref, mkT_ref, mv_ref, so_ref, gT_ref, sga_ref, sgb_ref,
                   *, cq, ck, mask_tail):
    tm = h_ref.shape[0]
    u = _rms(h_ref[...], nmix_ref[...]).astype(bf16)
    cm = cm_ref[...]
    sm = sm_ref[...]

    za = _dot(u, wa_ref[...])
    qn = _rms(za[:, :Q_LORA], qn_ref[...]).astype(bf16)
    kvn = _rms(za[:, Q_LORA:Q_LORA + KV_LORA], kvn_ref[...]).astype(bf16)
    o_kr = Q_LORA + KV_LORA
    kr = za[:, o_kr:o_kr + HEAD_PAD] * cm + za[:, o_kr + HEAD_PAD:o_kr + 2 * HEAD_PAD] * sm
    cm8 = jnp.concatenate([cm] * H_A, axis=1)
    sm8 = jnp.concatenate([sm] * H_A, axis=1)
    q = (_dot(qn, wq_ref[:, :H_A * HEAD_PAD]) * cm8 + _dot(qn, wq_ref[:, H_A * HEAD_PAD:]) * sm8) * QSCALE
    qT = q.T
    for c in range(tm // cq):
        qT_ref[c] = qT[:, c * cq:(c + 1) * cq].astype(bf16)
    kv = _dot(kvn, wkv_ref[...])
    for h in range(H_A):
        kh = (kv[:, h * HEAD_PAD:(h + 1) * HEAD_PAD] + kr).astype(bf16)
        for c in range(tm // ck):
            k_ref[c, h] = kh[c * ck:(c + 1) * ck, :]
    vT = kv[:, H_A * HEAD_PAD:].T
    for c in range(tm // ck):
        vT_ref[c] = vT[:, c * ck:(c + 1) * ck].astype(bf16)

    zb = _dot(u, wb_ref[...])
    wm = H_B * DH_B
    mq_ref[...] = zb[:, 0:wm].astype(bf16)
    mkT = (zb[:, wm:2 * wm] * (DH_B ** -0.5)).T
    for c in range(tm // CHUNK):
        mkT_ref[c] = mkT[:, c * CHUNK:(c + 1) * CHUNK].astype(bf16)
    mv_ref[...] = zb[:, 2 * wm:3 * wm].astype(bf16)
    so_ref[...] = _sigmoid(zb[:, 3 * wm:4 * wm]).astype(bf16)
    gz = zb[:, 4 * wm:4 * wm + LANES] + gb_ref[...]
    lane = lax.broadcasted_iota(jnp.int32, gz.shape, 1)
    gates = jnp.where(lane < 2 * H_B, gz, _log_sigmoid(gz))
    if mask_tail:
        row = lax.broadcasted_iota(jnp.int32, gz.shape, 0)
        pad = (row % TAIL) >= N_META
        gates = jnp.where(pad, jnp.where(lane < 2 * H_B, NEG, 0.0), gates)
    gT = gates.T
    for c in range(tm // CHUNK):
        gT_ref[c] = gT[:4 * H_B, c * CHUNK:(c + 1) * CHUNK]

    zc = _dot(u, wc_ref[...])
    sga_ref[...] = _sigmoid(zc[:, :D_MODEL]).astype(bf16)
    sgb_ref[...] = _sigmoid(zc[:, D_MODEL:]).astype(bf16)


def _inproj(h, cm, sm, tab_map, w, *, rows, tm, cq, ck, mask_tail):
    n = rows // tm
    row_spec = lambda width: pl.BlockSpec((tm, width), lambda i: (i, 0))
    h_spec = row_spec(D_MODEL) if h.shape[0] == rows else pl.BlockSpec((tm, D_MODEL), lambda i: (0, 0))
    chunk_spec = lambda height, cw: pl.BlockSpec((tm // cw, height, cw), lambda i: (i, 0, 0))
    out_shape = (
        jax.ShapeDtypeStruct((rows // cq, H_A * HEAD_PAD, cq), bf16),
        jax.ShapeDtypeStruct((rows // ck, H_A, ck, HEAD_PAD), bf16),
        jax.ShapeDtypeStruct((rows // ck, H_A * V_DIM, ck), bf16),
        jax.ShapeDtypeStruct((rows, H_B * DH_B), bf16),
        jax.ShapeDtypeStruct((rows // CHUNK, H_B * DH_B, CHUNK), bf16),
        jax.ShapeDtypeStruct((rows, H_B * DH_B), bf16),
        jax.ShapeDtypeStruct((rows, H_B * DH_B), bf16),
        jax.ShapeDtypeStruct((rows // CHUNK, 4 * H_B, CHUNK), f32),
        jax.ShapeDtypeStruct((rows, D_MODEL), bf16),
        jax.ShapeDtypeStruct((rows, D_MODEL), bf16),
    )
    out_specs = (
        chunk_spec(H_A * HEAD_PAD, cq),
        pl.BlockSpec((tm // ck, H_A, ck, HEAD_PAD), lambda i: (i, 0, 0, 0)), chunk_spec(H_A * V_DIM, ck),
        row_spec(H_B * DH_B), chunk_spec(H_B * DH_B, CHUNK), row_spec(H_B * DH_B), row_spec(H_B * DH_B),
        chunk_spec(4 * H_B, CHUNK), row_spec(D_MODEL), row_spec(D_MODEL),
    )
    in_specs = [
        h_spec,
        pl.BlockSpec((tm, LANES), tab_map), pl.BlockSpec((tm, LANES), tab_map),
        _const_spec((1, D_MODEL)), _const_spec(w["wa"].shape), _const_spec(w["wb"].shape),
        _const_spec(w["wc"].shape), _const_spec((1, Q_LORA)), _const_spec(w["wq"].shape),
        _const_spec((1, KV_LORA)), _const_spec(w["wkv"].shape), _const_spec((1, LANES)),
    ]
    return pl.pallas_call(
        functools.partial(_inproj_kernel, cq=cq, ck=ck, mask_tail=mask_tail),
        grid=(n,), in_specs=in_specs, out_specs=out_specs, out_shape=out_shape,
        compiler_params=_params(("parallel",)), name="inproj",
    )(h, cm, sm, w["norm_mix"], w["wa"], w["wb"], w["wc"], w["q_norm"], w["wq"], w["kv_norm"], w["wkv"], w["gbias"])


def _attn_scores(qT_ref, k_ref, step, s_ref):
    qs, c = step

    def head(h):
        s_ref[h] = _dot(k_ref[c, h], qT_ref[qs, h * HEAD_PAD:(h + 1) * HEAD_PAD, :])
    return head


def _attn_update(vT_ref, step, s_ref, m_sc, l_sc, acc_sc, scores_head):
    qs, c = step
    m_all = m_sc[qs]
    l_all = l_sc[qs]
    kc = s_ref.shape[1]
    half = kc // 2
    ones = jnp.ones((2 * SUBLANES, kc), bf16)
    ms, ls = [], []

    def value_product(h, p, alpha):
        v_aug = jnp.concatenate([vT_ref[c, h * V_DIM:(h + 1) * V_DIM, :], ones], axis=0)
        pv = _dot(v_aug, p)
        ls.append(alpha * l_all[h:h + 1, :] + pv[V_DIM:V_DIM + 1, :])
        acc_sc[qs, h * V_DIM:(h + 1) * V_DIM, :] = (alpha * acc_sc[qs, h * V_DIM:(h + 1) * V_DIM, :]
                                                    + pv[0:V_DIM, :])

    pending = []
    for h in range(H_A):
        if scores_head is not None:
            scores_head(h)
        if PV_LAG and len(pending) == PV_LAG:
            value_product(*pending.pop(0))
        m_old = m_all[h:h + 1, :]
        m_new = jnp.maximum(m_old, jnp.max(s_ref[h], axis=0, keepdims=True))
        p = jnp.concatenate([jnp.exp2((s_ref[h, 0:half, :] - m_new).astype(bf16)),
                             jnp.exp2((s_ref[h, half:kc, :] - m_new).astype(bf16))], axis=0)
        ms.append(m_new)
        pending.append((h, p, jnp.exp2(m_old - m_new)))
        if not PV_LAG:
            value_product(*pending.pop(0))
    for item in pending:
        value_product(*item)
    m_sc[qs] = jnp.concatenate(ms, axis=0)
    l_sc[qs] = jnp.concatenate(ls, axis=0)


def _attn_kernel(qT_ref, k_ref, vT_ref, kt_ref, vtt_ref, o_ref, m_sc, l_sc, acc_sc, *s_bufs, single_kv_block):
    ki = pl.program_id(2)
    nqs, _, tq = qT_ref.shape
    nkc = vT_ref.shape[0]

    def init():
        zk = jnp.zeros((N_META, HEAD_PAD), bf16)
        k_bd = jnp.concatenate(
            [jnp.concatenate([kt_ref[0, h, 0:N_META, :] if j == h else zk for j in range(H_A)], axis=1)
             for h in range(H_A)], axis=0)
        rr = lax.broadcasted_iota(jnp.int32, (TAIL, H_A * N_META), 0)
        rc = lax.broadcasted_iota(jnp.int32, (TAIL, H_A * N_META), 1)
        rep = jnp.where((rr < N_META) & (rc % N_META == rr), 1.0, 0.0).astype(bf16)
        hv = lax.broadcasted_iota(jnp.int32, (H_A * V_DIM, H_A * N_META), 0) // V_DIM
        cv = lax.broadcasted_iota(jnp.int32, (H_A * V_DIM, H_A * N_META), 1) // N_META
        v_bd = jnp.where(hv == cv, _dot(vtt_ref[0], rep), 0.0).astype(bf16)
        for qs in range(nqs):
            s = _dot(k_bd, qT_ref[qs])
            ms, ls, ps = [], [], []
            for h in range(H_A):
                sh = s[h * N_META:(h + 1) * N_META, :]
                m = jnp.max(sh, axis=0, keepdims=True)
                p = jnp.exp2(sh - m)
                ms.append(m)
                ls.append(jnp.sum(p, axis=0, keepdims=True))
                ps.append(p)
            m_sc[qs] = jnp.concatenate(ms, axis=0)
            l_sc[qs] = jnp.concatenate(ls, axis=0)
            acc_sc[qs] = _dot(v_bd, jnp.concatenate(ps, axis=0).astype(bf16))

    def finalize():
        for qs in range(nqs):
            inv = 1.0 / l_sc[qs]
            outs = [acc_sc[qs, h * V_DIM:(h + 1) * V_DIM, :] * inv[h:h + 1, :] for h in range(H_A)]
            o = jnp.concatenate(outs, axis=0)
            o_ref[qs * tq:(qs + 1) * tq, :] = o.T.astype(bf16)

    if single_kv_block:
        init()
    else:
        pl.when(ki == 0)(init)

    bufs = tuple(s_bufs)
    for d in range(ATTN_LOOKAHEAD):
        first = _attn_scores(qT_ref, k_ref, (d // nkc, d % nkc), bufs[d % len(bufs)])
        for h in range(H_A):
            first(h)

    def group(qs, last):
        for c in range(nkc):
            ahead = c + ATTN_LOOKAHEAD
            nxt = None if (last and ahead >= nkc) else _attn_scores(
                qT_ref, k_ref, (qs + ahead // nkc, ahead % nkc), bufs[ahead % len(bufs)])
            _attn_update(vT_ref, (qs, c), bufs[c % len(bufs)], m_sc, l_sc, acc_sc, nxt)

    def body(qs, carry):
        group(qs, False)
        return carry
    lax.fori_loop(0, nqs - 1, body, 0)
    group(nqs - 1, True)

    if single_kv_block:
        finalize()
    else:
        pl.when(ki == pl.num_programs(2) - 1)(finalize)


def _attention(qT, k, vT, k_tail, vT_tail, *, batch, seq, tqb, tkb, q_is_tail):
    tq = qT.shape[2]
    kc = vT.shape[2]
    nqb = (TAIL if q_is_tail else seq) // tqb
    nkb = seq // tkb
    nqs = tqb // tq
    assert (tkb // kc) % ATTN_RING == 0, "a step's score buffer is its key chunk index modulo the ring size"
    assert ATTN_LOOKAHEAD < ATTN_RING and (nqs > 1 or ATTN_LOOKAHEAD <= tkb // kc)
    q_rows = qT.shape[0] * tq
    in_specs = [
        pl.BlockSpec((nqs, H_A * HEAD_PAD, tq), lambda b, qi, ki: (b * nqb + qi, 0, 0)),
        pl.BlockSpec((tkb // kc, H_A, kc, HEAD_PAD), lambda b, qi, ki: (b * nkb + ki, 0, 0, 0)),
        pl.BlockSpec((tkb // kc, H_A * V_DIM, kc), lambda b, qi, ki: (b * nkb + ki, 0, 0)),
        pl.BlockSpec((1, H_A, TAIL, HEAD_PAD), lambda b, qi, ki: (b, 0, 0, 0)),
        pl.BlockSpec((1, H_A * V_DIM, TAIL), lambda b, qi, ki: (b, 0, 0)),
    ]
    return pl.pallas_call(
        functools.partial(_attn_kernel, single_kv_block=(nkb == 1)),
        grid=(batch, nqb, nkb), in_specs=in_specs,
        out_specs=pl.BlockSpec((tqb, H_A * V_DIM), lambda b, qi, ki: (b * nqb + qi, 0)),
        out_shape=jax.ShapeDtypeStruct((q_rows, H_A * V_DIM), bf16),
        scratch_shapes=[pltpu.VMEM((nqs, H_A, tq), f32), pltpu.VMEM((nqs, H_A, tq), f32),
                        pltpu.VMEM((nqs, H_A * V_DIM, tq), f32),
                        ] + [pltpu.VMEM((H_A, kc, tq), f32)] * ATTN_RING,
        compiler_params=_params(("parallel", "parallel", "arbitrary")), name="attention",
    )(qT, k, vT, k_tail, vT_tail)


def _mlstm_gates(gT, *, reverse):
    r = lax.broadcasted_iota(jnp.int32, (CHUNK, CHUNK), 0)
    c = lax.broadcasted_iota(jnp.int32, (CHUNK, CHUNK), 1)
    tri = jnp.where((r >= c) if reverse else (r <= c), 1.0, 0.0).astype(bf16)
    a = gT.astype(bf16)
    rem = gT - a.astype(f32)
    b = rem.astype(bf16)
    x = _dot(a, tri) + _dot(b, tri) + _dot((rem - b.astype(f32)).astype(bf16), tri)
    padded = jnp.concatenate([x, jnp.zeros((LANES - x.shape[0], CHUNK), f32)], axis=0)
    return padded.T, x


def _mlstm_head(h, qk, cums, q, kT, v, gT, c_sc, m_sc, *, reverse):
    d = 1 if reverse else 0
    bcol_all, brow_all = cums
    r = lax.broadcasted_iota(jnp.int32, (CHUNK, CHUNK), 0)
    c = lax.broadcasted_iota(jnp.int32, (CHUNK, CHUNK), 1)
    mask = (c >= r) if reverse else (c <= r)
    ones_blk = jnp.ones((CHUNK, LANES), bf16)
    tile_shape = (SUBLANES, LANES)
    ji = d * H_B + h
    jf = 2 * H_B + d * H_B + h
    bcb = jnp.broadcast_to(bcol_all[:, jf:jf + 1], (CHUNK, LANES))
    br = brow_all[jf:jf + 1, :]
    a_row = gT[ji:ji + 1, :] - br
    gtot = jnp.broadcast_to(br[:, 0:1] if reverse else br[:, CHUNK - 1:CHUNK], tile_shape)
    m_prev = m_sc[h]
    mpb = jnp.concatenate([m_prev] * (CHUNK // SUBLANES), axis=0)
    qh = q[:, h * DH_B:(h + 1) * DH_B]
    kTh = kT[h * DH_B:(h + 1) * DH_B, :]
    v_aug = jnp.concatenate([v[:, h * DH_B:(h + 1) * DH_B], ones_blk], axis=1)
    amask = jnp.where(mask, a_row, -jnp.inf)
    mt = jnp.maximum(jnp.broadcast_to(jnp.max(amask, axis=1, keepdims=True), (CHUNK, LANES)), mpb)
    s = (qk * jnp.exp(amask - mt)).astype(bf16)
    e_in = jnp.exp(mpb - mt)
    c_prev = c_sc[h]
    numden = _dot(s, v_aug) + jnp.concatenate([e_in, e_in], axis=1) * _dot(qh, c_prev.astype(bf16))
    den = numden[:, DH_B:]
    out = numden[:, :DH_B] * (1.0 / jnp.maximum(jnp.abs(den), jnp.exp(-(bcb + mt))))
    a_end = gtot[0:1, :] + a_row
    m_loc = jnp.broadcast_to(jnp.max(a_end, axis=1, keepdims=True), tile_shape)
    m_new = jnp.maximum(gtot + m_prev, m_loc)
    sp = jnp.exp(gtot + m_prev - m_new)
    kw = (kTh.astype(f32) * jnp.exp(a_end - m_new[0:1, :])).astype(bf16)
    rep = lambda x: jnp.concatenate([jnp.concatenate([x] * (DH_B // SUBLANES), axis=0)] * 2, axis=1)
    c_sc[h] = rep(sp) * c_prev + _dot(kw, v_aug)
    m_sc[h] = m_new
    return out


def _mlstm_chunks(loaders, stores, c_sc, m_sc, *, reverse):
    qk_head = lambda q, kT, h: _dot(q[:, h * DH_B:(h + 1) * DH_B], kT[h * DH_B:(h + 1) * DH_B, :])

    def prep_all(j):
        data = loaders[j]()
        return [data, _mlstm_gates(data[3], reverse=reverse), [qk_head(data[0], data[1], h) for h in range(H_B)]]

    ready = [prep_all(j) for j in range(min(MLSTM_LOOKAHEAD, len(loaders)))]
    for i, store in enumerate(stores):
        (q, kT, v, gT), cums, qks = ready.pop(0)
        j = i + MLSTM_LOOKAHEAD
        nxt = None
        if j < len(loaders):
            data = loaders[j]()
            nxt = [data, _mlstm_gates(data[3], reverse=reverse), []]
            ready.append(nxt)
        outs = []
        for h in range(H_B):
            if nxt is not None:
                nxt[2].append(qk_head(nxt[0][0], nxt[0][1], h))
            outs.append(_mlstm_head(h, qks[h], cums, q, kT, v, gT, c_sc, m_sc, reverse=reverse))
        store(jnp.concatenate(outs, axis=1))


def _mlstm_kernel(q_ref, kT_ref, v_ref, gT_ref, qt_ref, kTt_ref, vt_ref, gTt_ref, *rest, reverse):
    if reverse:
        hf_ref, hft_ref, o_ref, ot_ref, c_sc, m_sc = rest
    else:
        o_ref, ot_ref, c_sc, m_sc = rest
    j = pl.program_id(1)
    nj = pl.num_programs(1)
    nch = kT_ref.shape[0]
    run = functools.partial(_mlstm_chunks, c_sc=c_sc, m_sc=m_sc, reverse=reverse)

    def tail_chunk():
        def store(o):
            ot_ref[...] = ((o + hft_ref[...].astype(f32)) if reverse else o).astype(bf16)
        run([lambda: (qt_ref[...], kTt_ref[0], vt_ref[...], gTt_ref[0])], [store])

    @pl.when(j == 0)
    def _():
        c_sc[...] = jnp.zeros(c_sc.shape, f32)
        m_sc[...] = jnp.full(m_sc.shape, NEG, f32)
        if not reverse:
            tail_chunk()

    def loader(c):
        rows = slice(c * CHUNK, (c + 1) * CHUNK)
        return lambda: (q_ref[rows, :], kT_ref[c], v_ref[rows, :], gT_ref[c])

    def storer(c):
        rows = slice(c * CHUNK, (c + 1) * CHUNK)

        def store(o):
            o_ref[rows, :] = ((o + hf_ref[rows, :].astype(f32)) if reverse else o).astype(bf16)
        return store

    order = list(range(nch - 1, -1, -1) if reverse else range(nch))
    run([loader(c) for c in order], [storer(c) for c in order])

    if reverse:
        @pl.when(j == nj - 1)
        def _():
            tail_chunk()


def _mlstm(x_arrs, t_arrs, fwd_out=None, *, batch, seq, rb):
    reverse = fwd_out is not None
    nj = seq // rb
    nch = rb // CHUNK
    blk = (lambda b, j: b * nj + (nj - 1 - j)) if reverse else (lambda b, j: b * nj + j)
    in_specs = [
        pl.BlockSpec((rb, H_B * DH_B), lambda b, j: (blk(b, j), 0)),
        pl.BlockSpec((nch, H_B * DH_B, CHUNK), lambda b, j: (blk(b, j), 0, 0)),
        pl.BlockSpec((rb, H_B * DH_B), lambda b, j: (blk(b, j), 0)),
        pl.BlockSpec((nch, 4 * H_B, CHUNK), lambda b, j: (blk(b, j), 0, 0)),
        pl.BlockSpec((TAIL, H_B * DH_B), lambda b, j: (b, 0)),
        pl.BlockSpec((1, H_B * DH_B, CHUNK), lambda b, j: (b, 0, 0)),
        pl.BlockSpec((TAIL, H_B * DH_B), lambda b, j: (b, 0)),
        pl.BlockSpec((1, 4 * H_B, CHUNK), lambda b, j: (b, 0, 0)),
    ]
    out_specs = (pl.BlockSpec((rb, H_B * DH_B), lambda b, j: (blk(b, j), 0)),
                 pl.BlockSpec((TAIL, H_B * DH_B), lambda b, j: (b, 0)))
    out_shape = (jax.ShapeDtypeStruct((batch * seq, H_B * DH_B), bf16),
                 jax.ShapeDtypeStruct((batch * TAIL, H_B * DH_B), bf16))
    operands = tuple(x_arrs) + tuple(t_arrs)
    if reverse:
        in_specs = in_specs + list(out_specs)
        operands = operands + tuple(fwd_out)
    return pl.pallas_call(
        functools.partial(_mlstm_kernel, reverse=reverse),
        grid=(batch, nj), in_specs=in_specs, out_specs=out_specs, out_shape=out_shape,
        scratch_shapes=[pltpu.VMEM((H_B, DH_B, 2 * DH_B), f32), pltpu.VMEM((H_B, SUBLANES, LANES), f32)],
        compiler_params=_params(("parallel", "arbitrary")), name="mlstm_bwd" if reverse else "mlstm_fwd",
    )(*operands)


def _merge_kernel(attn_ref, hm_ref, so_ref, sga_ref, sgb_ref, res_ref, mn_ref, woa_ref, wom_ref, wout_ref, o_ref):
    hm = hm_ref[...].astype(f32)
    mn = mn_ref[...]
    parts = [_rms(hm[:, h * DH_B:(h + 1) * DH_B], mn[:, h * DH_B:(h + 1) * DH_B]) for h in range(H_B)]
    hmn = (jnp.concatenate(parts, axis=1) * so_ref[...].astype(f32)).astype(bf16)
    y = (sga_ref[...].astype(f32) * _dot(attn_ref[...], woa_ref[...])
         + sgb_ref[...].astype(f32) * _dot(hmn, wom_ref[...]))
    o_ref[...] = res_ref[...] + _dot(y.astype(bf16), wout_ref[...])


def _merge(attn, hm, so, sga, sgb, res, w, *, tm):
    rows = attn.shape[0]
    row_spec = lambda width: pl.BlockSpec((tm, width), lambda i: (i, 0))
    res_spec = row_spec(D_MODEL) if res.shape[0] == rows else pl.BlockSpec((tm, D_MODEL), lambda i: (0, 0))
    in_specs = [row_spec(H_A * V_DIM), row_spec(H_B * DH_B), row_spec(H_B * DH_B),
                row_spec(D_MODEL), row_spec(D_MODEL), res_spec,
                _const_spec((1, H_B * DH_B)), _const_spec(w["woa"].shape), _const_spec(w["wom"].shape),
                _const_spec(w["wout"].shape)]
    return pl.pallas_call(
        _merge_kernel, grid=(rows // tm,), in_specs=in_specs, out_specs=row_spec(D_MODEL),
        out_shape=jax.ShapeDtypeStruct((rows, D_MODEL), f32),
        compiler_params=_params(("parallel",)), name="merge",
    )(attn, hm, so, sga, sgb, res, w["mlstm_norm"], w["woa"], w["wom"], w["wout"])


def _gelu_tanh(x):
    return 0.5 * x * (1.0 + jnp.tanh(math.sqrt(2.0 / math.pi) * (x + 0.044715 * (x * x * x))))


def _ffn_kernel(h_ref, prev_ref, prevt_ref, next_ref, nf_ref, wup_ref, wgate_ref, cw_ref, cb_ref, wdown_ref,
                nfin_ref, o_ref, *, tiles_per_batch):
    i = pl.program_id(0)
    tm = h_ref.shape[0]
    first = (i % tiles_per_batch) == 0
    last = (i % tiles_per_batch) == tiles_per_batch - 1
    nf = nf_ref[...]
    h = h_ref[...]
    prev = jnp.where(first, prevt_ref[...], prev_ref[...])
    u_ext = _rms(jnp.concatenate([prev, h, next_ref[...]], axis=0), nf)
    u = u_ext[SUBLANES:tm + SUBLANES, :].astype(bf16)
    u_ext = u_ext.astype(bf16)
    row = lax.broadcasted_iota(jnp.int32, (tm, 1), 0)

    def up_gate(lo, hi):
        return _dot(u_ext, wup_ref[:, lo:hi]), _dot(u, wgate_ref[:, lo:hi])

    acc = h
    ahead = [up_gate(*FF_CHUNKS[j]) for j in range(FF_LOOKAHEAD)]
    for ci, (lo, hi) in enumerate(FF_CHUNKS):
        if ci + FF_LOOKAHEAD < len(FF_CHUNKS):
            ahead.append(up_gate(*FF_CHUNKS[ci + FF_LOOKAHEAD]))
        a_ext, gate = ahead.pop(0)
        a = a_ext[SUBLANES:tm + SUBLANES, :]
        a_prev = a_ext[SUBLANES - 1:SUBLANES, :]
        a_next = jnp.where(last, 0.0, a_ext[tm + SUBLANES:tm + SUBLANES + 1, :])
        a_dn = jnp.where(row == 0, a_prev, pltpu.roll(a, 1, axis=0))
        a_up = jnp.where(row == tm - 1, a_next, pltpu.roll(a, tm - 1, axis=0))
        cw = cw_ref[:, lo:hi]
        conv = a_dn * cw[0:1, :] + a * cw[1:2, :] + a_up * cw[2:3, :] + cb_ref[:, lo:hi]
        gated = (_gelu_tanh(conv) * gate).astype(bf16)
        acc = acc + _dot(gated, wdown_ref[lo:hi, :])
    o_ref[...] = _rms(acc, nfin_ref[...])


def _ffn(h1, h1_tail, w, *, seq, tm):
    rows = h1.shape[0]
    n = rows // tm
    tpb = seq // tm
    hb = tm // SUBLANES
    in_specs = [
        pl.BlockSpec((tm, D_MODEL), lambda i: (i, 0)),
        pl.BlockSpec((SUBLANES, D_MODEL), lambda i: (jnp.maximum(i * hb - 1, 0), 0)),
        pl.BlockSpec((SUBLANES, D_MODEL), lambda i: ((i // tpb) * (TAIL // SUBLANES) + N_META // SUBLANES - 1, 0)),
        pl.BlockSpec((SUBLANES, D_MODEL), lambda i: (jnp.minimum((i + 1) * hb, n * hb - 1), 0)),
        _const_spec((1, D_MODEL)), _const_spec(w["wup"].shape), _const_spec(w["wgate"].shape),
        _const_spec((SUBLANES, D_FF)), _const_spec((1, D_FF)), _const_spec(w["wdown"].shape),
        _const_spec((1, D_MODEL)),
    ]
    return pl.pallas_call(
        functools.partial(_ffn_kernel, tiles_per_batch=tpb),
        grid=(n,), in_specs=in_specs, out_specs=pl.BlockSpec((tm, D_MODEL), lambda i: (i, 0)),
        out_shape=jax.ShapeDtypeStruct((rows, D_MODEL), f32),
        compiler_params=_params(("parallel",)), name="conv_ffn",
    )(h1, h1, h1_tail, h1, w["norm_ffn"], w["wup"], w["wgate"], w["conv_w"], w["conv_b"], w["wdown"],
      w["norm_final"])


def _prep_weights(norm_mix, w_in, q_norm, w_uq, kv_norm, w_ukv, b_igate, b_fgate, mlstm_norm, w_o_attn, w_o_mlstm,
                  w_out, norm_ffn, w_up, w_gate, conv_w, conv_b, w_down, norm_final):
    w = w_in[0]
    offs = [0]
    for s in IN_SIZES:
        offs.append(offs[-1] + s)
    col = lambda j: w[:, offs[j]:offs[j + 1]]
    zeros = lambda r, c: jnp.zeros((r, c), f32)
    w_kr = col(2)
    half = ROPE // 2
    kr_main = jnp.concatenate([zeros(D_MODEL, NOPE), w_kr, zeros(D_MODEL, HEAD_PAD - QK_DIM)], axis=1)
    kr_rot = jnp.concatenate([zeros(D_MODEL, NOPE), -w_kr[:, half:], w_kr[:, :half],
                              zeros(D_MODEL, HEAD_PAD - QK_DIM)], axis=1)
    wa = jnp.concatenate([col(0), col(1), kr_main, kr_rot], axis=1)
    wb = jnp.concatenate([col(3), col(4), col(5), col(6), col(7), col(8), zeros(D_MODEL, LANES - 4 * H_B)], axis=1)
    wc = jnp.concatenate([col(9), col(10)], axis=1)
    wuq = w_uq[0].reshape(Q_LORA, H_A, QK_DIM)
    nope, rope = wuq[:, :, :NOPE], wuq[:, :, NOPE:]
    zq = lambda c: jnp.zeros((Q_LORA, H_A, c), f32)
    q_main = jnp.concatenate([nope, rope, zq(HEAD_PAD - QK_DIM)], axis=-1).reshape(Q_LORA, H_A * HEAD_PAD)
    q_rot = jnp.concatenate([zq(NOPE), -rope[:, :, half:], rope[:, :, :half], zq(HEAD_PAD - QK_DIM)],
                            axis=-1).reshape(Q_LORA, H_A * HEAD_PAD)
    wq = jnp.concatenate([q_main, q_rot], axis=1)
    wukv = w_ukv[0].reshape(KV_LORA, H_A, NOPE + V_DIM)
    wk = jnp.concatenate([wukv[:, :, :NOPE], jnp.zeros((KV_LORA, H_A, HEAD_PAD - NOPE), f32)],
                         axis=-1).reshape(KV_LORA, H_A * HEAD_PAD)
    wv = wukv[:, :, NOPE:].reshape(KV_LORA, H_A * V_DIM)
    wkv = jnp.concatenate([wk, wv], axis=1)
    gbias = jnp.concatenate([b_igate[0].reshape(-1), b_fgate[0].reshape(-1), jnp.zeros((LANES - 4 * H_B,), f32)])
    cw = jnp.concatenate([conv_w[0], jnp.zeros((SUBLANES - CONV_W, D_FF), f32)], axis=0)
    return dict(
        norm_mix=norm_mix[0][None], wa=wa.astype(bf16), wb=wb.astype(bf16), wc=wc.astype(bf16),
        q_norm=q_norm[0][None], wq=wq.astype(bf16), kv_norm=kv_norm[0][None], wkv=wkv.astype(bf16),
        gbias=gbias[None], mlstm_norm=mlstm_norm[0][None], woa=w_o_attn[0].astype(bf16),
        wom=w_o_mlstm[0].astype(bf16), wout=w_out[0].astype(bf16), norm_ffn=norm_ffn[0][None],
        wup=w_up[0].astype(bf16), wgate=w_gate[0].astype(bf16), conv_w=cw, conv_b=conv_b[0][None],
        wdown=w_down[0].astype(bf16), norm_final=norm_final[None])


def _rope_tables(positions):
    inv = ROPE_THETA ** (-jnp.arange(0, ROPE, 2, dtype=f32) / ROPE)
    ang = positions.astype(f32)[:, None] * inv[None, :]
    n = positions.shape[0]
    cos, sin = jnp.cos(ang), jnp.sin(ang)
    cm = jnp.concatenate([jnp.ones((n, NOPE), f32), cos, cos, jnp.zeros((n, HEAD_PAD - QK_DIM), f32)], axis=1)
    sm = jnp.concatenate([jnp.zeros((n, NOPE), f32), sin, sin, jnp.zeros((n, HEAD_PAD - QK_DIM), f32)], axis=1)
    return cm, sm


def _tiles(batch, seq):
    tm = math.gcd(seq, ROW_TILE)
    tm_tail = math.gcd(batch * TAIL, ROW_TILE)
    tkb = math.gcd(seq, 2048)
    tqb = math.gcd(seq, 2048)
    return dict(tm=tm, tm_wide=math.gcd(seq, 1024), tm_tail=tm_tail, tqb=tqb, tkb=tkb, rb=math.gcd(seq, 2048))


def _position_tables(max_seq):
    return _rope_tables(N_META + jnp.arange(max_seq)) + _rope_tables(jnp.tile(jnp.arange(TAIL), ROW_TILE // TAIL))


def _encode(x, meta_tokens, w, tables):
    batch, seq, _ = x.shape
    t = _tiles(batch, seq)
    xf = x.reshape(batch * seq, D_MODEL)
    tail_one = jnp.concatenate([meta_tokens.astype(f32), jnp.zeros((TAIL - N_META, D_MODEL), f32)], axis=0)
    tail_tile = jnp.tile(tail_one, (t["tm_tail"] // TAIL, 1))

    cm_x, sm_x, cm_t, sm_t = tables
    nper = seq // t["tm"]
    xs = _inproj(xf, cm_x, sm_x, lambda i: (i % nper, 0), w, rows=batch * seq, tm=t["tm"], cq=TQ, ck=KC,
                 mask_tail=False)
    ts = _inproj(tail_tile, cm_t, sm_t, lambda i: (0, 0), w, rows=batch * TAIL, tm=t["tm_tail"], cq=TAIL, ck=TAIL,
                 mask_tail=True)
    qT, k, vT, mq, mkT, mv, so, gT, sga, sgb = xs
    qT_t, k_t, vT_t, mq_t, mkT_t, mv_t, so_t, gT_t, sga_t, sgb_t = ts

    attn = _attention(qT, k, vT, k_t, vT_t, batch=batch, seq=seq, tqb=t["tqb"], tkb=t["tkb"], q_is_tail=False)
    attn_t = _attention(qT_t, k, vT, k_t, vT_t, batch=batch, seq=seq, tqb=TAIL, tkb=t["tkb"], q_is_tail=True)

    x_arrs = (mq, mkT, mv, gT)
    t_arrs = (mq_t, mkT_t, mv_t, gT_t)
    fwd = _mlstm(x_arrs, t_arrs, batch=batch, seq=seq, rb=t["rb"])
    hm, hm_t = _mlstm(x_arrs, t_arrs, fwd, batch=batch, seq=seq, rb=t["rb"])

    h1 = _merge(attn, hm, so, sga, sgb, xf, w, tm=t["tm_wide"])
    h1_t = _merge(attn_t, hm_t, so_t, sga_t, sgb_t, tail_tile, w, tm=t["tm_tail"])
    y = _ffn(h1, h1_t, w, seq=seq, tm=t["tm_wide"])
    return y.reshape(batch, seq, D_MODEL)


def kernel(x_prompt, x_sample, meta_tokens, norm_mix, w_in, q_norm, w_uq, kv_norm, w_ukv, b_igate, b_fgate,
           mlstm_norm, w_o_attn, w_o_mlstm, w_out, norm_ffn, w_up, w_gate, conv_w, conv_b, w_down, norm_final):
    w = _prep_weights(norm_mix, w_in, q_norm, w_uq, kv_norm, w_ukv, b_igate, b_fgate, mlstm_norm, w_o_attn,
                      w_o_mlstm, w_out, norm_ffn, w_up, w_gate, conv_w, conv_b, w_down, norm_final)
    tables = _position_tables(max(x_prompt.shape[1], x_sample.shape[1]))
    return (_encode(x_prompt, meta_tokens, w, tables), _encode(x_sample, meta_tokens, w, tables))
```

```python
import functools
import math

import jax
import jax.numpy as jnp
from jax import lax
from jax.experimental import pallas as pl
from jax.experimental.pallas import tpu as pltpu

D_MODEL = 1024
N_META = 16
H_A = 8
Q_LORA = 256
KV_LORA = 128
NOPE = 64
ROPE = 32
QK_DIM = NOPE + ROPE
V_DIM = 64
ROPE_THETA = 10000.0
H_B = 4
DH_B = 128
D_FF = 2816
CONV_W = 3
EPS = 1e-6
NEG = -1e30
IN_SIZES = (Q_LORA, KV_LORA, ROPE, H_B * DH_B, H_B * DH_B, H_B * DH_B, H_B * DH_B, 2 * H_B, 2 * H_B, D_MODEL, D_MODEL)

LANES = 128
SUBLANES = 8
TAIL = 128
HEAD_PAD = LANES
CHUNK = 128
TQ = 256
KC = 256
MLSTM_LOOKAHEAD = 2
ATTN_RING = 4
ATTN_LOOKAHEAD = 2
ROW_TILE = 512
PV_LAG = 1
VMEM_LIMIT = 56 * 1024 * 1024
QSCALE = (QK_DIM ** -0.5) * math.log2(math.e)
FF_CHUNKS = tuple((lo, min(lo + 512, D_FF)) for lo in range(0, D_FF, 512))
FF_LOOKAHEAD = 4

f32 = jnp.float32
bf16 = jnp.bfloat16


def _rms(x, g):
    return x * lax.rsqrt(jnp.mean(x * x, axis=-1, keepdims=True) + EPS) * g


def _sigmoid(x):
    return 1.0 / (1.0 + jnp.exp(-x))


def _log_sigmoid(x):
    return jnp.minimum(x, 0.0) - jnp.log1p(jnp.exp(-jnp.abs(x)))


def _dot(a, b):
    return jnp.dot(a, b, preferred_element_type=f32)


def _const_spec(shape):
    zeros = (0,) * len(shape)
    return pl.BlockSpec(shape, lambda *_: zeros, pipeline_mode=pl.Buffered(1))


def _params(sem):
    return pltpu.CompilerParams(dimension_semantics=sem, vmem_limit_bytes=VMEM_LIMIT)


def _inproj_kernel(h_ref, cm_ref, sm_ref, nmix_ref, wa_ref, wb_ref, wc_ref, qn_ref, wq_ref, kvn_ref, wkv_ref,
                   gb_ref, qT_ref, k_ref, vT_ref, mq_ref, mkT_ref, mv_ref, so_ref, gT_ref, sga_ref, sgb_ref,
                   *, cq, ck, mask_tail):
    tm = h_ref.shape[0]
    u = _rms(h_ref[...], nmix_ref[...]).astype(bf16)
    cm = cm_ref[...]
    sm = sm_ref[...]

    za = _dot(u, wa_ref[...])
    qn = _rms(za[:, :Q_LORA], qn_ref[...]).astype(bf16)
    kvn = _rms(za[:, Q_LORA:Q_LORA + KV_LORA], kvn_ref[...]).astype(bf16)
    o_kr = Q_LORA + KV_LORA
    kr = za[:, o_kr:o_kr + HEAD_PAD] * cm + za[:, o_kr + HEAD_PAD:o_kr + 2 * HEAD_PAD] * sm
    cm8 = jnp.concatenate([cm] * H_A, axis=1)
    sm8 = jnp.concatenate([sm] * H_A, axis=1)
    q = (_dot(qn, wq_ref[:, :H_A * HEAD_PAD]) * cm8 + _dot(qn, wq_ref[:, H_A * HEAD_PAD:]) * sm8) * QSCALE
    qT = q.T
    for c in range(tm // cq):
        qT_ref[c] = qT[:, c * cq:(c + 1) * cq].astype(bf16)
    kv = _dot(kvn, wkv_ref[...])
    for h in range(H_A):
        kh = (kv[:, h * HEAD_PAD:(h + 1) * HEAD_PAD] + kr).astype(bf16)
        for c in range(tm // ck):
            k_ref[c, h] = kh[c * ck:(c + 1) * ck, :]
    vT = kv[:, H_A * HEAD_PAD:].T
    for c in range(tm // ck):
        vT_ref[c] = vT[:, c * ck:(c + 1) * ck].astype(bf16)

    zb = _dot(u, wb_ref[...])
    wm = H_B * DH_B
    mq_ref[...] = zb[:, 0:wm].astype(bf16)
    mkT = (zb[:, wm:2 * wm] * (DH_B ** -0.5)).T
    for c in range(tm // CHUNK):
        mkT_ref[c] = mkT[:, c * CHUNK:(c + 1) * CHUNK].astype(bf16)
    mv_ref[...] = zb[:, 2 * wm:3 * wm].astype(bf16)
    so_ref[...] = _sigmoid(zb[:, 3 * wm:4 * wm]).astype(bf16)
    gz = zb[:, 4 * wm:4 * wm + LANES] + gb_ref[...]
    lane = lax.broadcasted_iota(jnp.int32, gz.shape, 1)
    gates = jnp.where(lane < 2 * H_B, gz, _log_sigmoid(gz))
    if mask_tail:
        row = lax.broadcasted_iota(jnp.int32, gz.shape, 0)
        pad = (row % TAIL) >= N_META
        gates = jnp.where(pad, jnp.where(lane < 2 * H_B, NEG, 0.0), gates)
    gT = gates.T
    for c in range(tm // CHUNK):
        gT_ref[c] = gT[:4 * H_B, c * CHUNK:(c + 1) * CHUNK]

    zc = _dot(u, wc_ref[...])
    sga_ref[...] = _sigmoid(zc[:, :D_MODEL]).astype(bf16)
    sgb_ref[...] = _sigmoid(zc[:, D_MODEL:]).astype(bf16)


def _inproj(h, cm, sm, tab_map, w, *, rows, tm, cq, ck, mask_tail):
    n = rows // tm
    row_spec = lambda width: pl.BlockSpec((tm, width), lambda i: (i, 0))
    h_spec = row_spec(D_MODEL) if h.shape[0] == rows else pl.BlockSpec((tm, D_MODEL), lambda i: (0, 0))
    chunk_spec = lambda height, cw: pl.BlockSpec((tm // cw, height, cw), lambda i: (i, 0, 0))
    out_shape = (
        jax.ShapeDtypeStruct((rows // cq, H_A * HEAD_PAD, cq), bf16),
        jax.ShapeDtypeStruct((rows // ck, H_A, ck, HEAD_PAD), bf16),
        jax.ShapeDtypeStruct((rows // ck, H_A * V_DIM, ck), bf16),
        jax.ShapeDtypeStruct((rows, H_B * DH_B), bf16),
        jax.ShapeDtypeStruct((rows // CHUNK, H_B * DH_B, CHUNK), bf16),
        jax.ShapeDtypeStruct((rows, H_B * DH_B), bf16),
        jax.ShapeDtypeStruct((rows, H_B * DH_B), bf16),
        jax.ShapeDtypeStruct((rows // CHUNK, 4 * H_B, CHUNK), f32),
        jax.ShapeDtypeStruct((rows, D_MODEL), bf16),
        jax.ShapeDtypeStruct((rows, D_MODEL), bf16),
    )
    out_specs = (
        chunk_spec(H_A * HEAD_PAD, cq),
        pl.BlockSpec((tm // ck, H_A, ck, HEAD_PAD), lambda i: (i, 0, 0, 0)), chunk_spec(H_A * V_DIM, ck),
        row_spec(H_B * DH_B), chunk_spec(H_B * DH_B, CHUNK), row_spec(H_B * DH_B), row_spec(H_B * DH_B),
        chunk_spec(4 * H_B, CHUNK), row_spec(D_MODEL), row_spec(D_MODEL),
    )
    in_specs = [
        h_spec,
        pl.BlockSpec((tm, LANES), tab_map), pl.BlockSpec((tm, LANES), tab_map),
        _const_spec((1, D_MODEL)), _const_spec(w["wa"].shape), _const_spec(w["wb"].shape),
        _const_spec(w["wc"].shape), _const_spec((1, Q_LORA)), _const_spec(w["wq"].shape),
        _const_spec((1, KV_LORA)), _const_spec(w["wkv"].shape), _const_spec((1, LANES)),
    ]
    return pl.pallas_call(
        functools.partial(_inproj_kernel, cq=cq, ck=ck, mask_tail=mask_tail),
        grid=(n,), in_specs=in_specs, out_specs=out_specs, out_shape=out_shape,
        compiler_params=_params(("parallel",)), name="inproj",
    )(h, cm, sm, w["norm_mix"], w["wa"], w["wb"], w["wc"], w["q_norm"], w["wq"], w["kv_norm"], w["wkv"], w["gbias"])


def _attn_scores(qT_ref, k_ref, step, s_ref):
    qs, c = step

    def head(h):
        s_ref[h] = _dot(k_ref[c, h], qT_ref[qs, h * HEAD_PAD:(h + 1) * HEAD_PAD, :])
    return head


def _attn_update(vT_ref, step, s_ref, m_sc, l_sc, acc_sc, scores_head):
    qs, c = step
    m_all = m_sc[qs]
    l_all = l_sc[qs]
    kc = s_ref.shape[1]
    half = kc // 2
    ones = jnp.ones((2 * SUBLANES, kc), bf16)
    ms, ls = [], []

    def value_product(h, p, alpha):
        v_aug = jnp.concatenate([vT_ref[c, h * V_DIM:(h + 1) * V_DIM, :], ones], axis=0)
        pv = _dot(v_aug, p)
        ls.append(alpha * l_all[h:h + 1, :] + pv[V_DIM:V_DIM + 1, :])
        acc_sc[qs, h * V_DIM:(h + 1) * V_DIM, :] = (alpha * acc_sc[qs, h * V_DIM:(h + 1) * V_DIM, :]
                                                    + pv[0:V_DIM, :])

    pending = []
    for h in range(H_A):
        if scores_head is not None:
            scores_head(h)
        if PV_LAG and len(pending) == PV_LAG:
            value_product(*pending.pop(0))
        m_old = m_all[h:h + 1, :]
        m_new = jnp.maximum(m_old, jnp.max(s_ref[h], axis=0, keepdims=True))
        p = jnp.concatenate([jnp.exp2((s_ref[h, 0:half, :] - m_new).astype(bf16)),
                             jnp.exp2((s_ref[h, half:kc, :] - m_new).astype(bf16))], axis=0)
        ms.append(m_new)
        pending.append((h, p, jnp.exp2(m_old - m_new)))
        if not PV_LAG:
            value_product(*pending.pop(0))
    for item in pending:
        value_product(*item)
    m_sc[qs] = jnp.concatenate(ms, axis=0)
    l_sc[qs] = jnp.concatenate(ls, axis=0)


def _attn_kernel(qT_ref, k_ref, vT_ref, kt_ref, vtt_ref, o_ref, m_sc, l_sc, acc_sc, *s_bufs, single_kv_block):
    ki = pl.program_id(2)
    nqs, _, tq = qT_ref.shape
    nkc = vT_ref.shape[0]

    def init():
        zk = jnp.zeros((N_META, HEAD_PAD), bf16)
        k_bd = jnp.concatenate(
            [jnp.concatenate([kt_ref[0, h, 0:N_META, :] if j == h else zk for j in range(H_A)], axis=1)
             for h in range(H_A)], axis=0)
        rr = lax.broadcasted_iota(jnp.int32, (TAIL, H_A * N_META), 0)
        rc = lax.broadcasted_iota(jnp.int32, (TAIL, H_A * N_META), 1)
        rep = jnp.where((rr < N_META) & (rc % N_META == rr), 1.0, 0.0).astype(bf16)
        hv = lax.broadcasted_iota(jnp.int32, (H_A * V_DIM, H_A * N_META), 0) // V_DIM
        cv = lax.broadcasted_iota(jnp.int32, (H_A * V_DIM, H_A * N_META), 1) // N_META
        v_bd = jnp.where(hv == cv, _dot(vtt_ref[0], rep), 0.0).astype(bf16)
        for qs in range(nqs):
            s = _dot(k_bd, qT_ref[qs])
            ms, ls, ps = [], [], []
            for h in range(H_A):
                sh = s[h * N_META:(h + 1) * N_META, :]
                m = jnp.max(sh, axis=0, keepdims=True)
                p = jnp.exp2(sh - m)
                ms.append(m)
                ls.append(jnp.sum(p, axis=0, keepdims=True))
                ps.append(p)
            m_sc[qs] = jnp.concatenate(ms, axis=0)
            l_sc[qs] = jnp.concatenate(ls, axis=0)
            acc_sc[qs] = _dot(v_bd, jnp.concatenate(ps, axis=0).astype(bf16))

    def finalize():
        for qs in range(nqs):
            inv = 1.0 / l_sc[qs]
            outs = [acc_sc[qs, h * V_DIM:(h + 1) * V_DIM, :] * inv[h:h + 1, :] for h in range(H_A)]
            o = jnp.concatenate(outs, axis=0)
            o_ref[qs * tq:(qs + 1) * tq, :] = o.T.astype(bf16)

    if single_kv_block:
        init()
    else:
        pl.when(ki == 0)(init)

    bufs = tuple(s_bufs)
    for d in range(ATTN_LOOKAHEAD):
        first = _attn_scores(qT_ref, k_ref, (d // nkc, d % nkc), bufs[d % len(bufs)])
        for h in range(H_A):
            first(h)

    def group(qs, last):
        for c in range(nkc):
            ahead = c + ATTN_LOOKAHEAD
            nxt = None if (last and ahead >= nkc) else _attn_scores(
                qT_ref, k_ref, (qs + ahead // nkc, ahead % nkc), bufs[ahead % len(bufs)])
            _attn_update(vT_ref, (qs, c), bufs[c % len(bufs)], m_sc, l_sc, acc_sc, nxt)

    def body(qs, carry):
        group(qs, False)
        return carry
    lax.fori_loop(0, nqs - 1, body, 0)
    group(nqs - 1, True)

    if single_kv_block:
        finalize()
    else:
        pl.when(ki == pl.num_programs(2) - 1)(finalize)


def _attention(qT, k, vT, k_tail, vT_tail, *, batch, seq, tqb, tkb, q_is_tail):
    tq = qT.shape[2]
    kc = vT.shape[2]
    nqb = (TAIL if q_is_tail else seq) // tqb
    nkb = seq // tkb
    nqs = tqb // tq
    assert (tkb // kc) % ATTN_RING == 0, "a step's score buffer is its key chunk index modulo the ring size"
    assert ATTN_LOOKAHEAD < ATTN_RING and (nqs > 1 or ATTN_LOOKAHEAD <= tkb // kc)
    q_rows = qT.shape[0] * tq
    in_specs = [
        pl.BlockSpec((nqs, H_A * HEAD_PAD, tq), lambda b, qi, ki: (b * nqb + qi, 0, 0)),
        pl.BlockSpec((tkb // kc, H_A, kc, HEAD_PAD), lambda b, qi, ki: (b * nkb + ki, 0, 0, 0)),
        pl.BlockSpec((tkb // kc, H_A * V_DIM, kc), lambda b, qi, ki: (b * nkb + ki, 0, 0)),
        pl.BlockSpec((1, H_A, TAIL, HEAD_PAD), lambda b, qi, ki: (b, 0, 0, 0)),
        pl.BlockSpec((1, H_A * V_DIM, TAIL), lambda b, qi, ki: (b, 0, 0)),
    ]
    return pl.pallas_call(
        functools.partial(_attn_kernel, single_kv_block=(nkb == 1)),
        grid=(batch, nqb, nkb), in_specs=in_specs,
        out_specs=pl.BlockSpec((tqb, H_A * V_DIM), lambda b, qi, ki: (b * nqb + qi, 0)),
        out_shape=jax.ShapeDtypeStruct((q_rows, H_A * V_DIM), bf16),
        scratch_shapes=[pltpu.VMEM((nqs, H_A, tq), f32), pltpu.VMEM((nqs, H_A, tq), f32),
                        pltpu.VMEM((nqs, H_A * V_DIM, tq), f32),
                        ] + [pltpu.VMEM((H_A, kc, tq), f32)] * ATTN_RING,
        compiler_params=_params(("parallel", "parallel", "arbitrary")), name="attention",
    )(qT, k, vT, k_tail, vT_tail)


def _mlstm_gates(gT, *, reverse):
    r = lax.broadcasted_iota(jnp.int32, (CHUNK, CHUNK), 0)
    c = lax.broadcasted_iota(jnp.int32, (CHUNK, CHUNK), 1)
    tri = jnp.where((r >= c) if reverse else (r <= c), 1.0, 0.0).astype(bf16)
    a = gT.astype(bf16)
    rem = gT - a.astype(f32)
    b = rem.astype(bf16)
    x = _dot(a, tri) + _dot(b, tri) + _dot((rem - b.astype(f32)).astype(bf16), tri)
    padded = jnp.concatenate([x, jnp.zeros((LANES - x.shape[0], CHUNK), f32)], axis=0)
    return padded.T, x


def _mlstm_head(h, qk, cums, q, kT, v, gT, c_sc, m_sc, *, reverse):
    d = 1 if reverse else 0
    bcol_all, brow_all = cums
    r = lax.broadcasted_iota(jnp.int32, (CHUNK, CHUNK), 0)
    c = lax.broadcasted_iota(jnp.int32, (CHUNK, CHUNK), 1)
    mask = (c >= r) if reverse else (c <= r)
    ones_blk = jnp.ones((CHUNK, LANES), bf16)
    tile_shape = (SUBLANES, LANES)
    ji = d * H_B + h
    jf = 2 * H_B + d * H_B + h
    bcb = jnp.broadcast_to(bcol_all[:, jf:jf + 1], (CHUNK, LANES))
    br = brow_all[jf:jf + 1, :]
    a_row = gT[ji:ji + 1, :] - br
    gtot = jnp.broadcast_to(br[:, 0:1] if reverse else br[:, CHUNK - 1:CHUNK], tile_shape)
    m_prev = m_sc[h]
    mpb = jnp.concatenate([m_prev] * (CHUNK // SUBLANES), axis=0)
    qh = q[:, h * DH_B:(h + 1) * DH_B]
    kTh = kT[h * DH_B:(h + 1) * DH_B, :]
    v_aug = jnp.concatenate([v[:, h * DH_B:(h + 1) * DH_B], ones_blk], axis=1)
    amask = jnp.where(mask, a_row, -jnp.inf)
    mt = jnp.maximum(jnp.broadcast_to(jnp.max(amask, axis=1, keepdims=True), (CHUNK, LANES)), mpb)
    s = (qk * jnp.exp(amask - mt)).astype(bf16)
    e_in = jnp.exp(mpb - mt)
    c_prev = c_sc[h]
    numden = _dot(s, v_aug) + jnp.concatenate([e_in, e_in], axis=1) * _dot(qh, c_prev.astype(bf16))
    den = numden[:, DH_B:]
    out = numden[:, :DH_B] * (1.0 / jnp.maximum(jnp.abs(den), jnp.exp(-(bcb + mt))))
    a_end = gtot[0:1, :] + a_row
    m_loc = jnp.broadcast_to(jnp.max(a_end, axis=1, keepdims=True), tile_shape)
    m_new = jnp.maximum(gtot + m_prev, m_loc)
    sp = jnp.exp(gtot + m_prev - m_new)
    kw = (kTh.astype(f32) * jnp.exp(a_end - m_new[0:1, :])).astype(bf16)
    rep = lambda x: jnp.concatenate([jnp.concatenate([x] * (DH_B // SUBLANES), axis=0)] * 2, axis=1)
    c_sc[h] = rep(sp) * c_prev + _dot(kw, v_aug)
    m_sc[h] = m_new
    return out


def _mlstm_chunks(loaders, stores, c_sc, m_sc, *, reverse):
    qk_head = lambda q, kT, h: _dot(q[:, h * DH_B:(h + 1) * DH_B], kT[h * DH_B:(h + 1) * DH_B, :])

    def prep_all(j):
        data = loaders[j]()
        return [data, _mlstm_gates(data[3], reverse=reverse), [qk_head(data[0], data[1], h) for h in range(H_B)]]

    ready = [prep_all(j) for j in range(min(MLSTM_LOOKAHEAD, len(loaders)))]
    for i, store in enumerate(stores):
        (q, kT, v, gT), cums, qks = ready.pop(0)
        j = i + MLSTM_LOOKAHEAD
        nxt = None
        if j < len(loaders):
            data = loaders[j]()
            nxt = [data, _mlstm_gates(data[3], reverse=reverse), []]
            ready.append(nxt)
        outs = []
        for h in range(H_B):
            if nxt is not None:
                nxt[2].append(qk_head(nxt[0][0], nxt[0][1], h))
            outs.append(_mlstm_head(h, qks[h], cums, q, kT, v, gT, c_sc, m_sc, reverse=reverse))
        store(jnp.concatenate(outs, axis=1))


def _mlstm_kernel(q_ref, kT_ref, v_ref, gT_ref, qt_ref, kTt_ref, vt_ref, gTt_ref, *rest, reverse):
    if reverse:
        hf_ref, hft_ref, o_ref, ot_ref, c_sc, m_sc = rest
    else:
        o_ref, ot_ref, c_sc, m_sc = rest
    j = pl.program_id(1)
    nj = pl.num_programs(1)
    nch = kT_ref.shape[0]
    run = functools.partial(_mlstm_chunks, c_sc=c_sc, m_sc=m_sc, reverse=reverse)

    def tail_chunk():
        def store(o):
            ot_ref[...] = ((o + hft_ref[...].astype(f32)) if reverse else o).astype(bf16)
        run([lambda: (qt_ref[...], kTt_ref[0], vt_ref[...], gTt_ref[0])], [store])

    @pl.when(j == 0)
    def _():
        c_sc[...] = jnp.zeros(c_sc.shape, f32)
        m_sc[...] = jnp.full(m_sc.shape, NEG, f32)
        if not reverse:
            tail_chunk()

    def loader(c):
        rows = slice(c * CHUNK, (c + 1) * CHUNK)
        return lambda: (q_ref[rows, :], kT_ref[c], v_ref[rows, :], gT_ref[c])

    def storer(c):
        rows = slice(c * CHUNK, (c + 1) * CHUNK)

        def store(o):
            o_ref[rows, :] = ((o + hf_ref[rows, :].astype(f32)) if reverse else o).astype(bf16)
        return store

    order = list(range(nch - 1, -1, -1) if reverse else range(nch))
    run([loader(c) for c in order], [storer(c) for c in order])

    if reverse:
        @pl.when(j == nj - 1)
        def _():
            tail_chunk()


def _mlstm(x_arrs, t_arrs, fwd_out=None, *, batch, seq, rb):
    reverse = fwd_out is not None
    nj = seq // rb
    nch = rb // CHUNK
    blk = (lambda b, j: b * nj + (nj - 1 - j)) if reverse else (lambda b, j: b * nj + j)
    in_specs = [
        pl.BlockSpec((rb, H_B * DH_B), lambda b, j: (blk(b, j), 0)),
        pl.BlockSpec((nch, H_B * DH_B, CHUNK), lambda b, j: (blk(b, j), 0, 0)),
        pl.BlockSpec((rb, H_B * DH_B), lambda b, j: (blk(b, j), 0)),
        pl.BlockSpec((nch, 4 * H_B, CHUNK), lambda b, j: (blk(b, j), 0, 0)),
        pl.BlockSpec((TAIL, H_B * DH_B), lambda b, j: (b, 0)),
        pl.BlockSpec((1, H_B * DH_B, CHUNK), lambda b, j: (b, 0, 0)),
        pl.BlockSpec((TAIL, H_B * DH_B), lambda b, j: (b, 0)),
        pl.BlockSpec((1, 4 * H_B, CHUNK), lambda b, j: (b, 0, 0)),
    ]
    out_specs = (pl.BlockSpec((rb, H_B * DH_B), lambda b, j: (blk(b, j), 0)),
                 pl.BlockSpec((TAIL, H_B * DH_B), lambda b, j: (b, 0)))
    out_shape = (jax.ShapeDtypeStruct((batch * seq, H_B * DH_B), bf16),
                 jax.ShapeDtypeStruct((batch * TAIL, H_B * DH_B), bf16))
    operands = tuple(x_arrs) + tuple(t_arrs)
    if reverse:
        in_specs = in_specs + list(out_specs)
        operands = operands + tuple(fwd_out)
    return pl.pallas_call(
        functools.partial(_mlstm_kernel, reverse=reverse),
        grid=(batch, nj), in_specs=in_specs, out_specs=out_specs, out_shape=out_shape,
        scratch_shapes=[pltpu.VMEM((H_B, DH_B, 2 * DH_B), f32), pltpu.VMEM((H_B, SUBLANES, LANES), f32)],
        compiler_params=_params(("parallel", "arbitrary")), name="mlstm_bwd" if reverse else "mlstm_fwd",
    )(*operands)


def _merge_kernel(attn_ref, hm_ref, so_ref, sga_ref, sgb_ref, res_ref, mn_ref, woa_ref, wom_ref, wout_ref, o_ref):
    hm = hm_ref[...].astype(f32)
    mn = mn_ref[...]
    parts = [_rms(hm[:, h * DH_B:(h + 1) * DH_B], mn[:, h * DH_B:(h + 1) * DH_B]) for h in range(H_B)]
    hmn = (jnp.concatenate(parts, axis=1) * so_ref[...].astype(f32)).astype(bf16)
    y = (sga_ref[...].astype(f32) * _dot(attn_ref[...], woa_ref[...])
         + sgb_ref[...].astype(f32) * _dot(hmn, wom_ref[...]))
    o_ref[...] = res_ref[...] + _dot(y.astype(bf16), wout_ref[...])


def _merge(attn, hm, so, sga, sgb, res, w, *, tm):
    rows = attn.shape[0]
    row_spec = lambda width: pl.BlockSpec((tm, width), lambda i: (i, 0))
    if res.shape[0] != rows:
        in_specs = [row_spec(H_A * V_DIM), row_spec(H_B * DH_B), row_spec(H_B * DH_B),
                    row_spec(D_MODEL), row_spec(D_MODEL), pl.BlockSpec((tm, D_MODEL), lambda i: (0, 0)),
                    _const_spec((1, H_B * DH_B)), _const_spec(w["woa"].shape), _const_spec(w["wom"].shape),
                    _const_spec(w["wout"].shape)]
        return pl.pallas_call(
            _merge_kernel, grid=(rows // tm,), in_specs=in_specs, out_specs=row_spec(D_MODEL),
            out_shape=jax.ShapeDtypeStruct((rows, D_MODEL), f32),
            compiler_params=_params(("parallel",)), name="merge",
        )(attn, hm, so, sga, sgb, res, w["mlstm_norm"], w["woa"], w["wom"], w["wout"])

    deep_spec = lambda width: pl.BlockSpec((tm, width), lambda i: (i, 0), pipeline_mode=pl.Buffered(3))

    def outer(attn_hbm, hm_hbm, so_hbm, sga_hbm, sgb_hbm, res_hbm, mn_ref, woa_ref, wom_ref, wout_ref, o_hbm):
        def body(attn_ref, hm_ref, so_ref, sga_ref, sgb_ref, res_ref, o_ref):
            _merge_kernel(attn_ref, hm_ref, so_ref, sga_ref, sgb_ref, res_ref, mn_ref, woa_ref, wom_ref, wout_ref,
                          o_ref)
        pltpu.emit_pipeline(
            body, grid=(rows // tm,),
            in_specs=[deep_spec(H_A * V_DIM), deep_spec(H_B * DH_B), deep_spec(H_B * DH_B),
                      deep_spec(D_MODEL), deep_spec(D_MODEL), deep_spec(D_MODEL)],
            out_specs=[row_spec(D_MODEL)],
        )(attn_hbm, hm_hbm, so_hbm, sga_hbm, sgb_hbm, res_hbm, o_hbm)

    hbm = pl.BlockSpec(memory_space=pl.ANY)
    vmem = pl.BlockSpec(memory_space=pltpu.VMEM)
    return pl.pallas_call(
        outer, in_specs=[hbm] * 6 + [vmem] * 4, out_specs=hbm,
        out_shape=jax.ShapeDtypeStruct((rows, D_MODEL), f32),
        compiler_params=pltpu.CompilerParams(vmem_limit_bytes=VMEM_LIMIT), name="merge",
    )(attn, hm, so, sga, sgb, res, w["mlstm_norm"], w["woa"], w["wom"], w["wout"])


def _gelu_tanh(x):
    return 0.5 * x * (1.0 + jnp.tanh(math.sqrt(2.0 / math.pi) * (x + 0.044715 * (x * x * x))))


def _ffn_kernel(h_ref, prev_ref, prevt_ref, next_ref, nf_ref, wup_ref, wgate_ref, cw_ref, cb_ref, wdown_ref,
                nfin_ref, o_ref, *, tiles_per_batch):
    i = pl.program_id(0)
    tm = h_ref.shape[0]
    first = (i % tiles_per_batch) == 0
    last = (i % tiles_per_batch) == tiles_per_batch - 1
    nf = nf_ref[...]
    h = h_ref[...]
    prev = jnp.where(first, prevt_ref[...], prev_ref[...])
    u_ext = _rms(jnp.concatenate([prev, h, next_ref[...]], axis=0), nf)
    u = u_ext[SUBLANES:tm + SUBLANES, :].astype(bf16)
    u_ext = u_ext.astype(bf16)
    row = lax.broadcasted_iota(jnp.int32, (tm, 1), 0)

    def up_gate(lo, hi):
        return _dot(u_ext, wup_ref[:, lo:hi]), _dot(u, wgate_ref[:, lo:hi])

    acc = h
    ahead = [up_gate(*FF_CHUNKS[j]) for j in range(FF_LOOKAHEAD)]
    for ci, (lo, hi) in enumerate(FF_CHUNKS):
        if ci + FF_LOOKAHEAD < len(FF_CHUNKS):
            ahead.append(up_gate(*FF_CHUNKS[ci + FF_LOOKAHEAD]))
        a_ext, gate = ahead.pop(0)
        a = a_ext[SUBLANES:tm + SUBLANES, :]
        a_prev = a_ext[SUBLANES - 1:SUBLANES, :]
        a_next = jnp.where(last, 0.0, a_ext[tm + SUBLANES:tm + SUBLANES + 1, :])
        a_dn = jnp.where(row == 0, a_prev, pltpu.roll(a, 1, axis=0))
        a_up = jnp.where(row == tm - 1, a_next, pltpu.roll(a, tm - 1, axis=0))
        cw = cw_ref[:, lo:hi]
        conv = a_dn * cw[0:1, :] + a * cw[1:2, :] + a_up * cw[2:3, :] + cb_ref[:, lo:hi]
        gated = (_gelu_tanh(conv) * gate).astype(bf16)
        acc = acc + _dot(gated, wdown_ref[lo:hi, :])
    o_ref[...] = _rms(acc, nfin_ref[...])


def _ffn(h1, h1_tail, w, *, seq, tm):
    rows = h1.shape[0]
    n = rows // tm
    tpb = seq // tm
    hb = tm // SUBLANES
    in_specs = [
        pl.BlockSpec((tm, D_MODEL), lambda i: (i, 0)),
        pl.BlockSpec((SUBLANES, D_MODEL), lambda i: (jnp.maximum(i * hb - 1, 0), 0)),
        pl.BlockSpec((SUBLANES, D_MODEL), lambda i: ((i // tpb) * (TAIL // SUBLANES) + N_META // SUBLANES - 1, 0)),
        pl.BlockSpec((SUBLANES, D_MODEL), lambda i: (jnp.minimum((i + 1) * hb, n * hb - 1), 0)),
        _const_spec((1, D_MODEL)), _const_spec(w["wup"].shape), _const_spec(w["wgate"].shape),
        _const_spec((SUBLANES, D_FF)), _const_spec((1, D_FF)), _const_spec(w["wdown"].shape),
        _const_spec((1, D_MODEL)),
    ]
    return pl.pallas_call(
        functools.partial(_ffn_kernel, tiles_per_batch=tpb),
        grid=(n,), in_specs=in_specs, out_specs=pl.BlockSpec((tm, D_MODEL), lambda i: (i, 0)),
        out_shape=jax.ShapeDtypeStruct((rows, D_MODEL), f32),
        compiler_params=_params(("parallel",)), name="conv_ffn",
    )(h1, h1, h1_tail, h1, w["norm_ffn"], w["wup"], w["wgate"], w["conv_w"], w["conv_b"], w["wdown"],
      w["norm_final"])


def _prep_weights(norm_mix, w_in, q_norm, w_uq, kv_norm, w_ukv, b_igate, b_fgate, mlstm_norm, w_o_attn, w_o_mlstm,
                  w_out, norm_ffn, w_up, w_gate, conv_w, conv_b, w_down, norm_final):
    w = w_in[0]
    offs = [0]
    for s in IN_SIZES:
        offs.append(offs[-1] + s)
    col = lambda j: w[:, offs[j]:offs[j + 1]]
    zeros = lambda r, c: jnp.zeros((r, c), f32)
    w_kr = col(2)
    half = ROPE // 2
    kr_main = jnp.concatenate([zeros(D_MODEL, NOPE), w_kr, zeros(D_MODEL, HEAD_PAD - QK_DIM)], axis=1)
    kr_rot = jnp.concatenate([zeros(D_MODEL, NOPE), -w_kr[:, half:], w_kr[:, :half],
                              zeros(D_MODEL, HEAD_PAD - QK_DIM)], axis=1)
    wa = jnp.concatenate([col(0), col(1), kr_main, kr_rot], axis=1)
    wb = jnp.concatenate([col(3), col(4), col(5), col(6), col(7), col(8), zeros(D_MODEL, LANES - 4 * H_B)], axis=1)
    wc = jnp.concatenate([col(9), col(10)], axis=1)
    wuq = w_uq[0].reshape(Q_LORA, H_A, QK_DIM)
    nope, rope = wuq[:, :, :NOPE], wuq[:, :, NOPE:]
    zq = lambda c: jnp.zeros((Q_LORA, H_A, c), f32)
    q_main = jnp.concatenate([nope, rope, zq(HEAD_PAD - QK_DIM)], axis=-1).reshape(Q_LORA, H_A * HEAD_PAD)
    q_rot = jnp.concatenate([zq(NOPE), -rope[:, :, half:], rope[:, :, :half], zq(HEAD_PAD - QK_DIM)],
                            axis=-1).reshape(Q_LORA, H_A * HEAD_PAD)
    wq = jnp.concatenate([q_main, q_rot], axis=1)
    wukv = w_ukv[0].reshape(KV_LORA, H_A, NOPE + V_DIM)
    wk = jnp.concatenate([wukv[:, :, :NOPE], jnp.zeros((KV_LORA, H_A, HEAD_PAD - NOPE), f32)],
                         axis=-1).reshape(KV_LORA, H_A * HEAD_PAD)
    wv = wukv[:, :, NOPE:].reshape(KV_LORA, H_A * V_DIM)
    wkv = jnp.concatenate([wk, wv], axis=1)
    gbias = jnp.concatenate([b_igate[0].reshape(-1), b_fgate[0].reshape(-1), jnp.zeros((LANES - 4 * H_B,), f32)])
    cw = jnp.concatenate([conv_w[0], jnp.zeros((SUBLANES - CONV_W, D_FF), f32)], axis=0)
    return dict(
        norm_mix=norm_mix[0][None], wa=wa.astype(bf16), wb=wb.astype(bf16), wc=wc.astype(bf16),
        q_norm=q_norm[0][None], wq=wq.astype(bf16), kv_norm=kv_norm[0][None], wkv=wkv.astype(bf16),
        gbias=gbias[None], mlstm_norm=mlstm_norm[0][None], woa=w_o_attn[0].astype(bf16),
        wom=w_o_mlstm[0].astype(bf16), wout=w_out[0].astype(bf16), norm_ffn=norm_ffn[0][None],
        wup=w_up[0].astype(bf16), wgate=w_gate[0].astype(bf16), conv_w=cw, conv_b=conv_b[0][None],
        wdown=w_down[0].astype(bf16), norm_final=norm_final[None])


def _rope_tables(positions):
    inv = ROPE_THETA ** (-jnp.arange(0, ROPE, 2, dtype=f32) / ROPE)
    ang = positions.astype(f32)[:, None] * inv[None, :]
    n = positions.shape[0]
    cos, sin = jnp.cos(ang), jnp.sin(ang)
    cm = jnp.concatenate([jnp.ones((n, NOPE), f32), cos, cos, jnp.zeros((n, HEAD_PAD - QK_DIM), f32)], axis=1)
    sm = jnp.concatenate([jnp.zeros((n, NOPE), f32), sin, sin, jnp.zeros((n, HEAD_PAD - QK_DIM), f32)], axis=1)
    return cm, sm


def _tiles(batch, seq):
    tm = math.gcd(seq, ROW_TILE)
    tm_tail = math.gcd(batch * TAIL, ROW_TILE)
    tkb = math.gcd(seq, 2048)
    tqb = math.gcd(seq, 2048)
    return dict(tm=tm, tm_wide=math.gcd(seq, 1024), tm_tail=tm_tail, tqb=tqb, tkb=tkb, rb=math.gcd(seq, 2048))


def _position_tables(max_seq):
    return _rope_tables(N_META + jnp.arange(max_seq)) + _rope_tables(jnp.tile(jnp.arange(TAIL), ROW_TILE // TAIL))


def _encode(x, meta_tokens, w, tables):
    batch, seq, _ = x.shape
    t = _tiles(batch, seq)
    xf = x.reshape(batch * seq, D_MODEL)
    tail_one = jnp.concatenate([meta_tokens.astype(f32), jnp.zeros((TAIL - N_META, D_MODEL), f32)], axis=0)
    tail_tile = jnp.tile(tail_one, (t["tm_tail"] // TAIL, 1))

    cm_x, sm_x, cm_t, sm_t = tables
    nper = seq // t["tm"]
    xs = _inproj(xf, cm_x, sm_x, lambda i: (i % nper, 0), w, rows=batch * seq, tm=t["tm"], cq=TQ, ck=KC,
                 mask_tail=False)
    ts = _inproj(tail_tile, cm_t, sm_t, lambda i: (0, 0), w, rows=batch * TAIL, tm=t["tm_tail"], cq=TAIL, ck=TAIL,
                 mask_tail=True)
    qT, k, vT, mq, mkT, mv, so, gT, sga, sgb = xs
    qT_t, k_t, vT_t, mq_t, mkT_t, mv_t, so_t, gT_t, sga_t, sgb_t = ts

    attn = _attention(qT, k, vT, k_t, vT_t, batch=batch, seq=seq, tqb=t["tqb"], tkb=t["tkb"], q_is_tail=False)
    attn_t = _attention(qT_t, k, vT, k_t, vT_t, batch=batch, seq=seq, tqb=TAIL, tkb=t["tkb"], q_is_tail=True)

    x_arrs = (mq, mkT, mv, gT)
    t_arrs = (mq_t, mkT_t, mv_t, gT_t)
    fwd = _mlstm(x_arrs, t_arrs, batch=batch, seq=seq, rb=t["rb"])
    hm, hm_t = _mlstm(x_arrs, t_arrs, fwd, batch=batch, seq=seq, rb=t["rb"])

    h1 = _merge(attn, hm, so, sga, sgb, xf, w, tm=t["tm_wide"])
    h1_t = _merge(attn_t, hm_t, so_t, sga_t, sgb_t, tail_tile, w, tm=t["tm_tail"])
    y = _ffn(h1, h1_t, w, seq=seq, tm=t["tm_wide"])
    return y.reshape(batch, seq, D_MODEL)


def kernel(x_prompt, x_sample, meta_tokens, norm_mix, w_in, q_norm, w_uq, kv_norm, w_ukv, b_igate, b_fgate,
           mlstm_norm, w_o_attn, w_o_mlstm, w_out, norm_ffn, w_up, w_gate, conv_w, conv_b, w_down, norm_final):
    w = _prep_weights(norm_mix, w_in, q_norm, w_uq, kv_norm, w_ukv, b_igate, b_fgate, mlstm_norm, w_o_attn,
                      w_o_mlstm, w_out, norm_ffn, w_up, w_gate, conv_w, conv_b, w_down, norm_final)
    tables = _position_tables(max(x_prompt.shape[1], x_sample.shape[1]))
    return (_encode(x_prompt, meta_tokens, w, tables), _encode(x_sample, meta_tokens, w, tables))
```
